```python
import math
import jax
import jax.numpy as jnp
from jax import lax
import numpy as np

D_MODEL = 2048
BATCH = 2
SEQ = 8192
DEPTH = 2

CTX_LEN = 256
GRID_W = 64

ATT_WIDTH = D_MODEL // 2
CONV_WIDTH = D_MODEL // 4
HGRN_WIDTH = D_MODEL // 4
MIX_WIDTH = ATT_WIDTH + CONV_WIDTH + HGRN_WIDTH

ATT_HEAD_DIM = 64
ATT_HEADS = ATT_WIDTH // (2 * ATT_HEAD_DIM)
ROPE_THETA = 10000.0
Q_BLOCK = 128

CONV_K = 3

HGRN_HEAD_DIM = 128
HGRN_HEADS = HGRN_WIDTH // HGRN_HEAD_DIM
HGRN_CHUNK = 64

N_EXPERTS = 16
EXPERT_FF = D_MODEL // 2
EC_CAPACITY_FACTOR = 2

PROJ_SIZES = (ATT_WIDTH,) * 3 + (CONV_WIDTH,) * 3 + (HGRN_WIDTH,) * 5
PROJ_WIDTH = sum(PROJ_SIZES)

ALPHA = (2.0 * DEPTH) ** 0.25
BETA = (8.0 * DEPTH) ** -0.25
EPS = 1e-6

kernel_name = 'hybrid_diffusion_diffattn_conv_hgrn2_ecmoe'


def layer_norm(x, w, b):
    xf = x.astype(jnp.float32)
    mu = jnp.mean(xf, axis=-1, keepdims=True)
    var = jnp.mean(jnp.square(xf - mu), axis=-1, keepdims=True)
    return ((xf - mu) * lax.rsqrt(var + EPS) * w + b).astype(x.dtype)


def rms_norm(x, w):
    xf = x.astype(jnp.float32)
    return (xf * lax.rsqrt(jnp.mean(jnp.square(xf), axis=-1, keepdims=True) + EPS) * w).astype(x.dtype)


def split_proj(p):
    offsets = np.cumsum(PROJ_SIZES)[:-1].tolist()
    return jnp.split(p, offsets, axis=-1)


def axial_rope(t, row, col):
    half = ATT_HEAD_DIM // 2
    nf = half // 2
    inv = ROPE_THETA ** (-jnp.arange(nf, dtype=jnp.float32) / nf)
    tf = t.astype(jnp.float32)

    def rot(part, pos):
        ang = pos[:, None] * inv
        cs = jnp.cos(ang)[:, None, None, :]
        sn = jnp.sin(ang)[:, None, None, :]
        p1, p2 = part[..., :nf], part[..., nf:]
        return jnp.concatenate([p1 * cs - p2 * sn, p2 * cs + p1 * sn], axis=-1)

    out = jnp.concatenate([rot(tf[..., :half], row), rot(tf[..., half:], col)], axis=-1)
    return out.astype(t.dtype)


def diff_attn(q, k, v, lam):
    s = jnp.einsum('bqhsd,bkhsd->bhsqk', q, k).astype(jnp.float32)
    p = jax.nn.softmax(s, axis=-1)
    a = p[:, :, 0] - lam * p[:, :, 1]
    return jnp.einsum('bhqk,bkhe->bqhe', a.astype(v.dtype), v)


def diff_attn_latent(q, k_all, v_all, lam):
    b, n = q.shape[:2]
    nb = n // Q_BLOCK
    qb = jnp.moveaxis(q.reshape(b, nb, Q_BLOCK, ATT_HEADS, 2, ATT_HEAD_DIM), 1, 0)
    ob = lax.map(lambda blk: diff_attn(blk, k_all, v_all, lam), qb)
    return jnp.moveaxis(ob, 0, 1).reshape(b, n, ATT_HEADS, 2 * ATT_HEAD_DIM)


def centred_conv(u, w):
    up = jnp.pad(u, ((0, 0), (1, 1), (0, 0)))
    return up[:, :-2] * w[0] + up[:, 1:-1] * w[1] + up[:, 2:] * w[2]


def forget_gate(z, lb):
    zf = z.astype(jnp.float32)
    logf = jnp.logaddexp(jnp.log(lb), jnp.log1p(-lb) + jax.nn.log_sigmoid(zf))
    key_in = (1.0 - lb) * jax.nn.sigmoid(-zf)
    return logf, key_in


def hgrn_chunk_scan(q, k, v, logf, s0):
    b, l, h, _ = q.shape
    nc = l // HGRN_CHUNK

    def to_chunks(t):
        return t.reshape(b, nc, HGRN_CHUNK, h, t.shape[-1]).transpose(1, 0, 3, 2, 4)

    mask = jnp.tril(jnp.ones((HGRN_CHUNK, HGRN_CHUNK), dtype=bool))[:, :, None]

    def step(s, inp):
        qc, kc, vc, lc = inp
        a = jnp.cumsum(lc, axis=-2)
        inter = jnp.einsum('bhid,bhde->bhie', qc * jnp.exp(a), s)
        rel = jnp.where(mask, a[:, :, :, None, :] - a[:, :, None, :, :], -jnp.inf)
        scores = jnp.einsum('bhid,bhjd,bhijd->bhij', qc, kc, jnp.exp(rel))
        intra = jnp.einsum('bhij,bhje->bhie', scores, vc)
        a_last = a[:, :, -1:, :]
        s_new = jnp.exp(a_last[:, :, 0, :])[..., None] * s + jnp.einsum('bhjd,bhje->bhde', kc * jnp.exp(a_last - a), vc)
        return s_new, inter + intra

    s_fin, o = lax.scan(step, s0, (to_chunks(q), to_chunks(k), to_chunks(v), to_chunks(logf)))
    o = o.transpose(1, 0, 3, 2, 4).reshape(b, l, h, v.shape[-1])
    return o, s_fin


def hgrn_direction(ctx_in, lat_in, reverse):
    flip = (lambda t: jnp.flip(t, axis=1)) if reverse else (lambda t: t)
    qc = ctx_in[0]
    s0 = jnp.zeros((qc.shape[0], HGRN_HEADS, HGRN_HEAD_DIM, HGRN_HEAD_DIM), jnp.float32)
    o_c, s_c = hgrn_chunk_scan(*[flip(t) for t in ctx_in], s0)
    o_l, _ = hgrn_chunk_scan(*[flip(t) for t in lat_in], s_c)
    return flip(o_c), flip(o_l)


def expert_choice_moe(h, w_router, w_gate, w_up, w_down):
    b, n, _ = h.shape
    cap = EC_CAPACITY_FACTOR * n // N_EXPERTS
    aff = jax.nn.softmax((h @ w_router).astype(jnp.float32), axis=-1)
    gates, idx = lax.top_k(jnp.swapaxes(aff, 1, 2), cap)
    xg = jax.vmap(lambda hb, ib: hb[ib])(h, idx)
    a = jnp.einsum('becd,edf->becf', xg, w_gate)
    u = jnp.einsum('becd,edf->becf', xg, w_up)
    y = jnp.einsum('becf,efd->becd', jax.nn.silu(a) * u, w_down) * gates[..., None].astype(h.dtype)
    bidx = jnp.arange(b)[:, None, None]
    return jnp.zeros_like(h).at[bidx, idx].add(y)


def hybrid_mixer(h, hc, w_in, w_conv, lambda_qk, lambda_init, subln_w, lb_fwd, lb_bwd, hgrn_norm_w,
                 w_out, row, col, need_ctx):
    b, n, _ = h.shape
    m = hc.shape[1]
    aq, ak, av, cx, cb, cc, gq, gi, gg, gff, gfb = split_proj(h @ w_in)
    aqc, akc, avc, cxc, cbc, ccc, gqc, gic, ggc, gffc, gfbc = split_proj(hc @ w_in)

    qk_heads = lambda t: t.reshape(t.shape[0], t.shape[1], ATT_HEADS, 2, ATT_HEAD_DIM)
    v_heads = lambda t: t.reshape(t.shape[0], t.shape[1], ATT_HEADS, 2 * ATT_HEAD_DIM)
    scale = ATT_HEAD_DIM ** -0.5
    lq = lambda_qk.astype(jnp.float32)
    lam = jnp.exp(jnp.sum(lq[0] * lq[1])) - jnp.exp(jnp.sum(lq[2] * lq[3])) + lambda_init
    q = axial_rope(qk_heads(aq), row, col) * scale
    k = axial_rope(qk_heads(ak), row, col)
    kc = qk_heads(akc)
    vc = v_heads(avc)
    k_all = jnp.concatenate([k, kc], axis=1)
    v_all = jnp.concatenate([v_heads(av), vc], axis=1)
    att = rms_norm(diff_attn_latent(q, k_all, v_all, lam), subln_w) * (1.0 - lambda_init)

    conv = cb * centred_conv(cc * cx, w_conv)

    gh = lambda t: t.reshape(t.shape[0], t.shape[1], HGRN_HEADS, HGRN_HEAD_DIM)
    lbf = lb_fwd.reshape(HGRN_HEADS, HGRN_HEAD_DIM)
    lbb = lb_bwd.reshape(HGRN_HEADS, HGRN_HEAD_DIM)
    lf_f, k_f = forget_gate(gh(gff), lbf)
    lf_b, k_b = forget_gate(gh(gfb), lbb)
    lf_fc, k_fc = forget_gate(gh(gffc), lbf)
    lf_bc, k_bc = forget_gate(gh(gfbc), lbb)
    q_l, v_l = gh(gq).astype(jnp.float32), gh(gi).astype(jnp.float32)
    q_c, v_c = gh(gqc).astype(jnp.float32), gh(gic).astype(jnp.float32)
    of_c, of_l = hgrn_direction((q_c, k_fc, v_c, lf_fc), (q_l, k_f, v_l, lf_f), reverse=False)
    ob_c, ob_l = hgrn_direction((q_c, k_bc, v_c, lf_bc), (q_l, k_b, v_l, lf_b), reverse=True)
    rec = rms_norm(of_l + ob_l, hgrn_norm_w) * jax.nn.silu(gh(gg).astype(jnp.float32))

    y = jnp.concatenate([att.reshape(b, n, ATT_WIDTH), conv,
                         rec.reshape(b, n, HGRN_WIDTH).astype(h.dtype)], axis=-1) @ w_out
    if not need_ctx:
        return y, None

    att_c = rms_norm(diff_attn(qk_heads(aqc) * scale, kc, vc, lam), subln_w) * (1.0 - lambda_init)
    conv_c = cbc * centred_conv(ccc * cxc, w_conv)
    rec_c = rms_norm(of_c + ob_c, hgrn_norm_w) * jax.nn.silu(gh(ggc).astype(jnp.float32))
    y_c = jnp.concatenate([att_c.reshape(b, m, ATT_WIDTH), conv_c,
                           rec_c.reshape(b, m, HGRN_WIDTH).astype(hc.dtype)], axis=-1) @ w_out
    return y, y_c


def setup_inputs(seed: int = 0) -> dict:
    key = jax.random.key(seed)
    ks = jax.random.split(key, 20)
    d = D_MODEL

    def nrm(k, shape, s):
        return jax.random.normal(k, shape, jnp.float32) * s

    return {
        'x': nrm(ks[0], (BATCH, SEQ, d), 1.0),
        'c': nrm(ks[1], (BATCH, d), 1.0),
        'ctx': nrm(ks[2], (BATCH, CTX_LEN, d), 1.0),
        'c_ctx': nrm(ks[3], (d,), 1.0),
        'w_mod': nrm(ks[4], (DEPTH, d, 6 * d), 0.5 * d ** -0.5),
        'b_mod': nrm(ks[5], (DEPTH, 6 * d), 0.02),
        'w_in': nrm(ks[6], (DEPTH, d, PROJ_WIDTH), d ** -0.5),
        'w_conv': nrm(ks[7], (DEPTH, CONV_K, CONV_WIDTH), CONV_K ** -0.5),
        'lambda_qk': nrm(ks[8], (DEPTH, 4, ATT_HEAD_DIM), 0.1),
        'subln_w': 1.0 + nrm(ks[9], (DEPTH, 2 * ATT_HEAD_DIM), 0.02),
        'hgrn_lb_logits': nrm(ks[10], (2, DEPTH, HGRN_WIDTH), 0.1),
        'hgrn_norm_w': 1.0 + nrm(ks[11], (DEPTH, HGRN_HEAD_DIM), 0.02),
        'w_out': nrm(ks[12], (DEPTH, MIX_WIDTH, d), BETA * MIX_WIDTH ** -0.5),
        'ln_w': 1.0 + nrm(ks[13], (DEPTH, 2, d), 0.02),
        'ln_b': nrm(ks[14], (DEPTH, 2, d), 0.02),
        'w_router': nrm(ks[15], (DEPTH, d, N_EXPERTS), d ** -0.5),
        'w_gate': nrm(ks[16], (DEPTH, N_EXPERTS, d, EXPERT_FF), d ** -0.5),
        'w_up': nrm(ks[17], (DEPTH, N_EXPERTS, d, EXPERT_FF), d ** -0.5),
        'w_down': nrm(ks[18], (DEPTH, N_EXPERTS, EXPERT_FF, d), BETA * EXPERT_FF ** -0.5),
    }


def reference(x, c, ctx, c_ctx, w_mod, b_mod, w_in, w_conv, lambda_qk, subln_w, hgrn_lb_logits,
              hgrn_norm_w, w_out, ln_w, ln_b, w_router, w_gate, w_up, w_down):
    n = x.shape[1]
    rows = n // GRID_W
    row = jnp.repeat(jnp.arange(rows, dtype=jnp.float32), GRID_W)
    col = jnp.tile(jnp.arange(GRID_W, dtype=jnp.float32), rows)
    lb = jnp.cumsum(jax.nn.softmax(hgrn_lb_logits.astype(jnp.float32), axis=1), axis=1)
    lb = lb - lb[:, :1]

    for l in range(DEPTH):
        need_ctx = l < DEPTH - 1
        lambda_init = 0.8 - 0.6 * math.exp(-0.3 * l)
        mod = jax.nn.silu(c) @ w_mod[l] + b_mod[l]
        mod_c = jax.nn.silu(c_ctx) @ w_mod[l] + b_mod[l]
        sh1, sc1, g1, sh2, sc2, g2 = jnp.split(mod[:, None, :], 6, axis=-1)
        csh1, csc1, cg1, csh2, csc2, cg2 = jnp.split(mod_c, 6, axis=-1)

        y, y_c = hybrid_mixer(x * (1.0 + sc1) + sh1, ctx * (1.0 + csc1) + csh1, w_in[l], w_conv[l],
                              lambda_qk[l], lambda_init, subln_w[l], lb[0, l], lb[1, l], hgrn_norm_w[l],
                              w_out[l], row, col, need_ctx)
        x = layer_norm(ALPHA * x + g1 * y, ln_w[l, 0], ln_b[l, 0])
        moe = expert_choice_moe(x * (1.0 + sc2) + sh2, w_router[l], w_gate[l], w_up[l], w_down[l])
        x = layer_norm(ALPHA * x + g2 * moe, ln_w[l, 1], ln_b[l, 1])

        if need_ctx:
            ctx = layer_norm(ALPHA * ctx + cg1 * y_c, ln_w[l, 0], ln_b[l, 0])
            moe_c = expert_choice_moe(ctx * (1.0 + csc2) + csh2, w_router[l], w_gate[l], w_up[l], w_down[l])
            ctx = layer_norm(ALPHA * ctx + cg2 * moe_c, ln_w[l, 1], ln_b[l, 1])
    return x
```

```python
import functools
import math

import jax
import jax.numpy as jnp
from jax import lax
from jax.experimental import pallas as pl
from jax.experimental.pallas import tpu as pltpu

F32 = jnp.float32
BF16 = jnp.bfloat16
HIGHEST = lax.Precision.HIGHEST

GRID_W = 64
ROPE_THETA = 10000.0
ATT_HEAD_DIM = 64
HEAD_LANES = 128
HGRN_CHUNK = 64
EC_CAPACITY_FACTOR = 2
EPS = 1e-6
TILE = 256
IDX_SLOT_BLOCKS = 4
WIN_ALIGN = 16
VMEM_LIMIT = 56 * 1024 * 1024


def _cparams(sem):
    return pltpu.CompilerParams(dimension_semantics=sem, vmem_limit_bytes=VMEM_LIMIT)


def _silu(x):
    return x * jax.nn.sigmoid(x)


def _mod_kernel(cs_ref, w_ref, b_ref, o_ref):
    a = _silu(cs_ref[...])
    o_ref[0] = jnp.dot(a, w_ref[0], precision=HIGHEST, preferred_element_type=F32) + b_ref[0]


def _modulation(cs, w_mod, b_mod):
    depth, d, n6 = w_mod.shape
    tn = 1024
    return pl.pallas_call(
        _mod_kernel,
        grid=(depth, n6 // tn),
        in_specs=[pl.BlockSpec((8, d), lambda l, j: (0, 0)),
                  pl.BlockSpec((1, d, tn), lambda l, j: (l, 0, j)),
                  pl.BlockSpec((1, 1, tn), lambda l, j: (l, 0, j))],
        out_specs=pl.BlockSpec((1, 8, tn), lambda l, j: (l, 0, j)),
        out_shape=jax.ShapeDtypeStruct((depth, 8, n6), F32),
        compiler_params=_cparams(("arbitrary", "arbitrary")),
        name="modulation",
    )(cs, w_mod, b_mod.reshape(depth, 1, n6))


def _inproj_kernel(x_ref, mod_ref, w_ref, rc_ref, ra_ref, rb_ref, o_ref, h_scr, *, m_ctx, rope_tiles):
    i = pl.program_id(1)
    j = pl.program_id(2)
    tm = x_ref.shape[1]

    @pl.when(j == 0)
    def _():
        row = i * tm + lax.broadcasted_iota(jnp.int32, (tm, 1), 0)
        is_ctx = row < m_ctx
        mod = mod_ref[0]
        sh = jnp.where(is_ctx, mod[0, 0:1], mod[1, 0:1])
        sc = jnp.where(is_ctx, mod[0, 1:2], mod[1, 1:2])
        h_scr[...] = (x_ref[0] * (1.0 + sc) + sh).astype(BF16)

    acc = jnp.dot(h_scr[...], w_ref[...], preferred_element_type=F32)

    if rope_tiles == 0:
        o_ref[0] = acc.astype(o_ref.dtype)
        return

    @pl.when(j < rope_tiles)
    def _():
        rc, ra, rb = rc_ref[...], ra_ref[...], rb_ref[...]
        qscale = jnp.where(j == 0, ATT_HEAD_DIM ** -0.5, 1.0).astype(F32)
        for c in range(acc.shape[1] // HEAD_LANES):
            blk = acc[:, c * HEAD_LANES:(c + 1) * HEAD_LANES]
            rot = (blk * rc + pltpu.roll(blk, 16, 1) * ra + pltpu.roll(blk, HEAD_LANES - 16, 1) * rb)
            o_ref[0, :, c * HEAD_LANES:(c + 1) * HEAD_LANES] = (rot * qscale).astype(o_ref.dtype)

    @pl.when(j >= rope_tiles)
    def _():
        o_ref[0] = acc.astype(o_ref.dtype)


def _inproj(xa, modtab, w_bf, tables, *, col0, ncols, out_dtype, rope_tiles, m_ctx):
    b, l, d = xa.shape
    tn = 1024
    tm = l // 8
    joff = col0 // tn
    kern = functools.partial(_inproj_kernel, m_ctx=m_ctx, rope_tiles=rope_tiles)
    tab_spec = pl.BlockSpec((tm, HEAD_LANES), lambda bb, i, j: (i, 0))
    return pl.pallas_call(
        kern,
        grid=(b, l // tm, ncols // tn),
        in_specs=[pl.BlockSpec((1, tm, d), lambda bb, i, j: (bb, i, 0)),
                  pl.BlockSpec((1, 2, 6, d), lambda bb, i, j: (bb, 0, 0, 0)),
                  pl.BlockSpec((d, tn), lambda bb, i, j: (0, j + joff)),
                  tab_spec, tab_spec, tab_spec],
        out_specs=pl.BlockSpec((1, tm, tn), lambda bb, i, j: (bb, i, j)),
        out_shape=jax.ShapeDtypeStruct((b, l, ncols), out_dtype),
        scratch_shapes=[pltpu.VMEM((tm, d), BF16)],
        compiler_params=_cparams(("arbitrary", "arbitrary", "arbitrary")),
        name="inproj",
    )(xa, modtab, w_bf, *tables)


def _rope_tables(m_ctx, n_lat):
    nf = ATT_HEAD_DIM // 4
    inv = ROPE_THETA ** (-jnp.arange(nf, dtype=F32) / nf)
    rows = n_lat // GRID_W
    row = jnp.repeat(jnp.arange(rows, dtype=F32), GRID_W)
    col = jnp.tile(jnp.arange(GRID_W, dtype=F32), rows)
    ar, ac = row[:, None] * inv, col[:, None] * inv
    cr, sr, cc, sc = jnp.cos(ar), jnp.sin(ar), jnp.cos(ac), jnp.sin(ac)
    z = jnp.zeros_like(sr)
    c64 = jnp.concatenate([cr, cr, cc, cc], axis=1)
    a64 = jnp.concatenate([z, sr, z, sc], axis=1)
    b64 = jnp.concatenate([-sr, z, -sc, z], axis=1)
    reps = HEAD_LANES // ATT_HEAD_DIM

    def full(t64, fill):
        lat = jnp.tile(t64, (1, reps))
        return jnp.concatenate([jnp.full((m_ctx, HEAD_LANES), fill, F32), lat], axis=0)

    return full(c64, 1.0), full(a64, 0.0), full(b64, 0.0)


def _attn_kernel(lq_ref, sw_ref, q_ref, k_ref, v_ref, o_ref, m_scr, l_scr, acc_scr, *, m_ctx, tk, lambda_init):
    i = pl.program_id(2)
    tq = q_ref.shape[1]
    n_lat = k_ref.shape[1] - m_ctx
    q = q_ref[0]
    lane = lax.broadcasted_iota(jnp.int32, q.shape, 1)
    zero = jnp.zeros_like(q)
    qs = (jnp.where(lane < ATT_HEAD_DIM, q, zero), jnp.where(lane >= ATT_HEAD_DIM, q, zero))
    m_scr[...] = jnp.full(m_scr.shape, -jnp.inf, F32)
    l_scr[...] = jnp.zeros(l_scr.shape, F32)
    acc_scr[...] = jnp.zeros(acc_scr.shape, F32)

    def process(start, size):
        kc = k_ref[0, pl.ds(start, size), :]
        vc = v_ref[0, pl.ds(start, size), :]
        for s in range(2):
            sc = lax.dot_general(qs[s], kc, (((1,), (1,)), ((), ())), preferred_element_type=F32)
            m_old = m_scr[s]
            m_new = jnp.maximum(m_old, jnp.max(sc, axis=1, keepdims=True))
            alpha = jnp.exp(m_old - m_new)
            p = jnp.exp(sc - m_new)
            l_scr[s] = alpha * l_scr[s] + jnp.sum(p, axis=1, keepdims=True)
            acc_scr[s] = alpha * acc_scr[s] + jnp.dot(p.astype(BF16), vc, preferred_element_type=F32)
            m_scr[s] = m_new

    process(0, m_ctx)

    @pl.when(i * tq >= m_ctx)
    def _():
        def body(c, carry):
            process(pl.multiple_of(m_ctx + c * tk, math.gcd(m_ctx, tk)), tk)
            return carry
        lax.fori_loop(0, n_lat // tk, body, 0)

    lq = lq_ref[...]
    lam = (jnp.exp(jnp.sum(lq[0:1] * lq[1:2], axis=1, keepdims=True))
           - jnp.exp(jnp.sum(lq[2:3] * lq[3:4], axis=1, keepdims=True)) + lambda_init)
    o = acc_scr[0] / l_scr[0] - lam * (acc_scr[1] / l_scr[1])
    ms = jnp.mean(o * o, axis=1, keepdims=True)
    o = o * lax.rsqrt(ms + EPS) * sw_ref[...] * (1.0 - lambda_init)
    o_ref[0] = o.astype(o_ref.dtype)


def _attention(qkv, lambda_qk, subln_w, *, m_ctx, lambda_init):
    b, l, w3 = qkv.shape
    heads = w3 // (3 * HEAD_LANES)
    tq = TILE
    tk = 512
    kern = functools.partial(_attn_kernel, m_ctx=m_ctx, tk=tk, lambda_init=lambda_init)
    return pl.pallas_call(
        kern,
        grid=(b, heads, l // tq),
        in_specs=[pl.BlockSpec((4, ATT_HEAD_DIM), lambda bb, h, i: (0, 0)),
                  pl.BlockSpec((1, HEAD_LANES), lambda bb, h, i: (0, 0)),
                  pl.BlockSpec((1, tq, HEAD_LANES), lambda bb, h, i: (bb, i, h)),
                  pl.BlockSpec((1, l, HEAD_LANES), lambda bb, h, i: (bb, 0, heads + h)),
                  pl.BlockSpec((1, l, HEAD_LANES), lambda bb, h, i: (bb, 0, 2 * heads + h))],
        out_specs=pl.BlockSpec((1, tq, HEAD_LANES), lambda bb, h, i: (bb, i, h)),
        out_shape=jax.ShapeDtypeStruct((b, l, heads * HEAD_LANES), BF16),
        scratch_shapes=[pltpu.VMEM((2, tq, 1), F32), pltpu.VMEM((2, tq, 1), F32),
                        pltpu.VMEM((2, tq, HEAD_LANES), F32)],
        compiler_params=_cparams(("arbitrary", "arbitrary", "arbitrary")),
        name="diff_attention",
    )(lambda_qk, subln_w.reshape(1, HEAD_LANES), qkv, qkv, qkv)


def _log_sigmoid(z):
    return jnp.minimum(z, 0.0) - jnp.log1p(jnp.exp(-jnp.abs(z)))


def _forget_gate(z, lb):
    ls = _log_sigmoid(z)
    key = jax.nn.sigmoid(-z)
    if lb is None:
        return ls, key
    a = jnp.log(lb)
    b = jnp.log1p(-lb) + ls
    logf = jnp.maximum(a, b) + jnp.log1p(jnp.exp(-jnp.abs(a - b)))
    return logf, (1.0 - lb) * key


def _hgrn_direction(q_ref, v_ref, z_ref, o_ref, s_scr, lb, reverse):
    t = q_ref.shape[1]
    ch = HGRN_CHUNK
    q, v = q_ref[0], v_ref[0]
    logf, key = _forget_gate(z_ref[0], lb)
    r = lax.broadcasted_iota(jnp.int32, (t, t), 0)
    c = lax.broadcasted_iota(jnp.int32, (t, t), 1)
    tri = ((r // ch) == (c // ch)) & ((c >= r) if reverse else (c <= r))
    a = jnp.dot(tri.astype(F32), logf, precision=HIGHEST, preferred_element_type=F32)
    ri = lax.broadcasted_iota(jnp.int32, (ch, ch), 0)
    ci = lax.broadcasted_iota(jnp.int32, (ch, ch), 1)
    mask = (ci >= ri) if reverse else (ci <= ri)
    st = s_scr[...]
    nch = t // ch
    for ck in (range(nch - 1, -1, -1) if reverse else range(nch)):
        sl = slice(ck * ch, (ck + 1) * ch)
        ac, qc, kc, vc = a[sl], q[sl], key[sl], v[sl]
        a_end = ac[0:1] if reverse else ac[ch - 1:ch]
        a_mid = ac[ch // 2 - 1:ch // 2]
        qe = (qc * jnp.exp(ac - a_mid)).astype(BF16)
        ke = (kc * jnp.exp(a_mid - ac)).astype(BF16)
        sc = lax.dot_general(qe, ke, (((1,), (1,)), ((), ())), preferred_element_type=F32)
        sc = jnp.where(mask, sc, 0.0).astype(BF16)
        vb = vc.astype(BF16)
        intra = jnp.dot(sc, vb, preferred_element_type=F32)
        inter = lax.dot_general((qc * jnp.exp(ac)).astype(BF16), st.astype(BF16),
                                (((1,), (1,)), ((), ())), preferred_element_type=F32)
        o_ref[0, sl, :] = inter + intra
        kd = (kc * jnp.exp(a_end - ac)).astype(BF16)
        st = jnp.exp(a_end) * st + lax.dot_general(vb, kd, (((0,), (0,)), ((), ())), preferred_element_type=F32)
    s_scr[...] = st


def _hgrn_kernel(lbl_ref, qf_ref, vf_ref, zf_ref, qb_ref, vb_ref, zb_ref, of_ref, ob_ref, sf_scr, sb_scr, *, layer):
    @pl.when(pl.program_id(2) == 0)
    def _():
        sf_scr[...] = jnp.zeros(sf_scr.shape, F32)
        sb_scr[...] = jnp.zeros(sb_scr.shape, F32)

    if layer == 0:
        lbf = lbb = None
    else:
        lg = lbl_ref[...]
        ex = jnp.exp(lg - jnp.max(lg, axis=1, keepdims=True))
        sm = ex / jnp.sum(ex, axis=1, keepdims=True)
        lb = sm[:, 1]
        for k in range(2, layer + 1):
            lb = lb + sm[:, k]
        lbf, lbb = lb[0:1], lb[1:2]
    _hgrn_direction(qf_ref, vf_ref, zf_ref, of_ref, sf_scr, lbf, False)
    _hgrn_direction(qb_ref, vb_ref, zb_ref, ob_ref, sb_scr, lbb, True)


def _hgrn(cg, lb_logits, *, layer, hg_heads, col_q, col_i, col_ff, col_fb):
    b, l, _ = cg.shape
    t = TILE
    nb = l // t
    depth = lb_logits.shape[1]

    def fwd(col):
        return pl.BlockSpec((1, t, HEAD_LANES), lambda bb, h, i: (bb, i, col + h))

    def bwd(col):
        return pl.BlockSpec((1, t, HEAD_LANES), lambda bb, h, i: (bb, jnp.where(i == 0, 0, nb - i), col + h))

    out_f = pl.BlockSpec((1, t, HEAD_LANES), lambda bb, h, i: (bb, i, h))
    out_b = pl.BlockSpec((1, t, HEAD_LANES), lambda bb, h, i: (bb, jnp.where(i == 0, 0, nb - i), h))
    shp = jax.ShapeDtypeStruct((b, l, hg_heads * HEAD_LANES), F32)
    return pl.pallas_call(
        functools.partial(_hgrn_kernel, layer=layer),
        grid=(b, hg_heads, nb),
        in_specs=[pl.BlockSpec((2, depth, HEAD_LANES), lambda bb, h, i: (0, 0, h)),
                  fwd(col_q), fwd(col_i), fwd(col_ff), bwd(col_q), bwd(col_i), bwd(col_fb)],
        out_specs=[out_f, out_b],
        out_shape=[shp, shp],
        scratch_shapes=[pltpu.VMEM((HEAD_LANES, HEAD_LANES), F32), pltpu.VMEM((HEAD_LANES, HEAD_LANES), F32)],
        compiler_params=_cparams(("arbitrary", "arbitrary", "arbitrary")),
        name="hgrn2",
    )(lb_logits, cg, cg, cg, cg, cg, cg)


def _layer_norm(r, w, b):
    mu = jnp.mean(r, axis=1, keepdims=True)
    var = jnp.mean(jnp.square(r - mu), axis=1, keepdims=True)
    return (r - mu) * lax.rsqrt(var + EPS) * w + b


def _outproj_kernel(x_ref, att_ref, cx_ref, cb_ref, cc_ref, cxp_ref, ccp_ref, cxn_ref, ccn_ref, of_ref, ob_ref,
                    gg_ref, wconv_ref, hnw_ref, wout_ref, mod_ref, lnw_ref, lnb_ref, wr_ref,
                    x1_ref, h2_ref, aff_ref, *, ctx_tiles, alpha, n_experts):
    i = pl.program_id(1)
    nt = pl.num_programs(1)
    tm = x_ref.shape[1]
    aw = att_ref.shape[2]
    cw = cx_ref.shape[2]

    u = cc_ref[0] * cx_ref[0]
    prev_ok = jnp.logical_and(i != 0, i != ctx_tiles)
    next_ok = jnp.logical_and(i != ctx_tiles - 1, i != nt - 1)
    u_before = jnp.where(prev_ok, (ccp_ref[0] * cxp_ref[0])[7:8], 0.0)
    u_after = jnp.where(next_ok, (ccn_ref[0] * cxn_ref[0])[0:1], 0.0)
    row = lax.broadcasted_iota(jnp.int32, (tm, 1), 0)
    u_prev = jnp.where(row == 0, u_before, pltpu.roll(u, 1, 0))
    u_next = jnp.where(row == tm - 1, u_after, pltpu.roll(u, tm - 1, 0))
    wc = wconv_ref[...]
    conv = cb_ref[0] * (u_prev * wc[0:1] + u * wc[1:2] + u_next * wc[2:3])

    o = of_ref[0] + ob_ref[0]
    gg = gg_ref[0]
    recs = []
    for h in range(o.shape[1] // HEAD_LANES):
        oh = o[:, h * HEAD_LANES:(h + 1) * HEAD_LANES]
        ms = jnp.mean(oh * oh, axis=1, keepdims=True)
        recs.append(oh * lax.rsqrt(ms + EPS) * hnw_ref[...] * _silu(gg[:, h * HEAD_LANES:(h + 1) * HEAD_LANES]))
    rec = jnp.concatenate(recs, axis=1)

    y = jnp.dot(att_ref[0], wout_ref[0:aw, :], preferred_element_type=F32)
    y = y + jnp.dot(conv.astype(BF16), wout_ref[aw:aw + cw, :], preferred_element_type=F32)
    y = y + jnp.dot(rec.astype(BF16), wout_ref[aw + cw:, :], preferred_element_type=F32)

    mod = mod_ref[0, 0]
    x1 = _layer_norm(alpha * x_ref[0] + mod[2:3] * y, lnw_ref[...], lnb_ref[...])
    x1_ref[0] = x1
    h2 = x1 * (1.0 + mod[4:5]) + mod[3:4]
    h2_ref[0] = h2
    logits = jnp.dot(h2, wr_ref[...], precision=HIGHEST, preferred_element_type=F32)
    lane = lax.broadcasted_iota(jnp.int32, logits.shape, 1)
    logits = jnp.where(lane < n_experts, logits, -jnp.inf)
    ex = jnp.exp(logits - jnp.max(logits, axis=1, keepdims=True))
    aff = ex / jnp.sum(ex, axis=1, keepdims=True)
    aff_ref[0] = aff[:, :n_experts]


def _outproj(xa, att, cg, o_f, o_b, w_conv, hgrn_norm_w, w_out_bf, modtab, ln_w, ln_b, w_router_pad, *,
             m_ctx, alpha, n_experts, cw, col_gg):
    b, l, d = xa.shape
    tm = TILE
    aw = att.shape[2]
    hw = o_f.shape[2]
    r8 = tm // 8
    last8 = l // 8 - 1
    ctx_tiles = m_ctx // tm

    def rows(width, col):
        return pl.BlockSpec((1, tm, width), lambda bb, i: (bb, i, col))

    def halo_prev(col):
        return pl.BlockSpec((1, 8, cw), lambda bb, i: (bb, jnp.maximum(i * r8 - 1, 0), col))

    def halo_next(col):
        return pl.BlockSpec((1, 8, cw), lambda bb, i: (bb, jnp.minimum((i + 1) * r8, last8), col))

    def const(shape):
        return pl.BlockSpec(shape, lambda bb, i: (0,) * len(shape))

    kern = functools.partial(_outproj_kernel, ctx_tiles=ctx_tiles, alpha=alpha, n_experts=n_experts)
    return pl.pallas_call(
        kern,
        grid=(b, l // tm),
        in_specs=[rows(d, 0), rows(aw, 0),
                  rows(cw, 0), rows(cw, 1), rows(cw, 2),
                  halo_prev(0), halo_prev(2), halo_next(0), halo_next(2),
                  rows(hw, 0), rows(hw, 0), rows(hw, col_gg),
                  const((3, cw)), const((1, HEAD_LANES)), const((d, d)),
                  pl.BlockSpec((1, 1, 6, d), lambda bb, i: (bb, jnp.minimum(i // ctx_tiles, 1), 0, 0)),
                  const((1, d)), const((1, d)), const((d, HEAD_LANES))],
        out_specs=[rows(d, 0), rows(d, 0), rows(n_experts, 0)],
        out_shape=[jax.ShapeDtypeStruct((b, l, d), F32), jax.ShapeDtypeStruct((b, l, d), F32),
                   jax.ShapeDtypeStruct((b, l, n_experts), F32)],
        compiler_params=_cparams(("arbitrary", "arbitrary")),
        name="outproj_ln_router",
    )(xa, att, cg, cg, cg, cg, cg, cg, cg, o_f, o_b, cg, w_conv, hgrn_norm_w.reshape(1, HEAD_LANES), w_out_bf,
      modtab, ln_w.reshape(1, d), ln_b.reshape(1, d), w_router_pad)


def _topk_kernel(aff_ref, pos_ref, off_ref, idx_ref, cnt_scr, *, m_ctx, cap_ctx, cap_lat):
    aff = aff_ref[0]
    ne, l = aff.shape
    ch = TILE
    nch = l // ch
    ctx_ch = m_ctx // ch
    bits = pltpu.bitcast(aff, jnp.int32)
    lane = lax.broadcasted_iota(jnp.int32, (ne, l), 1)
    in_ctx = lane < m_ctx

    def kth_largest(seg, k):
        def body(_, lohi):
            lo, hi = lohi
            mid = lo + lax.shift_right_logical(hi - lo, 1)
            cnt = jnp.sum(jnp.where(jnp.logical_and(seg, bits >= mid), 1.0, 0.0), axis=1, keepdims=True)
            ge = cnt >= k
            return jnp.where(ge, mid, lo), jnp.where(ge, hi, mid)
        lo0 = jnp.zeros((ne, 1), jnp.int32)
        hi0 = jnp.full((ne, 1), 0x7F800000, jnp.int32)
        return lax.fori_loop(0, 31, body, (lo0, hi0))[0]

    thr = jnp.where(in_ctx, kth_largest(in_ctx, cap_ctx), kth_largest(jnp.logical_not(in_ctx), cap_lat))
    gt = bits > thr
    eq = bits == thr
    gtf = jnp.where(gt, 1.0, 0.0)
    n_gt_ctx = jnp.sum(jnp.where(in_ctx, gtf, 0.0), axis=1, keepdims=True)
    n_gt_lat = jnp.sum(jnp.where(in_ctx, 0.0, gtf), axis=1, keepdims=True)
    need = jnp.where(in_ctx, cap_ctx - n_gt_ctx, cap_lat - n_gt_lat)

    tr = lax.broadcasted_iota(jnp.int32, (ch, ch), 0)
    tc = lax.broadcasted_iota(jnp.int32, (ch, ch), 1)
    tri = jnp.where(tr <= tc, 1.0, 0.0).astype(BF16)

    def seg_prefix(flag):
        excl, bases = [], []
        base = jnp.zeros((ne, 1), F32)
        for c in range(nch):
            if c == ctx_ch:
                base = jnp.zeros((ne, 1), F32)
            fc = flag[:, c * ch:(c + 1) * ch]
            incl = jnp.dot(fc.astype(BF16), tri, preferred_element_type=F32)
            excl.append(base + incl - fc)
            bases.append(base)
            base = base + incl[:, ch - 1:ch]
        bases.append(base)
        return jnp.concatenate(excl, axis=1), bases

    eq_excl, _ = seg_prefix(jnp.where(eq, 1.0, 0.0))
    sel = jnp.logical_or(gt, jnp.logical_and(eq, eq_excl < need))
    self_ = jnp.where(sel, 1.0, 0.0)
    sel_excl, bases = seg_prefix(self_)
    seg_off = jnp.where(in_ctx, 0.0, float(cap_ctx))
    pos_ref[0] = jnp.where(sel, sel_excl + seg_off, -1.0).astype(jnp.int32)

    olane = lax.broadcasted_iota(jnp.int32, (ne, HEAD_LANES), 1)
    off = jnp.zeros((ne, HEAD_LANES), F32)
    for c in range(nch + 1):
        off = jnp.where(olane == c, bases[c] + (0.0 if c < ctx_ch else float(cap_ctx)), off)
    off_ref[0] = off.astype(jnp.int32)

    cnt = sel_excl + self_ + seg_off
    for e in range(ne):
        cnt_scr[e] = cnt[e:e + 1]
    n_slots = idx_ref.shape[1]
    sb = n_slots // IDX_SLOT_BLOCKS
    ilane = lax.broadcasted_iota(jnp.int32, (sb, HEAD_LANES), 1)
    for blk in range(IDX_SLOT_BLOCKS):
        slot = (lax.broadcasted_iota(jnp.int32, (sb, HEAD_LANES), 0) + blk * sb).astype(F32)

        def per_expert(e, out):
            def per_chunk(c, acc):
                cnt_row = cnt_scr[e, :, pl.ds(pl.multiple_of(c * HEAD_LANES, HEAD_LANES), HEAD_LANES)]
                return acc + jnp.where(cnt_row <= slot, 1.0, 0.0)
            acc = lax.fori_loop(0, l // HEAD_LANES, per_chunk, jnp.zeros((sb, HEAD_LANES), F32))
            return jnp.where(ilane == e, jnp.sum(acc, axis=1, keepdims=True), out)
        out = lax.fori_loop(0, ne, per_expert, jnp.zeros((sb, HEAD_LANES), F32))
        idx_ref[0, blk * sb:(blk + 1) * sb, :] = out.astype(jnp.int32)


def _topk(aff_t, *, m_ctx, cap_ctx, cap_lat):
    b, ne, l = aff_t.shape
    n_slots = cap_ctx + cap_lat
    kern = functools.partial(_topk_kernel, m_ctx=m_ctx, cap_ctx=cap_ctx, cap_lat=cap_lat)
    return pl.pallas_call(
        kern,
        grid=(b,),
        in_specs=[pl.BlockSpec((1, ne, l), lambda bb: (bb, 0, 0))],
        out_specs=[pl.BlockSpec((1, ne, l), lambda bb: (bb, 0, 0)),
                   pl.BlockSpec((1, ne, HEAD_LANES), lambda bb: (bb, 0, 0)),
                   pl.BlockSpec((1, n_slots, HEAD_LANES), lambda bb: (bb, 0, 0))],
        out_shape=[jax.ShapeDtypeStruct((b, ne, l), jnp.int32),
                   jax.ShapeDtypeStruct((b, ne, HEAD_LANES), jnp.int32),
                   jax.ShapeDtypeStruct((b, n_slots, HEAD_LANES), jnp.int32)],
        scratch_shapes=[pltpu.VMEM((ne, 1, l), F32)],
        compiler_params=_cparams(("arbitrary",)),
        name="expert_choice_topk",
    )(aff_t)


def _ffn_kernel(idx_ref, h_hbm, wg_ref, wu_ref, wd_ref, y_ref, xg_scr, xb_scr, acc_scr, sem, *, n_experts):
    e, b, f = pl.program_id(0), pl.program_id(1), pl.program_id(2)
    n_slots = xg_scr.shape[0]

    @pl.when(f == 0)
    def _():
        base = (b * n_experts + e) * n_slots

        def issue(r, carry):
            pltpu.make_async_copy(h_hbm.at[b, pl.ds(idx_ref[base + r], 1)], xg_scr.at[pl.ds(r, 1)], sem.at[0]).start()
            return carry
        lax.fori_loop(0, n_slots, issue, 0)

        def drain(r, carry):
            pltpu.make_async_copy(h_hbm.at[b, pl.ds(0, 1)], xg_scr.at[pl.ds(r, 1)], sem.at[0]).wait()
            return carry
        lax.fori_loop(0, n_slots, drain, 0)
        xb_scr[...] = xg_scr[...].astype(BF16)
        acc_scr[...] = jnp.zeros(acc_scr.shape, F32)

    xb = xb_scr[...]
    a = jnp.dot(xb, wg_ref[0].astype(BF16), preferred_element_type=F32)
    u = jnp.dot(xb, wu_ref[0].astype(BF16), preferred_element_type=F32)
    hid = (_silu(a) * u).astype(BF16)
    acc_scr[...] += jnp.dot(hid, wd_ref[0].astype(BF16), preferred_element_type=F32)

    @pl.when(f == pl.num_programs(2) - 1)
    def _():
        y_ref[0, 0] = acc_scr[...].astype(y_ref.dtype)


def _expert_ffn(idx_flat, h2, w_gate, w_up, w_down, *, n_slots):
    b, l, d = h2.shape
    ne, _, ff = w_gate.shape
    tf = 256
    grid_spec = pltpu.PrefetchScalarGridSpec(
        num_scalar_prefetch=1,
        grid=(ne, b, ff // tf),
        in_specs=[pl.BlockSpec(memory_space=pl.ANY),
                  pl.BlockSpec((1, d, tf), lambda e, bb, f, idx: (e, 0, f)),
                  pl.BlockSpec((1, d, tf), lambda e, bb, f, idx: (e, 0, f)),
                  pl.BlockSpec((1, tf, d), lambda e, bb, f, idx: (e, f, 0))],
        out_specs=pl.BlockSpec((1, 1, n_slots, d), lambda e, bb, f, idx: (bb, e, 0, 0)),
        scratch_shapes=[pltpu.VMEM((n_slots, d), F32), pltpu.VMEM((n_slots, d), BF16),
                        pltpu.VMEM((n_slots, d), F32), pltpu.SemaphoreType.DMA((1,))])
    return pl.pallas_call(
        functools.partial(_ffn_kernel, n_experts=ne),
        grid_spec=grid_spec,
        out_shape=jax.ShapeDtypeStruct((b, ne, n_slots, d), BF16),
        compiler_params=_cparams(("arbitrary", "arbitrary", "arbitrary")),
        name="expert_ffn",
    )(idx_flat, h2, w_gate, w_up, w_down)


def _combine_kernel(off_ref, x1_ref, aff_ref, pos_ref, mod_ref, lnw_ref, lnb_ref, y_hbm, o_ref, win_scr, sem, *,
                    alpha, n_experts):
    b, i = pl.program_id(0), pl.program_id(1)
    tm = x1_ref.shape[1]
    win = win_scr.shape[1]
    n_slots = y_hbm.shape[2]
    copies = []
    for e in range(n_experts):
        p0 = off_ref[(b * n_experts + e) * HEAD_LANES + i]
        start = jnp.minimum((p0 // WIN_ALIGN) * WIN_ALIGN, n_slots - win)
        start = pl.multiple_of(start, WIN_ALIGN)
        cp = pltpu.make_async_copy(y_hbm.at[b, e, pl.ds(start, win)], win_scr.at[e], sem.at[e])
        cp.start()
        copies.append((start, cp))

    pos = pos_ref[0]
    aff = aff_ref[0]
    scol = lax.broadcasted_iota(jnp.int32, (tm, win), 1)
    acc = jnp.zeros((tm, x1_ref.shape[2]), F32)
    for e in range(n_experts):
        start, cp = copies[e]
        cp.wait()
        onehot = jnp.where((pos[:, e:e + 1] - start) == scol, 1.0, 0.0).astype(BF16)
        acc = acc + aff[:, e:e + 1] * jnp.dot(onehot, win_scr[e], preferred_element_type=F32)

    mod = mod_ref[0, 0]
    o_ref[0] = _layer_norm(alpha * x1_ref[0] + mod[5:6] * acc, lnw_ref[...], lnb_ref[...])


def _combine(off_flat, x1, aff, pos_tok, modtab, ln_w, ln_b, y, *, m_ctx, alpha):
    b, l, d = x1.shape
    ne = aff.shape[2]
    tm = TILE
    win = tm + WIN_ALIGN
    ctx_tiles = m_ctx // tm

    def rows(width):
        return pl.BlockSpec((1, tm, width), lambda bb, i, off: (bb, i, 0))

    grid_spec = pltpu.PrefetchScalarGridSpec(
        num_scalar_prefetch=1,
        grid=(b, l // tm),
        in_specs=[rows(d), rows(ne), rows(ne),
                  pl.BlockSpec((1, 1, 6, d), lambda bb, i, off: (bb, jnp.minimum(i // ctx_tiles, 1), 0, 0)),
                  pl.BlockSpec((1, d), lambda bb, i, off: (0, 0)),
                  pl.BlockSpec((1, d), lambda bb, i, off: (0, 0)),
                  pl.BlockSpec(memory_space=pl.ANY)],
        out_specs=rows(d),
        scratch_shapes=[pltpu.VMEM((ne, win, d), BF16), pltpu.SemaphoreType.DMA((ne,))])
    return pl.pallas_call(
        functools.partial(_combine_kernel, alpha=alpha, n_experts=ne),
        grid_spec=grid_spec,
        out_shape=jax.ShapeDtypeStruct((b, l, d), F32),
        compiler_params=_cparams(("arbitrary", "arbitrary")),
        name="moe_combine_ln",
    )(off_flat, x1, aff, pos_tok, modtab, ln_w.reshape(1, d), ln_b.reshape(1, d), y)


def kernel(x, c, ctx, c_ctx, w_mod, b_mod, w_in, w_conv, lambda_qk, subln_w, hgrn_lb_logits, hgrn_norm_w, w_out,
           ln_w, ln_b, w_router, w_gate, w_up, w_down):
    bsz, n_lat, d = x.shape
    m_ctx = ctx.shape[1]
    depth = w_mod.shape[0]
    ne = w_router.shape[2]
    aw, cw, hw = d // 2, d // 4, d // 4
    assert m_ctx % TILE == 0 and n_lat % 512 == 0 and n_lat % GRID_W == 0 and bsz + 1 <= 8
    assert (m_ctx + n_lat) % (8 * 16) == 0 and aw == 1024 and ne <= HEAD_LANES
    alpha = (2.0 * depth) ** 0.25
    cap_ctx = EC_CAPACITY_FACTOR * m_ctx // ne
    cap_lat = EC_CAPACITY_FACTOR * n_lat // ne
    n_slots = cap_ctx + cap_lat
    assert n_slots % WIN_ALIGN == 0 and n_slots >= TILE + WIN_ALIGN

    cs = jnp.zeros((8, d), F32).at[:bsz].set(c).at[bsz].set(c_ctx)
    mod = _modulation(cs, w_mod, b_mod)
    mod_lat = mod[:, :bsz].reshape(depth, bsz, 1, 6, d)
    mod_ctx = jnp.broadcast_to(mod[:, bsz].reshape(depth, 1, 1, 6, d), (depth, bsz, 1, 6, d))
    modtab = jnp.concatenate([mod_ctx, mod_lat], axis=2)

    tables = _rope_tables(m_ctx, n_lat)
    xa = jnp.concatenate([ctx, x], axis=1)
    w_router_pad = jnp.zeros((depth, d, HEAD_LANES), F32).at[:, :, :ne].set(w_router)
    hg_heads = hw // HEAD_LANES
    cb0 = 3 * cw // HEAD_LANES

    for l in range(depth):
        lambda_init = 0.8 - 0.6 * math.exp(-0.3 * l)
        w_in_bf = w_in[l].astype(BF16)
        qkv = _inproj(xa, modtab[l], w_in_bf, tables, col0=0, ncols=3 * aw, out_dtype=BF16, rope_tiles=2,
                      m_ctx=m_ctx)
        cg = _inproj(xa, modtab[l], w_in_bf, tables, col0=3 * aw, ncols=3 * cw + 5 * hw, out_dtype=F32,
                     rope_tiles=0, m_ctx=m_ctx)
        att = _attention(qkv, lambda_qk[l], subln_w[l], m_ctx=m_ctx, lambda_init=lambda_init)
        o_f, o_b = _hgrn(cg, hgrn_lb_logits, layer=l, hg_heads=hg_heads, col_q=cb0, col_i=cb0 + hg_heads,
                         col_ff=cb0 + 3 * hg_heads, col_fb=cb0 + 4 * hg_heads)
        x1, h2, aff = _outproj(xa, att, cg, o_f, o_b, w_conv[l], hgrn_norm_w[l], w_out[l].astype(BF16), modtab[l],
                               ln_w[l, 0], ln_b[l, 0], w_router_pad[l], m_ctx=m_ctx, alpha=alpha, n_experts=ne,
                               cw=cw, col_gg=(3 * cw + 2 * hw) // hw)
        pos, off, idx = _topk(jnp.swapaxes(aff, 1, 2), m_ctx=m_ctx, cap_ctx=cap_ctx, cap_lat=cap_lat)
        idx_flat = jnp.swapaxes(idx[:, :, :ne], 1, 2).reshape(-1)
        y = _expert_ffn(idx_flat, h2, w_gate[l], w_up[l], w_down[l], n_slots=n_slots)
        xa = _combine(off.reshape(-1), x1, aff, jnp.swapaxes(pos, 1, 2), modtab[l], ln_w[l, 1], ln_b[l, 1], y,
                      m_ctx=m_ctx, alpha=alpha)
    return xa[:, m_ctx:]
```

```python
import functools
import math

import jax
import jax.numpy as jnp
from jax import lax
from jax.experimental import pallas as pl
from jax.experimental.pallas import tpu as pltpu

F32 = jnp.float32
BF16 = jnp.bfloat16
HIGHEST = lax.Precision.HIGHEST

GRID_W = 64
ROPE_THETA = 10000.0
ATT_HEAD_DIM = 64
HEAD_LANES = 128
HGRN_CHUNK = 64
EC_CAPACITY_FACTOR = 2
EPS = 1e-6
TILE = 256
IDX_SLOT_BLOCKS = 4
WIN_ALIGN = 16
VMEM_LIMIT = 56 * 1024 * 1024


def _cparams(sem):
    return pltpu.CompilerParams(dimension_semantics=sem, vmem_limit_bytes=VMEM_LIMIT)


def _silu(x):
    return x * jax.nn.sigmoid(x)


def _mod_kernel(cs_ref, w_ref, b_ref, o_ref):
    a = _silu(cs_ref[...])
    o_ref[0] = jnp.dot(a, w_ref[0], precision=HIGHEST, preferred_element_type=F32) + b_ref[0]


def _modulation(cs, w_mod, b_mod):
    depth, d, n6 = w_mod.shape
    tn = 1024
    return pl.pallas_call(
        _mod_kernel,
        grid=(depth, n6 // tn),
        in_specs=[pl.BlockSpec((8, d), lambda l, j: (0, 0)),
                  pl.BlockSpec((1, d, tn), lambda l, j: (l, 0, j)),
                  pl.BlockSpec((1, 1, tn), lambda l, j: (l, 0, j))],
        out_specs=pl.BlockSpec((1, 8, tn), lambda l, j: (l, 0, j)),
        out_shape=jax.ShapeDtypeStruct((depth, 8, n6), F32),
        compiler_params=_cparams(("arbitrary", "arbitrary")),
        name="modulation",
    )(cs, w_mod, b_mod.reshape(depth, 1, n6))


def _inproj_kernel(x_ref, mod_ref, w_ref, rc_ref, ra_ref, rb_ref, o_ref, h_scr, *, m_ctx, rope_tiles):
    i = pl.program_id(1)
    j = pl.program_id(2)
    tm = x_ref.shape[1]

    @pl.when(j == 0)
    def _():
        row = i * tm + lax.broadcasted_iota(jnp.int32, (tm, 1), 0)
        is_ctx = row < m_ctx
        mod = mod_ref[0]
        sh = jnp.where(is_ctx, mod[0, 0:1], mod[1, 0:1])
        sc = jnp.where(is_ctx, mod[0, 1:2], mod[1, 1:2])
        h_scr[...] = (x_ref[0] * (1.0 + sc) + sh).astype(BF16)

    acc = jnp.dot(h_scr[...], w_ref[...], preferred_element_type=F32)

    if rope_tiles == 0:
        o_ref[0] = acc.astype(o_ref.dtype)
        return

    @pl.when(j < rope_tiles)
    def _():
        rc, ra, rb = rc_ref[...], ra_ref[...], rb_ref[...]
        qscale = jnp.where(j == 0, ATT_HEAD_DIM ** -0.5 * math.log2(math.e), 1.0).astype(F32)
        for c in range(acc.shape[1] // HEAD_LANES):
            blk = acc[:, c * HEAD_LANES:(c + 1) * HEAD_LANES]
            rot = (blk * rc + pltpu.roll(blk, 16, 1) * ra + pltpu.roll(blk, HEAD_LANES - 16, 1) * rb)
            o_ref[0, :, c * HEAD_LANES:(c + 1) * HEAD_LANES] = (rot * qscale).astype(o_ref.dtype)

    @pl.when(j >= rope_tiles)
    def _():
        o_ref[0] = acc.astype(o_ref.dtype)


def _inproj(xa, modtab, w_bf, tables, *, col0, ncols, out_dtype, rope_tiles, m_ctx):
    b, l, d = xa.shape
    tn = 1024
    tm = l // 8
    joff = col0 // tn
    kern = functools.partial(_inproj_kernel, m_ctx=m_ctx, rope_tiles=rope_tiles)
    tab_spec = pl.BlockSpec((tm, HEAD_LANES), lambda bb, i, j: (i, 0))
    return pl.pallas_call(
        kern,
        grid=(b, l // tm, ncols // tn),
        in_specs=[pl.BlockSpec((1, tm, d), lambda bb, i, j: (bb, i, 0)),
                  pl.BlockSpec((1, 2, 6, d), lambda bb, i, j: (bb, 0, 0, 0)),
                  pl.BlockSpec((d, tn), lambda bb, i, j: (0, j + joff)),
                  tab_spec, tab_spec, tab_spec],
        out_specs=pl.BlockSpec((1, tm, tn), lambda bb, i, j: (bb, i, j)),
        out_shape=jax.ShapeDtypeStruct((b, l, ncols), out_dtype),
        scratch_shapes=[pltpu.VMEM((tm, d), BF16)],
        compiler_params=_cparams(("arbitrary", "arbitrary", "arbitrary")),
        name="inproj",
    )(xa, modtab, w_bf, *tables)


def _rope_tables(m_ctx, n_lat):
    nf = ATT_HEAD_DIM // 4
    inv = ROPE_THETA ** (-jnp.arange(nf, dtype=F32) / nf)
    rows = n_lat // GRID_W
    row = jnp.repeat(jnp.arange(rows, dtype=F32), GRID_W)
    col = jnp.tile(jnp.arange(GRID_W, dtype=F32), rows)
    ar, ac = row[:, None] * inv, col[:, None] * inv
    cr, sr, cc, sc = jnp.cos(ar), jnp.sin(ar), jnp.cos(ac), jnp.sin(ac)
    z = jnp.zeros_like(sr)
    c64 = jnp.concatenate([cr, cr, cc, cc], axis=1)
    a64 = jnp.concatenate([z, sr, z, sc], axis=1)
    b64 = jnp.concatenate([-sr, z, -sc, z], axis=1)
    reps = HEAD_LANES // ATT_HEAD_DIM

    def full(t64, fill):
        lat = jnp.tile(t64, (1, reps))
        return jnp.concatenate([jnp.full((m_ctx, HEAD_LANES), fill, F32), lat], axis=0)

    return full(c64, 1.0), full(a64, 0.0), full(b64, 0.0)


def _attn_kernel(lq_ref, sw_ref, q_ref, k_ref, v_ref, o_ref, vt_scr, m0_scr, l0_scr, acc0_scr, m1_scr, l1_scr,
                 acc1_scr, sa0_scr, sa1_scr, sb0_scr, sb1_scr, *, m_ctx, tk, lambda_init):
    i = pl.program_id(2)
    tq = q_ref.shape[1]
    n_lat = k_ref.shape[1] - m_ctx

    @pl.when(i == 0)
    def _():
        vt_scr[...] = v_ref[0].T

    q = q_ref[0]
    lane = lax.broadcasted_iota(jnp.int32, q.shape, 1)
    zero = jnp.zeros_like(q)
    qs = (jnp.where(lane < ATT_HEAD_DIM, q, zero), jnp.where(lane >= ATT_HEAD_DIM, q, zero))
    stats = ((m0_scr, l0_scr, acc0_scr), (m1_scr, l1_scr, acc1_scr))
    for m_scr, l_scr, acc_scr in stats:
        m_scr[...] = jnp.full(m_scr.shape, -jnp.inf, F32)
        l_scr[...] = jnp.zeros(l_scr.shape, F32)
        acc_scr[...] = jnp.zeros(acc_scr.shape, F32)

    def scores(start, size, st_refs):
        kc = k_ref[0, pl.ds(start, size), :]
        cmax = []
        for s in range(2):
            st = lax.dot_general(kc, qs[s], (((1,), (1,)), ((), ())), preferred_element_type=F32)
            st_refs[s][0:size, :] = st
            cmax.append(jnp.max(st, axis=0, keepdims=True))
        return tuple(cmax)

    def accumulate(start, size, st_refs, cmax):
        vt = vt_scr[:, pl.ds(start, size)]
        for s, (m_scr, l_scr, acc_scr) in enumerate(stats):
            m_old = m_scr[...]
            m_new = jnp.maximum(m_old, cmax[s])
            alpha = jnp.exp2(m_old - m_new)
            p = jnp.exp2(st_refs[s][0:size, :] - m_new)
            l_scr[...] = alpha * l_scr[...] + jnp.sum(p, axis=0, keepdims=True)
            acc_scr[...] = alpha * acc_scr[...] + jnp.dot(vt, p.astype(BF16), preferred_element_type=F32)
            m_scr[...] = m_new

    st_a, st_b = (sa0_scr, sa1_scr), (sb0_scr, sb1_scr)
    accumulate(0, m_ctx, st_a, scores(0, m_ctx, st_a))

    @pl.when(i * tq >= m_ctx)
    def _():
        def at(c):
            return pl.multiple_of(m_ctx + c * tk, math.gcd(m_ctx, tk))
        nck = n_lat // tk

        def body(c2, cm_a):
            c = 2 * c2
            cm_b = scores(at(c + 1), tk, st_b)
            accumulate(at(c), tk, st_a, cm_a)
            cm_a = scores(at(c + 2), tk, st_a)
            accumulate(at(c + 1), tk, st_b, cm_b)
            return cm_a
        cm_a = lax.fori_loop(0, nck // 2 - 1, body, scores(at(0), tk, st_a))
        cm_b = scores(at(nck - 1), tk, st_b)
        accumulate(at(nck - 2), tk, st_a, cm_a)
        accumulate(at(nck - 1), tk, st_b, cm_b)

    lq = lq_ref[...]
    lam = (jnp.exp(jnp.sum(lq[0:1] * lq[1:2], axis=1, keepdims=True))
           - jnp.exp(jnp.sum(lq[2:3] * lq[3:4], axis=1, keepdims=True)) + lambda_init)
    ot = acc0_scr[...] / l0_scr[...] - lam * (acc1_scr[...] / l1_scr[...])
    ms = jnp.mean(ot * ot, axis=0, keepdims=True)
    o = (ot * lax.rsqrt(ms + EPS)).T * sw_ref[...] * (1.0 - lambda_init)
    o_ref[0] = o.astype(o_ref.dtype)


def _attention(qkv, lambda_qk, subln_w, *, m_ctx, lambda_init):
    b, l, w3 = qkv.shape
    heads = w3 // (3 * HEAD_LANES)
    tq = TILE
    tk = 512
    kern = functools.partial(_attn_kernel, m_ctx=m_ctx, tk=tk, lambda_init=lambda_init)
    return pl.pallas_call(
        kern,
        grid=(b, heads, l // tq),
        in_specs=[pl.BlockSpec((4, ATT_HEAD_DIM), lambda bb, h, i: (0, 0)),
                  pl.BlockSpec((1, HEAD_LANES), lambda bb, h, i: (0, 0)),
                  pl.BlockSpec((1, tq, HEAD_LANES), lambda bb, h, i: (bb, i, h)),
                  pl.BlockSpec((1, l, HEAD_LANES), lambda bb, h, i: (bb, 0, heads + h)),
                  pl.BlockSpec((1, l, HEAD_LANES), lambda bb, h, i: (bb, 0, 2 * heads + h))],
        out_specs=pl.BlockSpec((1, tq, HEAD_LANES), lambda bb, h, i: (bb, i, h)),
        out_shape=jax.ShapeDtypeStruct((b, l, heads * HEAD_LANES), BF16),
        scratch_shapes=[pltpu.VMEM((HEAD_LANES, l), BF16)]
        + [pltpu.VMEM((1, tq), F32), pltpu.VMEM((1, tq), F32), pltpu.VMEM((HEAD_LANES, tq), F32)] * 2
        + [pltpu.VMEM((max(tk, m_ctx), tq), F32)] * 4,
        compiler_params=_cparams(("arbitrary", "arbitrary", "arbitrary")),
        name="diff_attention",
    )(lambda_qk, subln_w.reshape(1, HEAD_LANES), qkv, qkv, qkv)


def _log_sigmoid(z):
    return jnp.minimum(z, 0.0) - jnp.log1p(jnp.exp(-jnp.abs(z)))


def _forget_gate(z, lb):
    ls = _log_sigmoid(z)
    key = jax.nn.sigmoid(-z)
    if lb is None:
        return ls, key
    a = jnp.log(lb)
    b = jnp.log1p(-lb) + ls
    logf = jnp.maximum(a, b) + jnp.log1p(jnp.exp(-jnp.abs(a - b)))
    return logf, (1.0 - lb) * key


def _hgrn_direction(q_ref, v_ref, z_ref, o_ref, s_scr, lb, reverse):
    t = q_ref.shape[1]
    ch = HGRN_CHUNK
    q, v = q_ref[0], v_ref[0]
    logf, key = _forget_gate(z_ref[0], lb)
    r = lax.broadcasted_iota(jnp.int32, (t, t), 0)
    c = lax.broadcasted_iota(jnp.int32, (t, t), 1)
    tri = ((r // ch) == (c // ch)) & ((c >= r) if reverse else (c <= r))
    a = jnp.dot(tri.astype(F32), logf, precision=HIGHEST, preferred_element_type=F32)
    ri = lax.broadcasted_iota(jnp.int32, (ch, ch), 0)
    ci = lax.broadcasted_iota(jnp.int32, (ch, ch), 1)
    mask = (ci >= ri) if reverse else (ci <= ri)
    st = s_scr[...]
    nch = t // ch
    for ck in (range(nch - 1, -1, -1) if reverse else range(nch)):
        sl = slice(ck * ch, (ck + 1) * ch)
        ac, qc, kc, vc = a[sl], q[sl], key[sl], v[sl]
        a_end = ac[0:1] if reverse else ac[ch - 1:ch]
        a_mid = ac[ch // 2 - 1:ch // 2]
        qe = (qc * jnp.exp(ac - a_mid)).astype(BF16)
        ke = (kc * jnp.exp(a_mid - ac)).astype(BF16)
        sc = lax.dot_general(qe, ke, (((1,), (1,)), ((), ())), preferred_element_type=F32)
        sc = jnp.where(mask, sc, 0.0).astype(BF16)
        vb = vc.astype(BF16)
        intra = jnp.dot(sc, vb, preferred_element_type=F32)
        inter = lax.dot_general((qc * jnp.exp(ac)).astype(BF16), st.astype(BF16),
                                (((1,), (1,)), ((), ())), preferred_element_type=F32)
        o_ref[0, sl, :] = inter + intra
        kd = (kc * jnp.exp(a_end - ac)).astype(BF16)
        st = jnp.exp(a_end) * st + lax.dot_general(vb, kd, (((0,), (0,)), ((), ())), preferred_element_type=F32)
    s_scr[...] = st


def _hgrn_kernel(lbl_ref, qf_ref, vf_ref, zf_ref, qb_ref, vb_ref, zb_ref, of_ref, ob_ref, sf_scr, sb_scr, *, layer):
    @pl.when(pl.program_id(2) == 0)
    def _():
        sf_scr[...] = jnp.zeros(sf_scr.shape, F32)
        sb_scr[...] = jnp.zeros(sb_scr.shape, F32)

    if layer == 0:
        lbf = lbb = None
    else:
        lg = lbl_ref[...]
        ex = jnp.exp(lg - jnp.max(lg, axis=1, keepdims=True))
        sm = ex / jnp.sum(ex, axis=1, keepdims=True)
        lb = sm[:, 1]
        for k in range(2, layer + 1):
            lb = lb + sm[:, k]
        lbf, lbb = lb[0:1], lb[1:2]
    _hgrn_direction(qf_ref, vf_ref, zf_ref, of_ref, sf_scr, lbf, False)
    _hgrn_direction(qb_ref, vb_ref, zb_ref, ob_ref, sb_scr, lbb, True)


def _hgrn(cg, lb_logits, *, layer, hg_heads, col_q, col_i, col_ff, col_fb):
    b, l, _ = cg.shape
    t = TILE
    nb = l // t
    depth = lb_logits.shape[1]

    def fwd(col):
        return pl.BlockSpec((1, t, HEAD_LANES), lambda bb, h, i: (bb, i, col + h))

    def bwd(col):
        return pl.BlockSpec((1, t, HEAD_LANES), lambda bb, h, i: (bb, jnp.where(i == 0, 0, nb - i), col + h))

    out_f = pl.BlockSpec((1, t, HEAD_LANES), lambda bb, h, i: (bb, i, h))
    out_b = pl.BlockSpec((1, t, HEAD_LANES), lambda bb, h, i: (bb, jnp.where(i == 0, 0, nb - i), h))
    shp = jax.ShapeDtypeStruct((b, l, hg_heads * HEAD_LANES), F32)
    return pl.pallas_call(
        functools.partial(_hgrn_kernel, layer=layer),
        grid=(b, hg_heads, nb),
        in_specs=[pl.BlockSpec((2, depth, HEAD_LANES), lambda bb, h, i: (0, 0, h)),
                  fwd(col_q), fwd(col_i), fwd(col_ff), bwd(col_q), bwd(col_i), bwd(col_fb)],
        out_specs=[out_f, out_b],
        out_shape=[shp, shp],
        scratch_shapes=[pltpu.VMEM((HEAD_LANES, HEAD_LANES), F32), pltpu.VMEM((HEAD_LANES, HEAD_LANES), F32)],
        compiler_params=_cparams(("arbitrary", "arbitrary", "arbitrary")),
        name="hgrn2",
    )(lb_logits, cg, cg, cg, cg, cg, cg)


def _layer_norm(r, w, b):
    mu = jnp.mean(r, axis=1, keepdims=True)
    var = jnp.mean(jnp.square(r - mu), axis=1, keepdims=True)
    return (r - mu) * lax.rsqrt(var + EPS) * w + b


def _outproj_kernel(x_ref, att_ref, cx_ref, cb_ref, cc_ref, cxp_ref, ccp_ref, cxn_ref, ccn_ref, of_ref, ob_ref,
                    gg_ref, wconv_ref, hnw_ref, wout_ref, mod_ref, lnw_ref, lnb_ref, wr_ref,
                    x1_ref, h2_ref, aff_ref, *, ctx_tiles, alpha, n_experts):
    i = pl.program_id(1)
    nt = pl.num_programs(1)
    tm = x_ref.shape[1]
    aw = att_ref.shape[2]
    cw = cx_ref.shape[2]

    u = cc_ref[0] * cx_ref[0]
    prev_ok = jnp.logical_and(i != 0, i != ctx_tiles)
    next_ok = jnp.logical_and(i != ctx_tiles - 1, i != nt - 1)
    u_before = jnp.where(prev_ok, (ccp_ref[0] * cxp_ref[0])[7:8], 0.0)
    u_after = jnp.where(next_ok, (ccn_ref[0] * cxn_ref[0])[0:1], 0.0)
    row = lax.broadcasted_iota(jnp.int32, (tm, 1), 0)
    u_prev = jnp.where(row == 0, u_before, pltpu.roll(u, 1, 0))
    u_next = jnp.where(row == tm - 1, u_after, pltpu.roll(u, tm - 1, 0))
    wc = wconv_ref[...]
    conv = cb_ref[0] * (u_prev * wc[0:1] + u * wc[1:2] + u_next * wc[2:3])

    o = of_ref[0] + ob_ref[0]
    gg = gg_ref[0]
    recs = []
    for h in range(o.shape[1] // HEAD_LANES):
        oh = o[:, h * HEAD_LANES:(h + 1) * HEAD_LANES]
        ms = jnp.mean(oh * oh, axis=1, keepdims=True)
        recs.append(oh * lax.rsqrt(ms + EPS) * hnw_ref[...] * _silu(gg[:, h * HEAD_LANES:(h + 1) * HEAD_LANES]))
    rec = jnp.concatenate(recs, axis=1)

    y = jnp.dot(att_ref[0], wout_ref[0:aw, :], preferred_element_type=F32)
    y = y + jnp.dot(conv.astype(BF16), wout_ref[aw:aw + cw, :], preferred_element_type=F32)
    y = y + jnp.dot(rec.astype(BF16), wout_ref[aw + cw:, :], preferred_element_type=F32)

    mod = mod_ref[0, 0]
    x1 = _layer_norm(alpha * x_ref[0] + mod[2:3] * y, lnw_ref[...], lnb_ref[...])
    x1_ref[0] = x1
    h2 = x1 * (1.0 + mod[4:5]) + mod[3:4]
    h2_ref[0] = h2
    logits = jnp.dot(h2, wr_ref[...], precision=HIGHEST, preferred_element_type=F32)
    lane = lax.broadcasted_iota(jnp.int32, logits.shape, 1)
    logits = jnp.where(lane < n_experts, logits, -jnp.inf)
    ex = jnp.exp(logits - jnp.max(logits, axis=1, keepdims=True))
    aff = ex / jnp.sum(ex, axis=1, keepdims=True)
    aff_ref[0] = aff[:, :n_experts]


def _outproj(xa, att, cg, o_f, o_b, w_conv, hgrn_norm_w, w_out_bf, modtab, ln_w, ln_b, w_router_pad, *,
             m_ctx, alpha, n_experts, cw, col_gg):
    b, l, d = xa.shape
    tm = TILE
    aw = att.shape[2]
    hw = o_f.shape[2]
    r8 = tm // 8
    last8 = l // 8 - 1
    ctx_tiles = m_ctx // tm

    def rows(width, col):
        return pl.BlockSpec((1, tm, width), lambda bb, i: (bb, i, col))

    def halo_prev(col):
        return pl.BlockSpec((1, 8, cw), lambda bb, i: (bb, jnp.maximum(i * r8 - 1, 0), col))

    def halo_next(col):
        return pl.BlockSpec((1, 8, cw), lambda bb, i: (bb, jnp.minimum((i + 1) * r8, last8), col))

    def const(shape):
        return pl.BlockSpec(shape, lambda bb, i: (0,) * len(shape))

    kern = functools.partial(_outproj_kernel, ctx_tiles=ctx_tiles, alpha=alpha, n_experts=n_experts)
    return pl.pallas_call(
        kern,
        grid=(b, l // tm),
        in_specs=[rows(d, 0), rows(aw, 0),
                  rows(cw, 0), rows(cw, 1), rows(cw, 2),
                  halo_prev(0), halo_prev(2), halo_next(0), halo_next(2),
                  rows(hw, 0), rows(hw, 0), rows(hw, col_gg),
                  const((3, cw)), const((1, HEAD_LANES)), const((d, d)),
                  pl.BlockSpec((1, 1, 6, d), lambda bb, i: (bb, jnp.minimum(i // ctx_tiles, 1), 0, 0)),
                  const((1, d)), const((1, d)), const((d, HEAD_LANES))],
        out_specs=[rows(d, 0), rows(d, 0), rows(n_experts, 0)],
        out_shape=[jax.ShapeDtypeStruct((b, l, d), F32), jax.ShapeDtypeStruct((b, l, d), F32),
                   jax.ShapeDtypeStruct((b, l, n_experts), F32)],
        compiler_params=_cparams(("arbitrary", "arbitrary")),
        name="outproj_ln_router",
    )(xa, att, cg, cg, cg, cg, cg, cg, cg, o_f, o_b, cg, w_conv, hgrn_norm_w.reshape(1, HEAD_LANES), w_out_bf,
      modtab, ln_w.reshape(1, d), ln_b.reshape(1, d), w_router_pad)


def _topk_kernel(aff_ref, pos_ref, off_ref, idx_ref, cnt_scr, *, m_ctx, cap_ctx, cap_lat):
    aff = aff_ref[0]
    ne, l = aff.shape
    ch = TILE
    nch = l // ch
    ctx_ch = m_ctx // ch
    bits = pltpu.bitcast(aff, jnp.int32)
    lane = lax.broadcasted_iota(jnp.int32, (ne, l), 1)
    in_ctx = lane < m_ctx

    def kth_largest(seg, k):
        def body(_, lohi):
            lo, hi = lohi
            mid = lo + lax.shift_right_logical(hi - lo, 1)
            cnt = jnp.sum(jnp.where(jnp.logical_and(seg, bits >= mid), 1.0, 0.0), axis=1, keepdims=True)
            ge = cnt >= k
            return jnp.where(ge, mid, lo), jnp.where(ge, hi, mid)
        lo0 = jnp.zeros((ne, 1), jnp.int32)
        hi0 = jnp.full((ne, 1), 0x7F800000, jnp.int32)
        return lax.fori_loop(0, 31, body, (lo0, hi0))[0]

    thr = jnp.where(in_ctx, kth_largest(in_ctx, cap_ctx), kth_largest(jnp.logical_not(in_ctx), cap_lat))
    gt = bits > thr
    eq = bits == thr
    gtf = jnp.where(gt, 1.0, 0.0)
    n_gt_ctx = jnp.sum(jnp.where(in_ctx, gtf, 0.0), axis=1, keepdims=True)
    n_gt_lat = jnp.sum(jnp.where(in_ctx, 0.0, gtf), axis=1, keepdims=True)
    need = jnp.where(in_ctx, cap_ctx - n_gt_ctx, cap_lat - n_gt_lat)

    tr = lax.broadcasted_iota(jnp.int32, (ch, ch), 0)
    tc = lax.broadcasted_iota(jnp.int32, (ch, ch), 1)
    tri = jnp.where(tr <= tc, 1.0, 0.0).astype(BF16)

    def seg_prefix(flag):
        excl, bases = [], []
        base = jnp.zeros((ne, 1), F32)
        for c in range(nch):
            if c == ctx_ch:
                base = jnp.zeros((ne, 1), F32)
            fc = flag[:, c * ch:(c + 1) * ch]
            incl = jnp.dot(fc.astype(BF16), tri, preferred_element_type=F32)
            excl.append(base + incl - fc)
            bases.append(base)
            base = base + incl[:, ch - 1:ch]
        bases.append(base)
        return jnp.concatenate(excl, axis=1), bases

    eq_excl, _ = seg_prefix(jnp.where(eq, 1.0, 0.0))
    sel = jnp.logical_or(gt, jnp.logical_and(eq, eq_excl < need))
    self_ = jnp.where(sel, 1.0, 0.0)
    sel_excl, bases = seg_prefix(self_)
    seg_off = jnp.where(in_ctx, 0.0, float(cap_ctx))
    pos_ref[0] = jnp.where(sel, sel_excl + seg_off, -1.0).astype(jnp.int32)

    olane = lax.broadcasted_iota(jnp.int32, (ne, HEAD_LANES), 1)
    off = jnp.zeros((ne, HEAD_LANES), F32)
    for c in range(nch + 1):
        off = jnp.where(olane == c, bases[c] + (0.0 if c < ctx_ch else float(cap_ctx)), off)
    off_ref[0] = off.astype(jnp.int32)

    cnt = sel_excl + self_ + seg_off
    for e in range(ne):
        cnt_scr[e] = cnt[e:e + 1]
    n_slots = idx_ref.shape[1]
    sb = n_slots // IDX_SLOT_BLOCKS
    ilane = lax.broadcasted_iota(jnp.int32, (sb, HEAD_LANES), 1)
    for blk in range(IDX_SLOT_BLOCKS):
        slot = (lax.broadcasted_iota(jnp.int32, (sb, HEAD_LANES), 0) + blk * sb).astype(F32)

        def per_expert(e, out):
            def per_chunk(c, acc):
                cnt_row = cnt_scr[e, :, pl.ds(pl.multiple_of(c * HEAD_LANES, HEAD_LANES), HEAD_LANES)]
                return acc + jnp.where(cnt_row <= slot, 1.0, 0.0)
            acc = lax.fori_loop(0, l // HEAD_LANES, per_chunk, jnp.zeros((sb, HEAD_LANES), F32))
            return jnp.where(ilane == e, jnp.sum(acc, axis=1, keepdims=True), out)
        out = lax.fori_loop(0, ne, per_expert, jnp.zeros((sb, HEAD_LANES), F32))
        idx_ref[0, blk * sb:(blk + 1) * sb, :] = out.astype(jnp.int32)


def _topk(aff_t, *, m_ctx, cap_ctx, cap_lat):
    b, ne, l = aff_t.shape
    n_slots = cap_ctx + cap_lat
    kern = functools.partial(_topk_kernel, m_ctx=m_ctx, cap_ctx=cap_ctx, cap_lat=cap_lat)
    return pl.pallas_call(
        kern,
        grid=(b,),
        in_specs=[pl.BlockSpec((1, ne, l), lambda bb: (bb, 0, 0))],
        out_specs=[pl.BlockSpec((1, ne, l), lambda bb: (bb, 0, 0)),
                   pl.BlockSpec((1, ne, HEAD_LANES), lambda bb: (bb, 0, 0)),
                   pl.BlockSpec((1, n_slots, HEAD_LANES), lambda bb: (bb, 0, 0))],
        out_shape=[jax.ShapeDtypeStruct((b, ne, l), jnp.int32),
                   jax.ShapeDtypeStruct((b, ne, HEAD_LANES), jnp.int32),
                   jax.ShapeDtypeStruct((b, n_slots, HEAD_LANES), jnp.int32)],
        scratch_shapes=[pltpu.VMEM((ne, 1, l), F32)],
        compiler_params=_cparams(("arbitrary",)),
        name="expert_choice_topk",
    )(aff_t)


def _ffn_kernel(idx_ref, h_hbm, wg_ref, wu_ref, wd_ref, y_ref, xg_scr, xb_scr, acc_scr, sem, *, n_experts):
    e, b, f = pl.program_id(0), pl.program_id(1), pl.program_id(2)
    n_slots = xg_scr.shape[0]

    @pl.when(f == 0)
    def _():
        base = (b * n_experts + e) * n_slots

        def issue(r, carry):
            pltpu.make_async_copy(h_hbm.at[b, pl.ds(idx_ref[base + r], 1)], xg_scr.at[pl.ds(r, 1)], sem.at[0]).start()
            return carry
        lax.fori_loop(0, n_slots, issue, 0)

        def drain(r, carry):
            pltpu.make_async_copy(h_hbm.at[b, pl.ds(0, 1)], xg_scr.at[pl.ds(r, 1)], sem.at[0]).wait()
            return carry
        lax.fori_loop(0, n_slots, drain, 0)
        xb_scr[...] = xg_scr[...].astype(BF16)
        acc_scr[...] = jnp.zeros(acc_scr.shape, F32)

    xb = xb_scr[...]
    a = jnp.dot(xb, wg_ref[0].astype(BF16), preferred_element_type=F32)
    u = jnp.dot(xb, wu_ref[0].astype(BF16), preferred_element_type=F32)
    hid = (_silu(a) * u).astype(BF16)
    acc_scr[...] += jnp.dot(hid, wd_ref[0].astype(BF16), preferred_element_type=F32)

    @pl.when(f == pl.num_programs(2) - 1)
    def _():
        y_ref[0, 0] = acc_scr[...].astype(y_ref.dtype)


def _expert_ffn(idx_flat, h2, w_gate, w_up, w_down, *, n_slots):
    b, l, d = h2.shape
    ne, _, ff = w_gate.shape
    tf = 256
    grid_spec = pltpu.PrefetchScalarGridSpec(
        num_scalar_prefetch=1,
        grid=(ne, b, ff // tf),
        in_specs=[pl.BlockSpec(memory_space=pl.ANY),
                  pl.BlockSpec((1, d, tf), lambda e, bb, f, idx: (e, 0, f)),
                  pl.BlockSpec((1, d, tf), lambda e, bb, f, idx: (e, 0, f)),
                  pl.BlockSpec((1, tf, d), lambda e, bb, f, idx: (e, f, 0))],
        out_specs=pl.BlockSpec((1, 1, n_slots, d), lambda e, bb, f, idx: (bb, e, 0, 0)),
        scratch_shapes=[pltpu.VMEM((n_slots, d), F32), pltpu.VMEM((n_slots, d), BF16),
                        pltpu.VMEM((n_slots, d), F32), pltpu.SemaphoreType.DMA((1,))])
    return pl.pallas_call(
        functools.partial(_ffn_kernel, n_experts=ne),
        grid_spec=grid_spec,
        out_shape=jax.ShapeDtypeStruct((b, ne, n_slots, d), BF16),
        compiler_params=_cparams(("arbitrary", "arbitrary", "arbitrary")),
        name="expert_ffn",
    )(idx_flat, h2, w_gate, w_up, w_down)


def _combine_kernel(off_ref, x1_ref, aff_ref, pos_ref, mod_ref, lnw_ref, lnb_ref, y_hbm, o_ref, win_scr, sem, *,
                    alpha, n_experts):
    b, i = pl.program_id(0), pl.program_id(1)
    tm = x1_ref.shape[1]
    win = win_scr.shape[1]
    n_slots = y_hbm.shape[2]
    copies = []
    for e in range(n_experts):
        p0 = off_ref[(b * n_experts + e) * HEAD_LANES + i]
        start = jnp.minimum((p0 // WIN_ALIGN) * WIN_ALIGN, n_slots - win)
        start = pl.multiple_of(start, WIN_ALIGN)
        cp = pltpu.make_async_copy(y_hbm.at[b, e, pl.ds(start, win)], win_scr.at[e], sem.at[e])
        cp.start()
        copies.append((start, cp))

    pos = pos_ref[0]
    aff = aff_ref[0]
    scol = lax.broadcasted_iota(jnp.int32, (tm, win), 1)
    acc = jnp.zeros((tm, x1_ref.shape[2]), F32)
    for e in range(n_experts):
        start, cp = copies[e]
        cp.wait()
        onehot = jnp.where((pos[:, e:e + 1] - start) == scol, 1.0, 0.0).astype(BF16)
        acc = acc + aff[:, e:e + 1] * jnp.dot(onehot, win_scr[e], preferred_element_type=F32)

    mod = mod_ref[0, 0]
    o_ref[0] = _layer_norm(alpha * x1_ref[0] + mod[5:6] * acc, lnw_ref[...], lnb_ref[...])


def _combine(off_flat, x1, aff, pos_tok, modtab, ln_w, ln_b, y, *, m_ctx, alpha):
    b, l, d = x1.shape
    ne = aff.shape[2]
    tm = TILE
    win = tm + WIN_ALIGN
    ctx_tiles = m_ctx // tm

    def rows(width):
        return pl.BlockSpec((1, tm, width), lambda bb, i, off: (bb, i, 0))

    grid_spec = pltpu.PrefetchScalarGridSpec(
        num_scalar_prefetch=1,
        grid=(b, l // tm),
        in_specs=[rows(d), rows(ne), rows(ne),
                  pl.BlockSpec((1, 1, 6, d), lambda bb, i, off: (bb, jnp.minimum(i // ctx_tiles, 1), 0, 0)),
                  pl.BlockSpec((1, d), lambda bb, i, off: (0, 0)),
                  pl.BlockSpec((1, d), lambda bb, i, off: (0, 0)),
                  pl.BlockSpec(memory_space=pl.ANY)],
        out_specs=rows(d),
        scratch_shapes=[pltpu.VMEM((ne, win, d), BF16), pltpu.SemaphoreType.DMA((ne,))])
    return pl.pallas_call(
        functools.partial(_combine_kernel, alpha=alpha, n_experts=ne),
        grid_spec=grid_spec,
        out_shape=jax.ShapeDtypeStruct((b, l, d), F32),
        compiler_params=_cparams(("arbitrary", "arbitrary")),
        name="moe_combine_ln",
    )(off_flat, x1, aff, pos_tok, modtab, ln_w.reshape(1, d), ln_b.reshape(1, d), y)


def kernel(x, c, ctx, c_ctx, w_mod, b_mod, w_in, w_conv, lambda_qk, subln_w, hgrn_lb_logits, hgrn_norm_w, w_out,
           ln_w, ln_b, w_router, w_gate, w_up, w_down):
    bsz, n_lat, d = x.shape
    m_ctx = ctx.shape[1]
    depth = w_mod.shape[0]
    ne = w_router.shape[2]
    aw, cw, hw = d // 2, d // 4, d // 4
    assert m_ctx % TILE == 0 and n_lat % 1024 == 0 and n_lat % GRID_W == 0 and bsz + 1 <= 8
    assert (m_ctx + n_lat) % (8 * 16) == 0 and aw == 1024 and ne <= HEAD_LANES
    alpha = (2.0 * depth) ** 0.25
    cap_ctx = EC_CAPACITY_FACTOR * m_ctx // ne
    cap_lat = EC_CAPACITY_FACTOR * n_lat // ne
    n_slots = cap_ctx + cap_lat
    assert n_slots % WIN_ALIGN == 0 and n_slots >= TILE + WIN_ALIGN

    cs = jnp.zeros((8, d), F32).at[:bsz].set(c).at[bsz].set(c_ctx)
    mod = _modulation(cs, w_mod, b_mod)
    mod_lat = mod[:, :bsz].reshape(depth, bsz, 1, 6, d)
    mod_ctx = jnp.broadcast_to(mod[:, bsz].reshape(depth, 1, 1, 6, d), (depth, bsz, 1, 6, d))
    modtab = jnp.concatenate([mod_ctx, mod_lat], axis=2)

    tables = _rope_tables(m_ctx, n_lat)
    xa = jnp.concatenate([ctx, x], axis=1)
    w_router_pad = jnp.zeros((depth, d, HEAD_LANES), F32).at[:, :, :ne].set(w_router)
    hg_heads = hw // HEAD_LANES
    cb0 = 3 * cw // HEAD_LANES

    for l in range(depth):
        lambda_init = 0.8 - 0.6 * math.exp(-0.3 * l)
        w_in_bf = w_in[l].astype(BF16)
        qkv = _inproj(xa, modtab[l], w_in_bf, tables, col0=0, ncols=3 * aw, out_dtype=BF16, rope_tiles=2,
                      m_ctx=m_ctx)
        cg = _inproj(xa, modtab[l], w_in_bf, tables, col0=3 * aw, ncols=3 * cw + 5 * hw, out_dtype=F32,
                     rope_tiles=0, m_ctx=m_ctx)
        att = _attention(qkv, lambda_qk[l], subln_w[l], m_ctx=m_ctx, lambda_init=lambda_init)
        o_f, o_b = _hgrn(cg, hgrn_lb_logits, layer=l, hg_heads=hg_heads, col_q=cb0, col_i=cb0 + hg_heads,
                         col_ff=cb0 + 3 * hg_heads, col_fb=cb0 + 4 * hg_heads)
        x1, h2, aff = _outproj(xa, att, cg, o_f, o_b, w_conv[l], hgrn_norm_w[l], w_out[l].astype(BF16), modtab[l],
                               ln_w[l, 0], ln_b[l, 0], w_router_pad[l], m_ctx=m_ctx, alpha=alpha, n_experts=ne,
                               cw=cw, col_gg=(3 * cw + 2 * hw) // hw)
        pos, off, idx = _topk(jnp.swapaxes(aff, 1, 2), m_ctx=m_ctx, cap_ctx=cap_ctx, cap_lat=cap_lat)
        idx_flat = jnp.swapaxes(idx[:, :, :ne], 1, 2).reshape(-1)
        y = _expert_ffn(idx_flat, h2, w_gate[l], w_up[l], w_down[l], n_slots=n_slots)
        xa = _combine(off.reshape(-1), x1, aff, jnp.swapaxes(pos, 1, 2), modtab[l], ln_w[l, 1], ln_b[l, 1], y,
                      m_ctx=m_ctx, alpha=alpha)
    return xa[:, m_ctx:]
```

```python
import functools
import math

import jax
import jax.numpy as jnp
from jax import lax
from jax.experimental import pallas as pl
from jax.experimental.pallas import tpu as pltpu

F32 = jnp.float32
BF16 = jnp.bfloat16
HIGHEST = lax.Precision.HIGHEST

GRID_W = 64
ROPE_THETA = 10000.0
ATT_HEAD_DIM = 64
HEAD_LANES = 128
HGRN_CHUNK = 64
EC_CAPACITY_FACTOR = 2
EPS = 1e-6
TILE = 256
COMBINE_TILE = 128
IDX_SLOT_BLOCKS = 4
WIN_ALIGN = 16
VMEM_LIMIT = 56 * 1024 * 1024


def _cparams(sem):
    return pltpu.CompilerParams(dimension_semantics=sem, vmem_limit_bytes=VMEM_LIMIT)


def _silu(x):
    return x * jax.nn.sigmoid(x)


def _mod_kernel(cs_ref, w_ref, b_ref, o_ref):
    a = _silu(cs_ref[...])
    o_ref[0] = jnp.dot(a, w_ref[0], precision=HIGHEST, preferred_element_type=F32) + b_ref[0]


def _modulation(cs, w_mod, b_mod):
    depth, d, n6 = w_mod.shape
    tn = 1024
    return pl.pallas_call(
        _mod_kernel,
        grid=(depth, n6 // tn),
        in_specs=[pl.BlockSpec((8, d), lambda l, j: (0, 0)),
                  pl.BlockSpec((1, d, tn), lambda l, j: (l, 0, j)),
                  pl.BlockSpec((1, 1, tn), lambda l, j: (l, 0, j))],
        out_specs=pl.BlockSpec((1, 8, tn), lambda l, j: (l, 0, j)),
        out_shape=jax.ShapeDtypeStruct((depth, 8, n6), F32),
        compiler_params=_cparams(("arbitrary", "arbitrary")),
        name="modulation",
    )(cs, w_mod, b_mod.reshape(depth, 1, n6))


def _inproj_kernel(x_ref, mod_ref, w_ref, rc_ref, ra_ref, rb_ref, o_ref, h_scr, *, m_ctx, rope_tiles):
    i = pl.program_id(1)
    j = pl.program_id(2)
    tm = x_ref.shape[1]

    @pl.when(j == 0)
    def _():
        row = i * tm + lax.broadcasted_iota(jnp.int32, (tm, 1), 0)
        is_ctx = row < m_ctx
        mod = mod_ref[0]
        sh = jnp.where(is_ctx, mod[0, 0:1], mod[1, 0:1])
        sc = jnp.where(is_ctx, mod[0, 1:2], mod[1, 1:2])
        h_scr[...] = (x_ref[0] * (1.0 + sc) + sh).astype(BF16)

    acc = jnp.dot(h_scr[...], w_ref[...], preferred_element_type=F32)

    if rope_tiles == 0:
        o_ref[0] = acc.astype(o_ref.dtype)
        return

    @pl.when(j < rope_tiles)
    def _():
        rc, ra, rb = rc_ref[...], ra_ref[...], rb_ref[...]
        qscale = jnp.where(j == 0, ATT_HEAD_DIM ** -0.5 * math.log2(math.e), 1.0).astype(F32)
        for c in range(acc.shape[1] // HEAD_LANES):
            blk = acc[:, c * HEAD_LANES:(c + 1) * HEAD_LANES]
            rot = (blk * rc + pltpu.roll(blk, 16, 1) * ra + pltpu.roll(blk, HEAD_LANES - 16, 1) * rb)
            o_ref[0, :, c * HEAD_LANES:(c + 1) * HEAD_LANES] = (rot * qscale).astype(o_ref.dtype)

    @pl.when(j >= rope_tiles)
    def _():
        o_ref[0] = acc.astype(o_ref.dtype)


def _inproj(xa, modtab, w_bf, tables, *, col0, ncols, out_dtype, rope_tiles, m_ctx):
    b, l, d = xa.shape
    tn = 1024
    tm = l // 8
    joff = col0 // tn
    kern = functools.partial(_inproj_kernel, m_ctx=m_ctx, rope_tiles=rope_tiles)
    tab_spec = pl.BlockSpec((tm, HEAD_LANES), lambda bb, i, j: (i, 0))
    return pl.pallas_call(
        kern,
        grid=(b, l // tm, ncols // tn),
        in_specs=[pl.BlockSpec((1, tm, d), lambda bb, i, j: (bb, i, 0)),
                  pl.BlockSpec((1, 2, 6, d), lambda bb, i, j: (bb, 0, 0, 0)),
                  pl.BlockSpec((d, tn), lambda bb, i, j: (0, j + joff)),
                  tab_spec, tab_spec, tab_spec],
        out_specs=pl.BlockSpec((1, tm, tn), lambda bb, i, j: (bb, i, j)),
        out_shape=jax.ShapeDtypeStruct((b, l, ncols), out_dtype),
        scratch_shapes=[pltpu.VMEM((tm, d), BF16)],
        compiler_params=_cparams(("arbitrary", "arbitrary", "arbitrary")),
        name="inproj",
    )(xa, modtab, w_bf, *tables)


def _rope_tables(m_ctx, n_lat):
    nf = ATT_HEAD_DIM // 4
    inv = ROPE_THETA ** (-jnp.arange(nf, dtype=F32) / nf)
    rows = n_lat // GRID_W
    row = jnp.repeat(jnp.arange(rows, dtype=F32), GRID_W)
    col = jnp.tile(jnp.arange(GRID_W, dtype=F32), rows)
    ar, ac = row[:, None] * inv, col[:, None] * inv
    cr, sr, cc, sc = jnp.cos(ar), jnp.sin(ar), jnp.cos(ac), jnp.sin(ac)
    z = jnp.zeros_like(sr)
    c64 = jnp.concatenate([cr, cr, cc, cc], axis=1)
    a64 = jnp.concatenate([z, sr, z, sc], axis=1)
    b64 = jnp.concatenate([-sr, z, -sc, z], axis=1)
    reps = HEAD_LANES // ATT_HEAD_DIM

    def full(t64, fill):
        lat = jnp.tile(t64, (1, reps))
        return jnp.concatenate([jnp.full((m_ctx, HEAD_LANES), fill, F32), lat], axis=0)

    return full(c64, 1.0), full(a64, 0.0), full(b64, 0.0)


def _attn_kernel(lq_ref, sw_ref, q_ref, k_ref, v_ref, o_ref, vt_scr, m0_scr, l0_scr, acc0_scr, m1_scr, l1_scr,
                 acc1_scr, sa0_scr, sa1_scr, sb0_scr, sb1_scr, *, m_ctx, tk, lambda_init):
    i = pl.program_id(2)
    tq = q_ref.shape[1]
    n_lat = k_ref.shape[1] - m_ctx

    @pl.when(i == 0)
    def _():
        vt_scr[...] = v_ref[0].T

    q = q_ref[0]
    lane = lax.broadcasted_iota(jnp.int32, q.shape, 1)
    zero = jnp.zeros_like(q)
    qs = (jnp.where(lane < ATT_HEAD_DIM, q, zero), jnp.where(lane >= ATT_HEAD_DIM, q, zero))
    stats = ((m0_scr, l0_scr, acc0_scr), (m1_scr, l1_scr, acc1_scr))
    for m_scr, l_scr, acc_scr in stats:
        m_scr[...] = jnp.full(m_scr.shape, -jnp.inf, F32)
        l_scr[...] = jnp.zeros(l_scr.shape, F32)
        acc_scr[...] = jnp.zeros(acc_scr.shape, F32)

    def scores(start, size, st_refs):
        kc = k_ref[0, pl.ds(start, size), :]
        cmax = []
        for s in range(2):
            st = lax.dot_general(kc, qs[s], (((1,), (1,)), ((), ())), preferred_element_type=F32)
            st_refs[s][0:size, :] = st
            cmax.append(jnp.max(st, axis=0, keepdims=True))
        return tuple(cmax)

    def accumulate(start, size, st_refs, cmax):
        vt = vt_scr[:, pl.ds(start, size)]
        for s, (m_scr, l_scr, acc_scr) in enumerate(stats):
            m_old = m_scr[...]
            m_new = jnp.maximum(m_old, cmax[s])
            alpha = jnp.exp2(m_old - m_new)
            p = jnp.exp2(st_refs[s][0:size, :] - m_new)
            l_scr[...] = alpha * l_scr[...] + jnp.sum(p, axis=0, keepdims=True)
            acc_scr[...] = alpha * acc_scr[...] + jnp.dot(vt, p.astype(BF16), preferred_element_type=F32)
            m_scr[...] = m_new

    st_a, st_b = (sa0_scr, sa1_scr), (sb0_scr, sb1_scr)
    accumulate(0, m_ctx, st_a, scores(0, m_ctx, st_a))

    @pl.when(i * tq >= m_ctx)
    def _():
        def at(c):
            return pl.multiple_of(m_ctx + c * tk, math.gcd(m_ctx, tk))
        nck = n_lat // tk

        def body(c2, cm_a):
            c = 2 * c2
            cm_b = scores(at(c + 1), tk, st_b)
            accumulate(at(c), tk, st_a, cm_a)
            cm_a = scores(at(c + 2), tk, st_a)
            accumulate(at(c + 1), tk, st_b, cm_b)
            return cm_a
        cm_a = lax.fori_loop(0, nck // 2 - 1, body, scores(at(0), tk, st_a))
        cm_b = scores(at(nck - 1), tk, st_b)
        accumulate(at(nck - 2), tk, st_a, cm_a)
        accumulate(at(nck - 1), tk, st_b, cm_b)

    lq = lq_ref[...]
    lam = (jnp.exp(jnp.sum(lq[0:1] * lq[1:2], axis=1, keepdims=True))
           - jnp.exp(jnp.sum(lq[2:3] * lq[3:4], axis=1, keepdims=True)) + lambda_init)
    ot = acc0_scr[...] / l0_scr[...] - lam * (acc1_scr[...] / l1_scr[...])
    ms = jnp.mean(ot * ot, axis=0, keepdims=True)
    o = (ot * lax.rsqrt(ms + EPS)).T * sw_ref[...] * (1.0 - lambda_init)
    o_ref[0] = o.astype(o_ref.dtype)


def _attention(qkv, lambda_qk, subln_w, *, m_ctx, lambda_init):
    b, l, w3 = qkv.shape
    heads = w3 // (3 * HEAD_LANES)
    tq = TILE
    tk = 512
    kern = functools.partial(_attn_kernel, m_ctx=m_ctx, tk=tk, lambda_init=lambda_init)
    return pl.pallas_call(
        kern,
        grid=(b, heads, l // tq),
        in_specs=[pl.BlockSpec((4, ATT_HEAD_DIM), lambda bb, h, i: (0, 0)),
                  pl.BlockSpec((1, HEAD_LANES), lambda bb, h, i: (0, 0)),
                  pl.BlockSpec((1, tq, HEAD_LANES), lambda bb, h, i: (bb, i, h)),
                  pl.BlockSpec((1, l, HEAD_LANES), lambda bb, h, i: (bb, 0, heads + h)),
                  pl.BlockSpec((1, l, HEAD_LANES), lambda bb, h, i: (bb, 0, 2 * heads + h))],
        out_specs=pl.BlockSpec((1, tq, HEAD_LANES), lambda bb, h, i: (bb, i, h)),
        out_shape=jax.ShapeDtypeStruct((b, l, heads * HEAD_LANES), BF16),
        scratch_shapes=[pltpu.VMEM((HEAD_LANES, l), BF16)]
        + [pltpu.VMEM((1, tq), F32), pltpu.VMEM((1, tq), F32), pltpu.VMEM((HEAD_LANES, tq), F32)] * 2
        + [pltpu.VMEM((max(tk, m_ctx), tq), F32)] * 4,
        compiler_params=_cparams(("arbitrary", "arbitrary", "arbitrary")),
        name="diff_attention",
    )(lambda_qk, subln_w.reshape(1, HEAD_LANES), qkv, qkv, qkv)


def _log_sigmoid(z):
    return jnp.minimum(z, 0.0) - jnp.log1p(jnp.exp(-jnp.abs(z)))


def _forget_gate(z, lb):
    ls = _log_sigmoid(z)
    key = jax.nn.sigmoid(-z)
    if lb is None:
        return ls, key
    a = jnp.log(lb)
    b = jnp.log1p(-lb) + ls
    logf = jnp.maximum(a, b) + jnp.log1p(jnp.exp(-jnp.abs(a - b)))
    return logf, (1.0 - lb) * key


def _hgrn_tile(dirs):
    ch = HGRN_CHUNK
    nt_dims = (((1,), (1,)), ((), ()))
    work = []
    for q_ref, v_ref, z_ref, o_ref, s_scr, lb, reverse in dirs:
        t = q_ref.shape[1]
        logf, key = _forget_gate(z_ref[0], lb)
        r = lax.broadcasted_iota(jnp.int32, (t, t), 0)
        c = lax.broadcasted_iota(jnp.int32, (t, t), 1)
        tri = ((r // ch) == (c // ch)) & ((c >= r) if reverse else (c <= r))
        a = jnp.dot(tri.astype(F32), logf, precision=HIGHEST, preferred_element_type=F32)
        work.append(dict(q=q_ref[0], v=v_ref[0].astype(BF16), key=key, a=a, tri=tri, o_ref=o_ref, s_scr=s_scr,
                         reverse=reverse, nch=t // ch))

    for w in work:
        a, nch = w["a"], w["nch"]

        def per_chunk(row_of):
            return jnp.concatenate([jnp.broadcast_to(row_of(a[k * ch:(k + 1) * ch]), (ch, a.shape[1]))
                                    for k in range(nch)], axis=0)
        a_mid = per_chunk(lambda ac: ac[ch // 2 - 1:ch // 2])
        a_end = per_chunk((lambda ac: ac[0:1]) if w["reverse"] else (lambda ac: ac[ch - 1:ch]))
        qe = (w["q"] * jnp.exp(a - a_mid)).astype(BF16)
        ke = (w["key"] * jnp.exp(a_mid - a)).astype(BF16)
        kd = (w["key"] * jnp.exp(a_end - a)).astype(BF16)
        w["qa"] = (w["q"] * jnp.exp(a)).astype(BF16)
        w["decay"] = jnp.exp(a_end)
        w["sc"] = lax.dot_general(qe, ke, nt_dims, preferred_element_type=F32)
        w["kv"] = [lax.dot_general(w["v"][k * ch:(k + 1) * ch], kd[k * ch:(k + 1) * ch], (((0,), (0,)), ((), ())),
                                   preferred_element_type=F32) for k in range(nch)]

    for w in work:
        w["intra"] = jnp.dot(jnp.where(w["tri"], w["sc"], 0.0).astype(BF16), w["v"], preferred_element_type=F32)

    for w in work:
        nch = w["nch"]
        st = w["s_scr"][...]
        states = [None] * nch
        for k in (range(nch - 1, -1, -1) if w["reverse"] else range(nch)):
            states[k] = st.astype(BF16)
            row = k * ch if w["reverse"] else (k + 1) * ch - 1
            st = w["decay"][row:row + 1] * st + w["kv"][k]
        w["s_scr"][...] = st
        w["states"] = states

    for w in work:
        inter = [lax.dot_general(w["qa"][k * ch:(k + 1) * ch], w["states"][k], nt_dims, preferred_element_type=F32)
                 for k in range(w["nch"])]
        w["o_ref"][0] = jnp.concatenate(inter, axis=0) + w["intra"]


def _hgrn_kernel(lbl_ref, qf_ref, vf_ref, zf_ref, qb_ref, vb_ref, zb_ref, of_ref, ob_ref, sf_scr, sb_scr, *, layer):
    @pl.when(pl.program_id(2) == 0)
    def _():
        sf_scr[...] = jnp.zeros(sf_scr.shape, F32)
        sb_scr[...] = jnp.zeros(sb_scr.shape, F32)

    if layer == 0:
        lbf = lbb = None
    else:
        lg = lbl_ref[...]
        ex = jnp.exp(lg - jnp.max(lg, axis=1, keepdims=True))
        sm = ex / jnp.sum(ex, axis=1, keepdims=True)
        lb = sm[:, 1]
        for k in range(2, layer + 1):
            lb = lb + sm[:, k]
        lbf, lbb = lb[0:1], lb[1:2]
    _hgrn_tile([(qf_ref, vf_ref, zf_ref, of_ref, sf_scr, lbf, False),
                (qb_ref, vb_ref, zb_ref, ob_ref, sb_scr, lbb, True)])


def _hgrn(cg, lb_logits, *, layer, hg_heads, col_q, col_i, col_ff, col_fb):
    b, l, _ = cg.shape
    t = TILE
    nb = l // t
    depth = lb_logits.shape[1]

    def fwd(col):
        return pl.BlockSpec((1, t, HEAD_LANES), lambda bb, h, i: (bb, i, col + h))

    def bwd(col):
        return pl.BlockSpec((1, t, HEAD_LANES), lambda bb, h, i: (bb, jnp.where(i == 0, 0, nb - i), col + h))

    out_f = pl.BlockSpec((1, t, HEAD_LANES), lambda bb, h, i: (bb, i, h))
    out_b = pl.BlockSpec((1, t, HEAD_LANES), lambda bb, h, i: (bb, jnp.where(i == 0, 0, nb - i), h))
    shp = jax.ShapeDtypeStruct((b, l, hg_heads * HEAD_LANES), F32)
    return pl.pallas_call(
        functools.partial(_hgrn_kernel, layer=layer),
        grid=(b, hg_heads, nb),
        in_specs=[pl.BlockSpec((2, depth, HEAD_LANES), lambda bb, h, i: (0, 0, h)),
                  fwd(col_q), fwd(col_i), fwd(col_ff), bwd(col_q), bwd(col_i), bwd(col_fb)],
        out_specs=[out_f, out_b],
        out_shape=[shp, shp],
        scratch_shapes=[pltpu.VMEM((HEAD_LANES, HEAD_LANES), F32), pltpu.VMEM((HEAD_LANES, HEAD_LANES), F32)],
        compiler_params=_cparams(("arbitrary", "arbitrary", "arbitrary")),
        name="hgrn2",
    )(lb_logits, cg, cg, cg, cg, cg, cg)


def _layer_norm(r, w, b):
    mu = jnp.mean(r, axis=1, keepdims=True)
    var = jnp.mean(jnp.square(r - mu), axis=1, keepdims=True)
    return (r - mu) * lax.rsqrt(var + EPS) * w + b


def _outproj_kernel(x_ref, att_ref, cx_ref, cb_ref, cc_ref, cxp_ref, ccp_ref, cxn_ref, ccn_ref, of_ref, ob_ref,
                    gg_ref, wconv_ref, hnw_ref, wout_ref, mod_ref, lnw_ref, lnb_ref, wr_ref,
                    x1_ref, h2_ref, aff_ref, *, ctx_tiles, alpha, n_experts):
    i = pl.program_id(1)
    nt = pl.num_programs(1)
    tm = x_ref.shape[1]
    aw = att_ref.shape[2]
    cw = cx_ref.shape[2]

    u = cc_ref[0] * cx_ref[0]
    prev_ok = jnp.logical_and(i != 0, i != ctx_tiles)
    next_ok = jnp.logical_and(i != ctx_tiles - 1, i != nt - 1)
    u_before = jnp.where(prev_ok, (ccp_ref[0] * cxp_ref[0])[7:8], 0.0)
    u_after = jnp.where(next_ok, (ccn_ref[0] * cxn_ref[0])[0:1], 0.0)
    row = lax.broadcasted_iota(jnp.int32, (tm, 1), 0)
    u_prev = jnp.where(row == 0, u_before, pltpu.roll(u, 1, 0))
    u_next = jnp.where(row == tm - 1, u_after, pltpu.roll(u, tm - 1, 0))
    wc = wconv_ref[...]
    conv = cb_ref[0] * (u_prev * wc[0:1] + u * wc[1:2] + u_next * wc[2:3])

    o = of_ref[0] + ob_ref[0]
    gg = gg_ref[0]
    recs = []
    for h in range(o.shape[1] // HEAD_LANES):
        oh = o[:, h * HEAD_LANES:(h + 1) * HEAD_LANES]
        ms = jnp.mean(oh * oh, axis=1, keepdims=True)
        recs.append(oh * lax.rsqrt(ms + EPS) * hnw_ref[...] * _silu(gg[:, h * HEAD_LANES:(h + 1) * HEAD_LANES]))
    rec = jnp.concatenate(recs, axis=1)

    y = jnp.dot(att_ref[0], wout_ref[0:aw, :], preferred_element_type=F32)
    y = y + jnp.dot(conv.astype(BF16), wout_ref[aw:aw + cw, :], preferred_element_type=F32)
    y = y + jnp.dot(rec.astype(BF16), wout_ref[aw + cw:, :], preferred_element_type=F32)

    mod = mod_ref[0, 0]
    x1 = _layer_norm(alpha * x_ref[0] + mod[2:3] * y, lnw_ref[...], lnb_ref[...])
    x1_ref[0] = x1
    h2 = x1 * (1.0 + mod[4:5]) + mod[3:4]
    d = h2.shape[1]
    h2_ref[0, :, :d] = h2
    wr = wr_ref[...]
    wr_hi = wr.astype(BF16)
    wr_lo = (wr - wr_hi.astype(F32)).astype(BF16)
    h2_hi = h2.astype(BF16)
    h2_lo = (h2 - h2_hi.astype(F32)).astype(BF16)
    logits = (jnp.dot(h2_hi, wr_hi, preferred_element_type=F32) + jnp.dot(h2_hi, wr_lo, preferred_element_type=F32)
              + jnp.dot(h2_lo, wr_hi, preferred_element_type=F32))
    lane = lax.broadcasted_iota(jnp.int32, logits.shape, 1)
    logits = jnp.where(lane < n_experts, logits, -jnp.inf)
    ex = jnp.exp(logits - jnp.max(logits, axis=1, keepdims=True))
    aff = ex / jnp.sum(ex, axis=1, keepdims=True)
    aff_ref[0] = aff[:, :n_experts]
    h2_ref[0, :, d:] = aff


def _outproj(xa, att, cg, o_f, o_b, w_conv, hgrn_norm_w, w_out_bf, modtab, ln_w, ln_b, w_router_pad, *,
             m_ctx, alpha, n_experts, cw, col_gg):
    b, l, d = xa.shape
    tm = TILE
    aw = att.shape[2]
    hw = o_f.shape[2]
    r8 = tm // 8
    last8 = l // 8 - 1
    ctx_tiles = m_ctx // tm

    def rows(width, col):
        return pl.BlockSpec((1, tm, width), lambda bb, i: (bb, i, col))

    def halo_prev(col):
        return pl.BlockSpec((1, 8, cw), lambda bb, i: (bb, jnp.maximum(i * r8 - 1, 0), col))

    def halo_next(col):
        return pl.BlockSpec((1, 8, cw), lambda bb, i: (bb, jnp.minimum((i + 1) * r8, last8), col))

    def const(shape):
        return pl.BlockSpec(shape, lambda bb, i: (0,) * len(shape))

    kern = functools.partial(_outproj_kernel, ctx_tiles=ctx_tiles, alpha=alpha, n_experts=n_experts)
    return pl.pallas_call(
        kern,
        grid=(b, l // tm),
        in_specs=[rows(d, 0), rows(aw, 0),
                  rows(cw, 0), rows(cw, 1), rows(cw, 2),
                  halo_prev(0), halo_prev(2), halo_next(0), halo_next(2),
                  rows(hw, 0), rows(hw, 0), rows(hw, col_gg),
                  const((3, cw)), const((1, HEAD_LANES)), const((d, d)),
                  pl.BlockSpec((1, 1, 6, d), lambda bb, i: (bb, jnp.minimum(i // ctx_tiles, 1), 0, 0)),
                  const((1, d)), const((1, d)), const((d, HEAD_LANES))],
        out_specs=[rows(d, 0), rows(d + HEAD_LANES, 0), rows(n_experts, 0)],
        out_shape=[jax.ShapeDtypeStruct((b, l, d), F32), jax.ShapeDtypeStruct((b, l, d + HEAD_LANES), F32),
                   jax.ShapeDtypeStruct((b, l, n_experts), F32)],
        compiler_params=_cparams(("arbitrary", "arbitrary")),
        name="outproj_ln_router",
    )(xa, att, cg, cg, cg, cg, cg, cg, cg, o_f, o_b, cg, w_conv, hgrn_norm_w.reshape(1, HEAD_LANES), w_out_bf,
      modtab, ln_w.reshape(1, d), ln_b.reshape(1, d), w_router_pad)


def _topk_kernel(aff_ref, pos_ref, off_ref, idx_ref, cnt_scr, *, m_ctx, cap_ctx, cap_lat):
    aff = aff_ref[0]
    ne, l = aff.shape
    ch = COMBINE_TILE
    nch = l // ch
    ctx_ch = m_ctx // ch
    bits = pltpu.bitcast(aff, jnp.int32)
    lane = lax.broadcasted_iota(jnp.int32, (ne, l), 1)
    in_ctx = lane < m_ctx

    def kth_largest(seg, k):
        def body(_, lohi):
            lo, hi = lohi
            mid = lo + lax.shift_right_logical(hi - lo, 1)
            cnt = jnp.sum(jnp.where(jnp.logical_and(seg, bits >= mid), 1.0, 0.0), axis=1, keepdims=True)
            ge = cnt >= k
            return jnp.where(ge, mid, lo), jnp.where(ge, hi, mid)
        lo0 = jnp.zeros((ne, 1), jnp.int32)
        hi0 = jnp.full((ne, 1), 0x7F800000, jnp.int32)
        return lax.fori_loop(0, 31, body, (lo0, hi0))[0]

    thr = jnp.where(in_ctx, kth_largest(in_ctx, cap_ctx), kth_largest(jnp.logical_not(in_ctx), cap_lat))
    gt = bits > thr
    eq = bits == thr
    gtf = jnp.where(gt, 1.0, 0.0)
    n_gt_ctx = jnp.sum(jnp.where(in_ctx, gtf, 0.0), axis=1, keepdims=True)
    n_gt_lat = jnp.sum(jnp.where(in_ctx, 0.0, gtf), axis=1, keepdims=True)
    need = jnp.where(in_ctx, cap_ctx - n_gt_ctx, cap_lat - n_gt_lat)

    tr = lax.broadcasted_iota(jnp.int32, (ch, ch), 0)
    tc = lax.broadcasted_iota(jnp.int32, (ch, ch), 1)
    tri = jnp.where(tr <= tc, 1.0, 0.0).astype(BF16)

    def seg_prefix(flag):
        excl, bases = [], []
        base = jnp.zeros((ne, 1), F32)
        for c in range(nch):
            if c == ctx_ch:
                base = jnp.zeros((ne, 1), F32)
            fc = flag[:, c * ch:(c + 1) * ch]
            incl = jnp.dot(fc.astype(BF16), tri, preferred_element_type=F32)
            excl.append(base + incl - fc)
            bases.append(base)
            base = base + incl[:, ch - 1:ch]
        bases.append(base)
        return jnp.concatenate(excl, axis=1), bases

    eq_excl, _ = seg_prefix(jnp.where(eq, 1.0, 0.0))
    sel = jnp.logical_or(gt, jnp.logical_and(eq, eq_excl < need))
    self_ = jnp.where(sel, 1.0, 0.0)
    sel_excl, bases = seg_prefix(self_)
    seg_off = jnp.where(in_ctx, 0.0, float(cap_ctx))
    pos_ref[0] = jnp.where(sel, sel_excl + seg_off, -1.0).astype(jnp.int32)

    olane = lax.broadcasted_iota(jnp.int32, (ne, HEAD_LANES), 1)
    off = jnp.zeros((ne, HEAD_LANES), F32)
    for c in range(nch + 1):
        off = jnp.where(olane == c, bases[c] + (0.0 if c < ctx_ch else float(cap_ctx)), off)
    off_ref[0] = off.astype(jnp.int32)

    cnt = sel_excl + self_ + seg_off
    for e in range(ne):
        cnt_scr[e] = cnt[e:e + 1]
    n_slots = idx_ref.shape[1]
    sb = n_slots // IDX_SLOT_BLOCKS
    ilane = lax.broadcasted_iota(jnp.int32, (sb, HEAD_LANES), 1)
    for blk in range(IDX_SLOT_BLOCKS):
        slot = (lax.broadcasted_iota(jnp.int32, (sb, HEAD_LANES), 0) + blk * sb).astype(F32)

        def per_expert(e, out):
            def per_chunk(c, acc):
                cnt_row = cnt_scr[e, :, pl.ds(pl.multiple_of(c * HEAD_LANES, HEAD_LANES), HEAD_LANES)]
                return acc + jnp.where(cnt_row <= slot, 1.0, 0.0)
            acc = lax.fori_loop(0, l // HEAD_LANES, per_chunk, jnp.zeros((sb, HEAD_LANES), F32))
            return jnp.where(ilane == e, jnp.sum(acc, axis=1, keepdims=True), out)
        out = lax.fori_loop(0, ne, per_expert, jnp.zeros((sb, HEAD_LANES), F32))
        idx_ref[0, blk * sb:(blk + 1) * sb, :] = out.astype(jnp.int32)


def _topk(aff_t, *, m_ctx, cap_ctx, cap_lat):
    b, ne, l = aff_t.shape
    n_slots = cap_ctx + cap_lat
    kern = functools.partial(_topk_kernel, m_ctx=m_ctx, cap_ctx=cap_ctx, cap_lat=cap_lat)
    return pl.pallas_call(
        kern,
        grid=(b,),
        in_specs=[pl.BlockSpec((1, ne, l), lambda bb: (bb, 0, 0))],
        out_specs=[pl.BlockSpec((1, ne, l), lambda bb: (bb, 0, 0)),
                   pl.BlockSpec((1, ne, HEAD_LANES), lambda bb: (bb, 0, 0)),
                   pl.BlockSpec((1, n_slots, HEAD_LANES), lambda bb: (bb, 0, 0))],
        out_shape=[jax.ShapeDtypeStruct((b, ne, l), jnp.int32),
                   jax.ShapeDtypeStruct((b, ne, HEAD_LANES), jnp.int32),
                   jax.ShapeDtypeStruct((b, n_slots, HEAD_LANES), jnp.int32)],
        scratch_shapes=[pltpu.VMEM((ne, 1, l), F32)],
        compiler_params=_cparams(("arbitrary",)),
        name="expert_choice_topk",
    )(aff_t)


def _ffn_kernel(idx_ref, h_hbm, wg_ref, wu_ref, wd_ref, y_ref, xg_scr, xb_scr, acc_scr, sem, *, n_experts):
    e, b, f = pl.program_id(0), pl.program_id(1), pl.program_id(2)
    n_slots, d = xb_scr.shape

    @pl.when(f == 0)
    def _():
        base = (b * n_experts + e) * n_slots

        def issue(r, carry):
            pltpu.make_async_copy(h_hbm.at[b, pl.ds(idx_ref[base + r], 1)], xg_scr.at[pl.ds(r, 1)], sem.at[0]).start()
            return carry
        lax.fori_loop(0, n_slots, issue, 0, unroll=8)
        pltpu.make_async_copy(h_hbm.at[b, pl.ds(0, n_slots)], xg_scr, sem.at[0]).wait()
        xb_scr[...] = xg_scr[:, :d].astype(BF16)
        acc_scr[...] = jnp.zeros(acc_scr.shape, F32)

    xb = xb_scr[...]
    a = jnp.dot(xb, wg_ref[0, 0].astype(BF16), preferred_element_type=F32)
    u = jnp.dot(xb, wu_ref[0, 0].astype(BF16), preferred_element_type=F32)
    hid = (_silu(a) * u).astype(BF16)
    acc_scr[...] += jnp.dot(hid, wd_ref[0, 0].astype(BF16), preferred_element_type=F32)

    @pl.when(f == pl.num_programs(2) - 1)
    def _():
        tail = xg_scr[:, d:]
        lane = lax.broadcasted_iota(jnp.int32, tail.shape, 1)
        gate = jnp.sum(jnp.where(lane == e, tail, 0.0), axis=1, keepdims=True)
        y_ref[0, 0] = (acc_scr[...] * gate).astype(y_ref.dtype)


def _expert_ffn(idx_flat, h2, w_gate, w_up, w_down, *, n_slots, layer):
    b, l, dx = h2.shape
    _, ne, d, ff = w_gate.shape
    tf = 256
    grid_spec = pltpu.PrefetchScalarGridSpec(
        num_scalar_prefetch=1,
        grid=(ne, b, ff // tf),
        in_specs=[pl.BlockSpec(memory_space=pl.ANY),
                  pl.BlockSpec((1, 1, d, tf), lambda e, bb, f, idx: (layer, e, 0, f)),
                  pl.BlockSpec((1, 1, d, tf), lambda e, bb, f, idx: (layer, e, 0, f)),
                  pl.BlockSpec((1, 1, tf, d), lambda e, bb, f, idx: (layer, e, f, 0))],
        out_specs=pl.BlockSpec((1, 1, n_slots, d), lambda e, bb, f, idx: (bb, e, 0, 0)),
        scratch_shapes=[pltpu.VMEM((n_slots, dx), F32), pltpu.VMEM((n_slots, d), BF16),
                        pltpu.VMEM((n_slots, d), F32), pltpu.SemaphoreType.DMA((1,))])
    return pl.pallas_call(
        functools.partial(_ffn_kernel, n_experts=ne),
        grid_spec=grid_spec,
        out_shape=jax.ShapeDtypeStruct((b, ne, n_slots, d), BF16),
        compiler_params=_cparams(("arbitrary", "arbitrary", "arbitrary")),
        name="expert_ffn",
    )(idx_flat, h2, w_gate, w_up, w_down)


def _combine_kernel(off_ref, x1_ref, pos_ref, mod_ref, lnw_ref, lnb_ref, y_hbm, o_ref, win_scr, sem, *,
                    alpha, n_experts, win):
    b, i = pl.program_id(0), pl.program_id(1)
    tm = x1_ref.shape[1]
    n_slots = y_hbm.shape[2]
    blk = win_scr.shape[0] // n_experts

    @pl.when(jnp.logical_and(b == 0, i == 0))
    def _():
        win_scr[...] = jnp.zeros(win_scr.shape, win_scr.dtype)

    copies = []
    for e in range(n_experts):
        p0 = off_ref[(b * n_experts + e) * HEAD_LANES + i]
        start = jnp.minimum((p0 // WIN_ALIGN) * WIN_ALIGN, n_slots - win)
        start = pl.multiple_of(start, WIN_ALIGN)
        cp = pltpu.make_async_copy(y_hbm.at[b, e, pl.ds(start, win)], win_scr.at[pl.ds(e * blk, win)], sem.at[e])
        cp.start()
        copies.append((start, cp))

    pos = pos_ref[0]
    scol = lax.broadcasted_iota(jnp.int32, (tm, blk), 1)
    onehot = jnp.concatenate([jnp.where((pos[:, e:e + 1] - copies[e][0]) == scol, 1.0, 0.0).astype(BF16)
                              for e in range(n_experts)], axis=1)
    for _, cp in copies:
        cp.wait()
    moe = jnp.dot(onehot, win_scr[...], preferred_element_type=F32)
    mod = mod_ref[0, 0]
    o_ref[0] = _layer_norm(alpha * x1_ref[0] + mod[5:6] * moe, lnw_ref[...], lnb_ref[...])


def _combine(off_flat, x1, pos_tok, modtab, ln_w, ln_b, y, *, m_ctx, alpha):
    b, l, d = x1.shape
    ne = pos_tok.shape[2]
    tm = COMBINE_TILE
    win = tm + WIN_ALIGN
    blk = -(-win // HEAD_LANES) * HEAD_LANES
    ctx_tiles = m_ctx // tm

    def rows(width):
        return pl.BlockSpec((1, tm, width), lambda bb, i, off: (bb, i, 0))

    grid_spec = pltpu.PrefetchScalarGridSpec(
        num_scalar_prefetch=1,
        grid=(b, l // tm),
        in_specs=[rows(d), rows(ne),
                  pl.BlockSpec((1, 1, 6, d), lambda bb, i, off: (bb, jnp.minimum(i // ctx_tiles, 1), 0, 0)),
                  pl.BlockSpec((1, d), lambda bb, i, off: (0, 0)),
                  pl.BlockSpec((1, d), lambda bb, i, off: (0, 0)),
                  pl.BlockSpec(memory_space=pl.ANY)],
        out_specs=rows(d),
        scratch_shapes=[pltpu.VMEM((ne * blk, d), BF16), pltpu.SemaphoreType.DMA((ne,))])
    return pl.pallas_call(
        functools.partial(_combine_kernel, alpha=alpha, n_experts=ne, win=win),
        grid_spec=grid_spec,
        out_shape=jax.ShapeDtypeStruct((b, l, d), F32),
        compiler_params=_cparams(("arbitrary", "arbitrary")),
        name="moe_combine_ln",
    )(off_flat, x1, pos_tok, modtab, ln_w.reshape(1, d), ln_b.reshape(1, d), y)


def kernel(x, c, ctx, c_ctx, w_mod, b_mod, w_in, w_conv, lambda_qk, subln_w, hgrn_lb_logits, hgrn_norm_w, w_out,
           ln_w, ln_b, w_router, w_gate, w_up, w_down):
    bsz, n_lat, d = x.shape
    m_ctx = ctx.shape[1]
    depth = w_mod.shape[0]
    ne = w_router.shape[2]
    aw, cw, hw = d // 2, d // 4, d // 4
    assert m_ctx % TILE == 0 and n_lat % 1024 == 0 and n_lat % GRID_W == 0 and bsz + 1 <= 8
    assert (m_ctx + n_lat) % (8 * 16) == 0 and aw == 1024 and ne <= HEAD_LANES
    alpha = (2.0 * depth) ** 0.25
    cap_ctx = EC_CAPACITY_FACTOR * m_ctx // ne
    cap_lat = EC_CAPACITY_FACTOR * n_lat // ne
    n_slots = cap_ctx + cap_lat
    assert n_slots % WIN_ALIGN == 0 and n_slots >= COMBINE_TILE + WIN_ALIGN
    assert (m_ctx + n_lat) // COMBINE_TILE + 1 <= HEAD_LANES

    cs = jnp.zeros((8, d), F32).at[:bsz].set(c).at[bsz].set(c_ctx)
    mod = _modulation(cs, w_mod, b_mod)
    mod_lat = mod[:, :bsz].reshape(depth, bsz, 1, 6, d)
    mod_ctx = jnp.broadcast_to(mod[:, bsz].reshape(depth, 1, 1, 6, d), (depth, bsz, 1, 6, d))
    modtab = jnp.concatenate([mod_ctx, mod_lat], axis=2)

    tables = _rope_tables(m_ctx, n_lat)
    xa = jnp.concatenate([ctx, x], axis=1)
    w_router_pad = jnp.zeros((depth, d, HEAD_LANES), F32).at[:, :, :ne].set(w_router)
    hg_heads = hw // HEAD_LANES
    cb0 = 3 * cw // HEAD_LANES

    for l in range(depth):
        lambda_init = 0.8 - 0.6 * math.exp(-0.3 * l)
        w_in_bf = w_in[l].astype(BF16)
        qkv = _inproj(xa, modtab[l], w_in_bf, tables, col0=0, ncols=3 * aw, out_dtype=BF16, rope_tiles=2,
                      m_ctx=m_ctx)
        cg = _inproj(xa, modtab[l], w_in_bf, tables, col0=3 * aw, ncols=3 * cw + 5 * hw, out_dtype=F32,
                     rope_tiles=0, m_ctx=m_ctx)
        att = _attention(qkv, lambda_qk[l], subln_w[l], m_ctx=m_ctx, lambda_init=lambda_init)
        o_f, o_b = _hgrn(cg, hgrn_lb_logits, layer=l, hg_heads=hg_heads, col_q=cb0, col_i=cb0 + hg_heads,
                         col_ff=cb0 + 3 * hg_heads, col_fb=cb0 + 4 * hg_heads)
        x1, h2, aff = _outproj(xa, att, cg, o_f, o_b, w_conv[l], hgrn_norm_w[l], w_out[l].astype(BF16), modtab[l],
                               ln_w[l, 0], ln_b[l, 0], w_router_pad[l], m_ctx=m_ctx, alpha=alpha, n_experts=ne,
                               cw=cw, col_gg=(3 * cw + 2 * hw) // hw)
        pos, off, idx = _topk(jnp.swapaxes(aff, 1, 2), m_ctx=m_ctx, cap_ctx=cap_ctx, cap_lat=cap_lat)
        idx_flat = jnp.swapaxes(idx[:, :, :ne], 1, 2).reshape(-1)
        y = _expert_ffn(idx_flat, h2, w_gate, w_up, w_down, n_slots=n_slots, layer=l)
        xa = _combine(off.reshape(-1), x1, jnp.swapaxes(pos, 1, 2), modtab[l], ln_w[l, 1], ln_b[l, 1], y,
                      m_ctx=m_ctx, alpha=alpha)
    return xa[:, m_ctx:]
```

```python
import functools
import math

import jax
import jax.numpy as jnp
from jax import lax
from jax.experimental import pallas as pl
from jax.experimental.pallas import tpu as pltpu

F32 = jnp.float32
BF16 = jnp.bfloat16
HIGHEST = lax.Precision.HIGHEST

GRID_W = 64
ROPE_THETA = 10000.0
ATT_HEAD_DIM = 64
HEAD_LANES = 128
HGRN_CHUNK = 64
EC_CAPACITY_FACTOR = 2
EPS = 1e-6
TILE = 256
INPROJ_SUB = 256
ATTN_UNROLL = 4
COMBINE_TILE = 128
IDX_SLOT_BLOCKS = 4
WIN_SHORT = 48
WIN_ALIGN = 16
VMEM_LIMIT = 56 * 1024 * 1024


def _cparams(sem):
    return pltpu.CompilerParams(dimension_semantics=sem, vmem_limit_bytes=VMEM_LIMIT)


def _silu(x):
    return x * jax.nn.sigmoid(x)


def _mod_kernel(cs_ref, w_ref, b_ref, o_ref):
    a = _silu(cs_ref[...])
    o_ref[0] = jnp.dot(a, w_ref[0], precision=HIGHEST, preferred_element_type=F32) + b_ref[0]


def _modulation(cs, w_mod, b_mod):
    depth, d, n6 = w_mod.shape
    tn = 1024
    return pl.pallas_call(
        _mod_kernel,
        grid=(depth, n6 // tn),
        in_specs=[pl.BlockSpec((8, d), lambda l, j: (0, 0)),
                  pl.BlockSpec((1, d, tn), lambda l, j: (l, 0, j)),
                  pl.BlockSpec((1, 1, tn), lambda l, j: (l, 0, j))],
        out_specs=pl.BlockSpec((1, 8, tn), lambda l, j: (l, 0, j)),
        out_shape=jax.ShapeDtypeStruct((depth, 8, n6), F32),
        compiler_params=_cparams(("arbitrary", "arbitrary")),
        name="modulation",
    )(cs, w_mod, b_mod.reshape(depth, 1, n6))


def _inproj_kernel(x_ref, mod_ref, w_ref, rc_ref, ra_ref, rb_ref, o_ref, h_scr, *, m_ctx, rope_tiles):
    i = pl.program_id(1)
    j = pl.program_id(2)
    tm = x_ref.shape[1]

    @pl.when(j == 0)
    def _():
        row = i * tm + lax.broadcasted_iota(jnp.int32, (tm, 1), 0)
        is_ctx = row < m_ctx
        mod = mod_ref[0]
        sh = jnp.where(is_ctx, mod[0, 0:1], mod[1, 0:1])
        sc = jnp.where(is_ctx, mod[0, 1:2], mod[1, 1:2])
        h_scr[...] = (x_ref[0] * (1.0 + sc) + sh).astype(BF16)

    h = h_scr[...]
    if rope_tiles:
        rc, ra, rb = rc_ref[0], ra_ref[0], rb_ref[0]
        qscale = jnp.where(j == 0, ATT_HEAD_DIM ** -0.5 * math.log2(math.e), 1.0).astype(F32)

    def finish(c, acc):
        for k in range(INPROJ_SUB // HEAD_LANES):
            blk = acc[:, k * HEAD_LANES:(k + 1) * HEAD_LANES]
            if rope_tiles:
                blk = (blk * rc + pltpu.roll(blk, 16, 1) * ra + pltpu.roll(blk, HEAD_LANES - 16, 1) * rb) * qscale
            lo = c * INPROJ_SUB + k * HEAD_LANES
            o_ref[0, :, lo:lo + HEAD_LANES] = blk.astype(o_ref.dtype)

    pending = None
    for c in range(w_ref.shape[1] // INPROJ_SUB):
        acc = jnp.dot(h, w_ref[:, c * INPROJ_SUB:(c + 1) * INPROJ_SUB], preferred_element_type=F32)
        if pending is not None:
            finish(*pending)
        pending = (c, acc)
    finish(*pending)


def _inproj(xa, modtab, w_bf, tables, *, col0, ncols, out_dtype, rope_tiles, m_ctx):
    b, l, d = xa.shape
    tn = 1024
    tm = l // 8
    joff = col0 // tn
    kern = functools.partial(_inproj_kernel, m_ctx=m_ctx, rope_tiles=rope_tiles)
    tab_spec = pl.BlockSpec((1, tm, HEAD_LANES), lambda bb, i, j: (jnp.where(j < rope_tiles, 0, 1), i, 0))
    return pl.pallas_call(
        kern,
        grid=(b, l // tm, ncols // tn),
        in_specs=[pl.BlockSpec((1, tm, d), lambda bb, i, j: (bb, i, 0)),
                  pl.BlockSpec((1, 2, 6, d), lambda bb, i, j: (bb, 0, 0, 0)),
                  pl.BlockSpec((d, tn), lambda bb, i, j: (0, j + joff)),
                  tab_spec, tab_spec, tab_spec],
        out_specs=pl.BlockSpec((1, tm, tn), lambda bb, i, j: (bb, i, j)),
        out_shape=jax.ShapeDtypeStruct((b, l, ncols), out_dtype),
        scratch_shapes=[pltpu.VMEM((tm, d), BF16)],
        compiler_params=_cparams(("arbitrary", "arbitrary", "arbitrary")),
        name="inproj",
    )(xa, modtab, w_bf, *tables)


def _rope_tables(m_ctx, n_lat):
    nf = ATT_HEAD_DIM // 4
    inv = ROPE_THETA ** (-jnp.arange(nf, dtype=F32) / nf)
    rows = n_lat // GRID_W
    row = jnp.repeat(jnp.arange(rows, dtype=F32), GRID_W)
    col = jnp.tile(jnp.arange(GRID_W, dtype=F32), rows)
    ar, ac = row[:, None] * inv, col[:, None] * inv
    cr, sr, cc, sc = jnp.cos(ar), jnp.sin(ar), jnp.cos(ac), jnp.sin(ac)
    z = jnp.zeros_like(sr)
    c64 = jnp.concatenate([cr, cr, cc, cc], axis=1)
    a64 = jnp.concatenate([z, sr, z, sc], axis=1)
    b64 = jnp.concatenate([-sr, z, -sc, z], axis=1)
    reps = HEAD_LANES // ATT_HEAD_DIM

    def full(t64, fill):
        lat = jnp.tile(t64, (1, reps))
        rot = jnp.concatenate([jnp.full((m_ctx, HEAD_LANES), fill, F32), lat], axis=0)
        return jnp.stack([rot, jnp.full(rot.shape, fill, F32)])

    return full(c64, 1.0), full(a64, 0.0), full(b64, 0.0)


def _attn_kernel(lq_ref, sw_ref, q_ref, k_ref, v_ref, o_ref, vt_scr, m0_scr, l0_scr, acc0_scr, m1_scr, l1_scr,
                 acc1_scr, sa0_scr, sa1_scr, sb0_scr, sb1_scr, *, m_ctx, tk, lambda_init):
    i = pl.program_id(2)
    tq = q_ref.shape[1]
    n_lat = k_ref.shape[1] - m_ctx

    @pl.when(i == 0)
    def _():
        vt_scr[...] = v_ref[0].T

    q = q_ref[0]
    lane = lax.broadcasted_iota(jnp.int32, q.shape, 1)
    zero = jnp.zeros_like(q)
    qs = (jnp.where(lane < ATT_HEAD_DIM, q, zero), jnp.where(lane >= ATT_HEAD_DIM, q, zero))
    stats = ((m0_scr, l0_scr, acc0_scr), (m1_scr, l1_scr, acc1_scr))
    for m_scr, l_scr, acc_scr in stats:
        m_scr[...] = jnp.full(m_scr.shape, -jnp.inf, F32)
        l_scr[...] = jnp.zeros(l_scr.shape, F32)
        acc_scr[...] = jnp.zeros(acc_scr.shape, F32)

    def scores(start, size, st_refs):
        kc = k_ref[0, pl.ds(start, size), :]
        cmax = []
        for s in range(2):
            st = lax.dot_general(kc, qs[s], (((1,), (1,)), ((), ())), preferred_element_type=F32)
            st_refs[s][0:size, :] = st
            cmax.append(jnp.max(st, axis=0, keepdims=True))
        return tuple(cmax)

    def accumulate(start, size, st_refs, cmax):
        vt = vt_scr[:, pl.ds(start, size)]
        for s, (m_scr, l_scr, acc_scr) in enumerate(stats):
            m_old = m_scr[...]
            m_new = jnp.maximum(m_old, cmax[s])
            alpha = jnp.exp2(m_old - m_new)
            p = jnp.exp2(st_refs[s][0:size, :] - m_new)
            l_scr[...] = alpha * l_scr[...] + jnp.sum(p, axis=0, keepdims=True)
            acc_scr[...] = alpha * acc_scr[...] + jnp.dot(vt, p.astype(BF16), preferred_element_type=F32)
            m_scr[...] = m_new

    st_a, st_b = (sa0_scr, sa1_scr), (sb0_scr, sb1_scr)

    @pl.when(i * tq < m_ctx)
    def _():
        accumulate(0, m_ctx, st_a, scores(0, m_ctx, st_a))

    @pl.when(i * tq >= m_ctx)
    def _():
        def at(c):
            return pl.multiple_of(m_ctx + c * tk, math.gcd(m_ctx, tk))
        nck = n_lat // tk
        bufs = (st_b, st_a)

        def run(first, count, cm):
            for u in range(count):
                c = first + u
                last = isinstance(c, int) and c + 1 >= nck
                cm_next = None if last else scores(at(c + 1), tk, bufs[(u + 1) % 2])
                accumulate(at(c), tk, bufs[u % 2], cm)
                cm = cm_next
            return cm

        cm_ctx = scores(0, m_ctx, st_a)
        cm = scores(at(0), tk, st_b)
        accumulate(0, m_ctx, st_a, cm_ctx)
        trips = (nck - 1) // ATTN_UNROLL
        cm = lax.fori_loop(0, trips, lambda j, cm: run(j * ATTN_UNROLL, ATTN_UNROLL, cm), cm)
        run(trips * ATTN_UNROLL, nck - trips * ATTN_UNROLL, cm)

    lq = lq_ref[...]
    lam = (jnp.exp(jnp.sum(lq[0:1] * lq[1:2], axis=1, keepdims=True))
           - jnp.exp(jnp.sum(lq[2:3] * lq[3:4], axis=1, keepdims=True)) + lambda_init)
    ot = acc0_scr[...] / l0_scr[...] - lam * (acc1_scr[...] / l1_scr[...])
    ms = jnp.mean(ot * ot, axis=0, keepdims=True)
    o = (ot * lax.rsqrt(ms + EPS)).T * sw_ref[...] * (1.0 - lambda_init)
    o_ref[0] = o.astype(o_ref.dtype)


def _attention(qkv, lambda_qk, subln_w, *, m_ctx, lambda_init):
    b, l, w3 = qkv.shape
    heads = w3 // (3 * HEAD_LANES)
    tq = TILE
    tk = 512
    kern = functools.partial(_attn_kernel, m_ctx=m_ctx, tk=tk, lambda_init=lambda_init)
    return pl.pallas_call(
        kern,
        grid=(b, heads, l // tq),
        in_specs=[pl.BlockSpec((4, ATT_HEAD_DIM), lambda bb, h, i: (0, 0)),
                  pl.BlockSpec((1, HEAD_LANES), lambda bb, h, i: (0, 0)),
                  pl.BlockSpec((1, tq, HEAD_LANES), lambda bb, h, i: (bb, i, h)),
                  pl.BlockSpec((1, l, HEAD_LANES), lambda bb, h, i: (bb, 0, heads + h)),
                  pl.BlockSpec((1, l, HEAD_LANES), lambda bb, h, i: (bb, 0, 2 * heads + h))],
        out_specs=pl.BlockSpec((1, tq, HEAD_LANES), lambda bb, h, i: (bb, i, h)),
        out_shape=jax.ShapeDtypeStruct((b, l, heads * HEAD_LANES), BF16),
        scratch_shapes=[pltpu.VMEM((HEAD_LANES, l), BF16)]
        + [pltpu.VMEM((1, tq), F32), pltpu.VMEM((1, tq), F32), pltpu.VMEM((HEAD_LANES, tq), F32)] * 2
        + [pltpu.VMEM((max(tk, m_ctx), tq), F32)] * 4,
        compiler_params=_cparams(("arbitrary", "arbitrary", "arbitrary")),
        name="diff_attention",
    )(lambda_qk, subln_w.reshape(1, HEAD_LANES), qkv, qkv, qkv)


def _log_sigmoid(z):
    return jnp.minimum(z, 0.0) - jnp.log1p(jnp.exp(-jnp.abs(z)))


def _forget_gate(z, lb):
    ls = _log_sigmoid(z)
    key = jax.nn.sigmoid(-z)
    if lb is None:
        return ls, key
    a = jnp.log(lb)
    b = jnp.log1p(-lb) + ls
    logf = jnp.maximum(a, b) + jnp.log1p(jnp.exp(-jnp.abs(a - b)))
    return logf, (1.0 - lb) * key


def _hgrn_tile(dirs):
    ch = HGRN_CHUNK
    nt_dims = (((1,), (1,)), ((), ()))
    work = []
    for q_ref, v_ref, z_ref, o_ref, s_scr, lb, reverse in dirs:
        t = q_ref.shape[1]
        logf, key = _forget_gate(z_ref[0], lb)
        r = lax.broadcasted_iota(jnp.int32, (t, t), 0)
        c = lax.broadcasted_iota(jnp.int32, (t, t), 1)
        tri = ((r // ch) == (c // ch)) & ((c >= r) if reverse else (c <= r))
        a = jnp.dot(tri.astype(F32), logf, precision=HIGHEST, preferred_element_type=F32)
        work.append(dict(q=q_ref[0], v=v_ref[0].astype(BF16), key=key, a=a, tri=tri, o_ref=o_ref, s_scr=s_scr,
                         reverse=reverse, nch=t // ch))

    for w in work:
        a, nch = w["a"], w["nch"]

        def per_chunk(row_of):
            return jnp.concatenate([jnp.broadcast_to(row_of(a[k * ch:(k + 1) * ch]), (ch, a.shape[1]))
                                    for k in range(nch)], axis=0)
        a_mid = per_chunk(lambda ac: ac[ch // 2 - 1:ch // 2])
        a_end = per_chunk((lambda ac: ac[0:1]) if w["reverse"] else (lambda ac: ac[ch - 1:ch]))
        qe = (w["q"] * jnp.exp(a - a_mid)).astype(BF16)
        ke = (w["key"] * jnp.exp(a_mid - a)).astype(BF16)
        kd = (w["key"] * jnp.exp(a_end - a)).astype(BF16)
        w["qa"] = (w["q"] * jnp.exp(a)).astype(BF16)
        w["decay"] = jnp.exp(a_end)
        w["sc"] = lax.dot_general(qe, ke, nt_dims, preferred_element_type=F32)
        w["kv"] = [lax.dot_general(w["v"][k * ch:(k + 1) * ch], kd[k * ch:(k + 1) * ch], (((0,), (0,)), ((), ())),
                                   preferred_element_type=F32) for k in range(nch)]

    for w in work:
        w["intra"] = jnp.dot(jnp.where(w["tri"], w["sc"], 0.0).astype(BF16), w["v"], preferred_element_type=F32)

    for w in work:
        nch = w["nch"]
        st = w["s_scr"][...]
        states = [None] * nch
        for k in (range(nch - 1, -1, -1) if w["reverse"] else range(nch)):
            states[k] = st.astype(BF16)
            row = k * ch if w["reverse"] else (k + 1) * ch - 1
            st = w["decay"][row:row + 1] * st + w["kv"][k]
        w["s_scr"][...] = st
        w["states"] = states

    for w in work:
        inter = [lax.dot_general(w["qa"][k * ch:(k + 1) * ch], w["states"][k], nt_dims, preferred_element_type=F32)
                 for k in range(w["nch"])]
        w["o_ref"][0] = jnp.concatenate(inter, axis=0) + w["intra"]


def _hgrn_kernel(lbl_ref, qf_ref, vf_ref, zf_ref, qb_ref, vb_ref, zb_ref, of_ref, ob_ref, sf_scr, sb_scr, *, layer):
    @pl.when(pl.program_id(2) == 0)
    def _():
        sf_scr[...] = jnp.zeros(sf_scr.shape, F32)
        sb_scr[...] = jnp.zeros(sb_scr.shape, F32)

    if layer == 0:
        lbf = lbb = None
    else:
        lg = lbl_ref[...]
        ex = jnp.exp(lg - jnp.max(lg, axis=1, keepdims=True))
        sm = ex / jnp.sum(ex, axis=1, keepdims=True)
        lb = sm[:, 1]
        for k in range(2, layer + 1):
            lb = lb + sm[:, k]
        lbf, lbb = lb[0:1], lb[1:2]
    _hgrn_tile([(qf_ref, vf_ref, zf_ref, of_ref, sf_scr, lbf, False),
                (qb_ref, vb_ref, zb_ref, ob_ref, sb_scr, lbb, True)])


def _hgrn(cg, lb_logits, *, layer, hg_heads, col_q, col_i, col_ff, col_fb):
    b, l, _ = cg.shape
    t = TILE
    nb = l // t
    depth = lb_logits.shape[1]

    def fwd(col):
        return pl.BlockSpec((1, t, HEAD_LANES), lambda bb, h, i: (bb, i, col + h))

    def bwd(col):
        return pl.BlockSpec((1, t, HEAD_LANES), lambda bb, h, i: (bb, jnp.where(i == 0, 0, nb - i), col + h))

    out_f = pl.BlockSpec((1, t, HEAD_LANES), lambda bb, h, i: (bb, i, h))
    out_b = pl.BlockSpec((1, t, HEAD_LANES), lambda bb, h, i: (bb, jnp.where(i == 0, 0, nb - i), h))
    shp = jax.ShapeDtypeStruct((b, l, hg_heads * HEAD_LANES), F32)
    return pl.pallas_call(
        functools.partial(_hgrn_kernel, layer=layer),
        grid=(b, hg_heads, nb),
        in_specs=[pl.BlockSpec((2, depth, HEAD_LANES), lambda bb, h, i: (0, 0, h)),
                  fwd(col_q), fwd(col_i), fwd(col_ff), bwd(col_q), bwd(col_i), bwd(col_fb)],
        out_specs=[out_f, out_b],
        out_shape=[shp, shp],
        scratch_shapes=[pltpu.VMEM((HEAD_LANES, HEAD_LANES), F32), pltpu.VMEM((HEAD_LANES, HEAD_LANES), F32)],
        compiler_params=_cparams(("arbitrary", "arbitrary", "arbitrary")),
        name="hgrn2",
    )(lb_logits, cg, cg, cg, cg, cg, cg)


def _layer_norm(r, w, b):
    mu = jnp.mean(r, axis=1, keepdims=True)
    var = jnp.mean(jnp.square(r - mu), axis=1, keepdims=True)
    return (r - mu) * lax.rsqrt(var + EPS) * w + b


def _outproj_kernel(x_ref, att_ref, cx_ref, cb_ref, cc_ref, cxp_ref, ccp_ref, cxn_ref, ccn_ref, of_ref, ob_ref,
                    gg_ref, wconv_ref, hnw_ref, wout_ref, mod_ref, lnw_ref, lnb_ref, wr_ref,
                    x1_ref, h2_ref, aff_ref, *, ctx_tiles, alpha, n_experts):
    i = pl.program_id(1)
    nt = pl.num_programs(1)
    tm = x_ref.shape[1]
    aw = att_ref.shape[2]
    cw = cx_ref.shape[2]

    u = cc_ref[0] * cx_ref[0]
    prev_ok = jnp.logical_and(i != 0, i != ctx_tiles)
    next_ok = jnp.logical_and(i != ctx_tiles - 1, i != nt - 1)
    u_before = jnp.where(prev_ok, (ccp_ref[0] * cxp_ref[0])[7:8], 0.0)
    u_after = jnp.where(next_ok, (ccn_ref[0] * cxn_ref[0])[0:1], 0.0)
    row = lax.broadcasted_iota(jnp.int32, (tm, 1), 0)
    u_prev = jnp.where(row == 0, u_before, pltpu.roll(u, 1, 0))
    u_next = jnp.where(row == tm - 1, u_after, pltpu.roll(u, tm - 1, 0))
    wc = wconv_ref[...]
    conv = cb_ref[0] * (u_prev * wc[0:1] + u * wc[1:2] + u_next * wc[2:3])

    o = of_ref[0] + ob_ref[0]
    gg = gg_ref[0]
    recs = []
    for h in range(o.shape[1] // HEAD_LANES):
        oh = o[:, h * HEAD_LANES:(h + 1) * HEAD_LANES]
        ms = jnp.mean(oh * oh, axis=1, keepdims=True)
        recs.append(oh * lax.rsqrt(ms + EPS) * hnw_ref[...] * _silu(gg[:, h * HEAD_LANES:(h + 1) * HEAD_LANES]))
    rec = jnp.concatenate(recs, axis=1)

    y = jnp.dot(att_ref[0], wout_ref[0:aw, :], preferred_element_type=F32)
    y = y + jnp.dot(conv.astype(BF16), wout_ref[aw:aw + cw, :], preferred_element_type=F32)
    y = y + jnp.dot(rec.astype(BF16), wout_ref[aw + cw:, :], preferred_element_type=F32)

    mod = mod_ref[0, 0]
    x1 = _layer_norm(alpha * x_ref[0] + mod[2:3] * y, lnw_ref[...], lnb_ref[...])
    x1_ref[0] = x1
    h2 = x1 * (1.0 + mod[4:5]) + mod[3:4]
    d = h2.shape[1]
    h2_ref[0, :, :d] = h2
    wr = wr_ref[...]
    wr_hi = wr.astype(BF16)
    wr_lo = (wr - wr_hi.astype(F32)).astype(BF16)
    h2_hi = h2.astype(BF16)
    h2_lo = (h2 - h2_hi.astype(F32)).astype(BF16)
    logits = (jnp.dot(h2_hi, wr_hi, preferred_element_type=F32) + jnp.dot(h2_hi, wr_lo, preferred_element_type=F32)
              + jnp.dot(h2_lo, wr_hi, preferred_element_type=F32))
    lane = lax.broadcasted_iota(jnp.int32, logits.shape, 1)
    logits = jnp.where(lane < n_experts, logits, -jnp.inf)
    ex = jnp.exp(logits - jnp.max(logits, axis=1, keepdims=True))
    aff = ex / jnp.sum(ex, axis=1, keepdims=True)
    aff_ref[0] = aff[:, :n_experts]
    h2_ref[0, :, d:] = aff


def _outproj(xa, att, cg, o_f, o_b, w_conv, hgrn_norm_w, w_out_bf, modtab, ln_w, ln_b, w_router_pad, *,
             m_ctx, alpha, n_experts, cw, col_gg):
    b, l, d = xa.shape
    tm = TILE
    aw = att.shape[2]
    hw = o_f.shape[2]
    r8 = tm // 8
    last8 = l // 8 - 1
    ctx_tiles = m_ctx // tm

    def rows(width, col):
        return pl.BlockSpec((1, tm, width), lambda bb, i: (bb, i, col))

    def halo_prev(col):
        return pl.BlockSpec((1, 8, cw), lambda bb, i: (bb, jnp.maximum(i * r8 - 1, 0), col))

    def halo_next(col):
        return pl.BlockSpec((1, 8, cw), lambda bb, i: (bb, jnp.minimum((i + 1) * r8, last8), col))

    def const(shape):
        return pl.BlockSpec(shape, lambda bb, i: (0,) * len(shape))

    kern = functools.partial(_outproj_kernel, ctx_tiles=ctx_tiles, alpha=alpha, n_experts=n_experts)
    return pl.pallas_call(
        kern,
        grid=(b, l // tm),
        in_specs=[rows(d, 0), rows(aw, 0),
                  rows(cw, 0), rows(cw, 1), rows(cw, 2),
                  halo_prev(0), halo_prev(2), halo_next(0), halo_next(2),
                  rows(hw, 0), rows(hw, 0), rows(hw, col_gg),
                  const((3, cw)), const((1, HEAD_LANES)), const((d, d)),
                  pl.BlockSpec((1, 1, 6, d), lambda bb, i: (bb, jnp.minimum(i // ctx_tiles, 1), 0, 0)),
                  const((1, d)), const((1, d)), const((d, HEAD_LANES))],
        out_specs=[rows(d, 0), rows(d + HEAD_LANES, 0), rows(n_experts, 0)],
        out_shape=[jax.ShapeDtypeStruct((b, l, d), F32), jax.ShapeDtypeStruct((b, l, d + HEAD_LANES), F32),
                   jax.ShapeDtypeStruct((b, l, n_experts), F32)],
        compiler_params=_cparams(("arbitrary", "arbitrary")),
        name="outproj_ln_router",
    )(xa, att, cg, cg, cg, cg, cg, cg, cg, o_f, o_b, cg, w_conv, hgrn_norm_w.reshape(1, HEAD_LANES), w_out_bf,
      modtab, ln_w.reshape(1, d), ln_b.reshape(1, d), w_router_pad)


def _topk_kernel(aff_ref, pos_ref, off_ref, idx_ref, cnt_scr, *, m_ctx, cap_ctx, cap_lat):
    aff = aff_ref[0]
    ne, l = aff.shape
    ch = COMBINE_TILE
    nch = l // ch
    ctx_ch = m_ctx // ch
    bits = pltpu.bitcast(aff, jnp.int32)
    lane = lax.broadcasted_iota(jnp.int32, (ne, l), 1)
    in_ctx = lane < m_ctx

    def kth_largest(seg, k):
        def body(_, lohi):
            lo, hi = lohi
            mid = lo + lax.shift_right_logical(hi - lo, 1)
            cnt = jnp.sum(jnp.where(jnp.logical_and(seg, bits >= mid), 1.0, 0.0), axis=1, keepdims=True)
            ge = cnt >= k
            return jnp.where(ge, mid, lo), jnp.where(ge, hi, mid)
        lo0 = jnp.zeros((ne, 1), jnp.int32)
        hi0 = jnp.full((ne, 1), 0x7F800000, jnp.int32)
        return lax.fori_loop(0, 31, body, (lo0, hi0))[0]

    thr = jnp.where(in_ctx, kth_largest(in_ctx, cap_ctx), kth_largest(jnp.logical_not(in_ctx), cap_lat))
    gt = bits > thr
    eq = bits == thr
    gtf = jnp.where(gt, 1.0, 0.0)
    n_gt_ctx = jnp.sum(jnp.where(in_ctx, gtf, 0.0), axis=1, keepdims=True)
    n_gt_lat = jnp.sum(jnp.where(in_ctx, 0.0, gtf), axis=1, keepdims=True)
    need = jnp.where(in_ctx, cap_ctx - n_gt_ctx, cap_lat - n_gt_lat)

    tr = lax.broadcasted_iota(jnp.int32, (ch, ch), 0)
    tc = lax.broadcasted_iota(jnp.int32, (ch, ch), 1)
    tri = jnp.where(tr <= tc, 1.0, 0.0).astype(BF16)

    def seg_prefix(flag):
        excl, bases = [], []
        base = jnp.zeros((ne, 1), F32)
        for c in range(nch):
            if c == ctx_ch:
                base = jnp.zeros((ne, 1), F32)
            fc = flag[:, c * ch:(c + 1) * ch]
            incl = jnp.dot(fc.astype(BF16), tri, preferred_element_type=F32)
            excl.append(base + incl - fc)
            bases.append(base)
            base = base + incl[:, ch - 1:ch]
        bases.append(base)
        return jnp.concatenate(excl, axis=1), bases

    eq_excl, _ = seg_prefix(jnp.where(eq, 1.0, 0.0))
    sel = jnp.logical_or(gt, jnp.logical_and(eq, eq_excl < need))
    self_ = jnp.where(sel, 1.0, 0.0)
    sel_excl, bases = seg_prefix(self_)
    seg_off = jnp.where(in_ctx, 0.0, float(cap_ctx))
    pos_ref[0] = jnp.where(sel, sel_excl + seg_off, -1.0).astype(jnp.int32)

    olane = lax.broadcasted_iota(jnp.int32, (ne, HEAD_LANES), 1)
    off = jnp.zeros((ne, HEAD_LANES), F32)
    for c in range(nch + 1):
        off = jnp.where(olane == c, bases[c] + (0.0 if c < ctx_ch else float(cap_ctx)), off)
    off_ref[0] = off.astype(jnp.int32)

    cnt = sel_excl + self_ + seg_off
    for e in range(ne):
        cnt_scr[e] = cnt[e:e + 1]
    n_slots = idx_ref.shape[1]
    sb = n_slots // IDX_SLOT_BLOCKS
    ilane = lax.broadcasted_iota(jnp.int32, (sb, HEAD_LANES), 1)
    for blk in range(IDX_SLOT_BLOCKS):
        slot = (lax.broadcasted_iota(jnp.int32, (sb, HEAD_LANES), 0) + blk * sb).astype(F32)

        def per_expert(e, out):
            def per_chunk(c, acc):
                cnt_row = cnt_scr[e, :, pl.ds(pl.multiple_of(c * HEAD_LANES, HEAD_LANES), HEAD_LANES)]
                return acc + jnp.where(cnt_row <= slot, 1.0, 0.0)
            acc = lax.fori_loop(0, l // HEAD_LANES, per_chunk, jnp.zeros((sb, HEAD_LANES), F32))
            return jnp.where(ilane == e, jnp.sum(acc, axis=1, keepdims=True), out)
        out = lax.fori_loop(0, ne, per_expert, jnp.zeros((sb, HEAD_LANES), F32))
        idx_ref[0, blk * sb:(blk + 1) * sb, :] = out.astype(jnp.int32)


def _topk(aff_t, *, m_ctx, cap_ctx, cap_lat):
    b, ne, l = aff_t.shape
    n_slots = cap_ctx + cap_lat
    kern = functools.partial(_topk_kernel, m_ctx=m_ctx, cap_ctx=cap_ctx, cap_lat=cap_lat)
    return pl.pallas_call(
        kern,
        grid=(b,),
        in_specs=[pl.BlockSpec((1, ne, l), lambda bb: (bb, 0, 0))],
        out_specs=[pl.BlockSpec((1, ne, l), lambda bb: (bb, 0, 0)),
                   pl.BlockSpec((1, ne, HEAD_LANES), lambda bb: (bb, 0, 0)),
                   pl.BlockSpec((1, n_slots, HEAD_LANES), lambda bb: (bb, 0, 0))],
        out_shape=[jax.ShapeDtypeStruct((b, ne, l), jnp.int32),
                   jax.ShapeDtypeStruct((b, ne, HEAD_LANES), jnp.int32),
                   jax.ShapeDtypeStruct((b, n_slots, HEAD_LANES), jnp.int32)],
        scratch_shapes=[pltpu.VMEM((ne, 1, l), F32)],
        compiler_params=_cparams(("arbitrary",)),
        name="expert_choice_topk",
    )(aff_t)


def _ffn_kernel(idx_ref, h_hbm, wg_ref, wu_ref, wd_ref, y_ref, xg_scr, xb_scr, acc_scr, sem, *, n_experts):
    e, b, f = pl.program_id(0), pl.program_id(1), pl.program_id(2)
    n_slots, d = xb_scr.shape

    @pl.when(f == 0)
    def _():
        base = (b * n_experts + e) * n_slots

        def issue(r, carry):
            pltpu.make_async_copy(h_hbm.at[b, pl.ds(idx_ref[base + r], 1)], xg_scr.at[pl.ds(r, 1)], sem.at[0]).start()
            return carry
        lax.fori_loop(0, n_slots, issue, 0, unroll=8)
        pltpu.make_async_copy(h_hbm.at[b, pl.ds(0, n_slots)], xg_scr, sem.at[0]).wait()
        xb_scr[...] = xg_scr[:, :d].astype(BF16)
        acc_scr[...] = jnp.zeros(acc_scr.shape, F32)

    xb = xb_scr[...]
    a = jnp.dot(xb, wg_ref[0, 0].astype(BF16), preferred_element_type=F32)
    u = jnp.dot(xb, wu_ref[0, 0].astype(BF16), preferred_element_type=F32)
    hid = (_silu(a) * u).astype(BF16)
    acc_scr[...] += jnp.dot(hid, wd_ref[0, 0].astype(BF16), preferred_element_type=F32)

    @pl.when(f == pl.num_programs(2) - 1)
    def _():
        tail = xg_scr[:, d:]
        lane = lax.broadcasted_iota(jnp.int32, tail.shape, 1)
        gate = jnp.sum(jnp.where(lane == e, tail, 0.0), axis=1, keepdims=True)
        y_ref[0, 0] = (acc_scr[...] * gate).astype(y_ref.dtype)


def _expert_ffn(idx_flat, h2, w_gate, w_up, w_down, *, n_slots, layer):
    b, l, dx = h2.shape
    _, ne, d, ff = w_gate.shape
    tf = 256
    grid_spec = pltpu.PrefetchScalarGridSpec(
        num_scalar_prefetch=1,
        grid=(ne, b, ff // tf),
        in_specs=[pl.BlockSpec(memory_space=pl.ANY),
                  pl.BlockSpec((1, 1, d, tf), lambda e, bb, f, idx: (layer, e, 0, f)),
                  pl.BlockSpec((1, 1, d, tf), lambda e, bb, f, idx: (layer, e, 0, f)),
                  pl.BlockSpec((1, 1, tf, d), lambda e, bb, f, idx: (layer, e, f, 0))],
        out_specs=pl.BlockSpec((1, 1, n_slots, d), lambda e, bb, f, idx: (bb, e, 0, 0)),
        scratch_shapes=[pltpu.VMEM((n_slots, dx), F32), pltpu.VMEM((n_slots, d), BF16),
                        pltpu.VMEM((n_slots, d), F32), pltpu.SemaphoreType.DMA((1,))])
    return pl.pallas_call(
        functools.partial(_ffn_kernel, n_experts=ne),
        grid_spec=grid_spec,
        out_shape=jax.ShapeDtypeStruct((b, ne, n_slots, d), BF16),
        compiler_params=_cparams(("arbitrary", "arbitrary", "arbitrary")),
        name="expert_ffn",
    )(idx_flat, h2, w_gate, w_up, w_down)


def _combine_kernel(off_ref, x1_ref, pos_ref, mod_ref, lnw_ref, lnb_ref, y_hbm, o_ref, win_scr, sem, *,
                    alpha, n_experts, win):
    b, i = pl.program_id(0), pl.program_id(1)
    nb, nt = pl.num_programs(0), pl.num_programs(1)
    tm = x1_ref.shape[1]
    n_slots = y_hbm.shape[2]
    blk = win_scr.shape[1] // n_experts
    step = b * nt + i
    slot = step % 2

    def window(bb, ii, e):
        base = (bb * n_experts + e) * HEAD_LANES + ii
        start = jnp.minimum((off_ref[base] // WIN_ALIGN) * WIN_ALIGN, n_slots - win)
        return pl.multiple_of(start, WIN_ALIGN), off_ref[base + 1] - start <= WIN_SHORT

    def transfer(bb, ii, sl, go):
        for e in range(n_experts):
            start, short = window(bb, ii, e)
            for size, cond in ((WIN_SHORT, short), (win, jnp.logical_not(short))):
                @pl.when(cond)
                def _():
                    go(pltpu.make_async_copy(y_hbm.at[bb, e, pl.ds(start, size)],
                                             win_scr.at[sl, pl.ds(e * blk, size)], sem.at[sl, e]))

    @pl.when(step == 0)
    def _():
        win_scr[...] = jnp.zeros(win_scr.shape, win_scr.dtype)
        transfer(b, i, slot, lambda cp: cp.start())

    @pl.when(step + 1 < nb * nt)
    def _():
        wrap = i + 1 == nt
        transfer(jnp.where(wrap, b + 1, b), jnp.where(wrap, 0, i + 1), 1 - slot, lambda cp: cp.start())

    pos = pos_ref[0]
    scol = lax.broadcasted_iota(jnp.int32, (tm, blk), 1)
    onehot = jnp.concatenate([jnp.where((pos[:, e:e + 1] - window(b, i, e)[0]) == scol, 1.0, 0.0).astype(BF16)
                              for e in range(n_experts)], axis=1)
    transfer(b, i, slot, lambda cp: cp.wait())
    moe = jnp.dot(onehot, win_scr[slot], preferred_element_type=F32)
    mod = mod_ref[0, 0]
    o_ref[0] = _layer_norm(alpha * x1_ref[0] + mod[5:6] * moe, lnw_ref[...], lnb_ref[...])


def _combine(off_flat, x1, pos_tok, modtab, ln_w, ln_b, y, *, m_ctx, alpha):
    b, l, d = x1.shape
    ne = pos_tok.shape[2]
    tm = COMBINE_TILE
    win = tm + WIN_ALIGN
    blk = -(-win // HEAD_LANES) * HEAD_LANES
    ctx_tiles = m_ctx // tm

    def rows(width):
        return pl.BlockSpec((1, tm, width), lambda bb, i, off: (bb, i, 0))

    grid_spec = pltpu.PrefetchScalarGridSpec(
        num_scalar_prefetch=1,
        grid=(b, l // tm),
        in_specs=[rows(d), rows(ne),
                  pl.BlockSpec((1, 1, 6, d), lambda bb, i, off: (bb, jnp.minimum(i // ctx_tiles, 1), 0, 0)),
                  pl.BlockSpec((1, d), lambda bb, i, off: (0, 0)),
                  pl.BlockSpec((1, d), lambda bb, i, off: (0, 0)),
                  pl.BlockSpec(memory_space=pl.ANY)],
        out_specs=rows(d),
        scratch_shapes=[pltpu.VMEM((2, ne * blk, d), BF16), pltpu.SemaphoreType.DMA((2, ne))])
    return pl.pallas_call(
        functools.partial(_combine_kernel, alpha=alpha, n_experts=ne, win=win),
        grid_spec=grid_spec,
        out_shape=jax.ShapeDtypeStruct((b, l, d), F32),
        compiler_params=_cparams(("arbitrary", "arbitrary")),
        name="moe_combine_ln",
    )(off_flat, x1, pos_tok, modtab, ln_w.reshape(1, d), ln_b.reshape(1, d), y)


def kernel(x, c, ctx, c_ctx, w_mod, b_mod, w_in, w_conv, lambda_qk, subln_w, hgrn_lb_logits, hgrn_norm_w, w_out,
           ln_w, ln_b, w_router, w_gate, w_up, w_down):
    bsz, n_lat, d = x.shape
    m_ctx = ctx.shape[1]
    depth = w_mod.shape[0]
    ne = w_router.shape[2]
    aw, cw, hw = d // 2, d // 4, d // 4
    assert m_ctx % TILE == 0 and n_lat % 1024 == 0 and n_lat % GRID_W == 0 and bsz + 1 <= 8
    assert (m_ctx + n_lat) % (8 * 16) == 0 and aw == 1024 and ne <= HEAD_LANES
    alpha = (2.0 * depth) ** 0.25
    cap_ctx = EC_CAPACITY_FACTOR * m_ctx // ne
    cap_lat = EC_CAPACITY_FACTOR * n_lat // ne
    n_slots = cap_ctx + cap_lat
    assert n_slots % WIN_ALIGN == 0 and n_slots >= COMBINE_TILE + WIN_ALIGN
    assert (m_ctx + n_lat) // COMBINE_TILE + 1 <= HEAD_LANES

    cs = jnp.zeros((8, d), F32).at[:bsz].set(c).at[bsz].set(c_ctx)
    mod = _modulation(cs, w_mod, b_mod)
    mod_lat = mod[:, :bsz].reshape(depth, bsz, 1, 6, d)
    mod_ctx = jnp.broadcast_to(mod[:, bsz].reshape(depth, 1, 1, 6, d), (depth, bsz, 1, 6, d))
    modtab = jnp.concatenate([mod_ctx, mod_lat], axis=2)

    tables = _rope_tables(m_ctx, n_lat)
    xa = jnp.concatenate([ctx, x], axis=1)
    w_router_pad = jnp.zeros((depth, d, HEAD_LANES), F32).at[:, :, :ne].set(w_router)
    hg_heads = hw // HEAD_LANES
    cb0 = 3 * cw // HEAD_LANES

    for l in range(depth):
        lambda_init = 0.8 - 0.6 * math.exp(-0.3 * l)
        w_in_bf = w_in[l].astype(BF16)
        qkv = _inproj(xa, modtab[l], w_in_bf, tables, col0=0, ncols=3 * aw, out_dtype=BF16, rope_tiles=2,
                      m_ctx=m_ctx)
        cg = _inproj(xa, modtab[l], w_in_bf, tables, col0=3 * aw, ncols=3 * cw + 5 * hw, out_dtype=F32,
                     rope_tiles=0, m_ctx=m_ctx)
        att = _attention(qkv, lambda_qk[l], subln_w[l], m_ctx=m_ctx, lambda_init=lambda_init)
        o_f, o_b = _hgrn(cg, hgrn_lb_logits, layer=l, hg_heads=hg_heads, col_q=cb0, col_i=cb0 + hg_heads,
                         col_ff=cb0 + 3 * hg_heads, col_fb=cb0 + 4 * hg_heads)
        x1, h2, aff = _outproj(xa, att, cg, o_f, o_b, w_conv[l], hgrn_norm_w[l], w_out[l].astype(BF16), modtab[l],
                               ln_w[l, 0], ln_b[l, 0], w_router_pad[l], m_ctx=m_ctx, alpha=alpha, n_experts=ne,
                               cw=cw, col_gg=(3 * cw + 2 * hw) // hw)
        pos, off, idx = _topk(jnp.swapaxes(aff, 1, 2), m_ctx=m_ctx, cap_ctx=cap_ctx, cap_lat=cap_lat)
        idx_flat = jnp.swapaxes(idx[:, :, :ne], 1, 2).reshape(-1)
        y = _expert_ffn(idx_flat, h2, w_gate, w_up, w_down, n_slots=n_slots, layer=l)
        xa = _combine(off.reshape(-1), x1, jnp.swapaxes(pos, 1, 2), modtab[l], ln_w[l, 1], ln_b[l, 1], y,
                      m_ctx=m_ctx, alpha=alpha)
    return xa[:, m_ctx:]
```

```python
import functools
import math

import jax
import jax.numpy as jnp
from jax import lax
from jax.experimental import pallas as pl
from jax.experimental.pallas import tpu as pltpu

F32 = jnp.float32
BF16 = jnp.bfloat16
HIGHEST = lax.Precision.HIGHEST

GRID_W = 64
ROPE_THETA = 10000.0
ATT_HEAD_DIM = 64
HEAD_LANES = 128
HGRN_CHUNK = 64
EC_CAPACITY_FACTOR = 2
EPS = 1e-6
TILE = 256
INPROJ_SUB = 256
ATTN_UNROLL = 4
COMBINE_TILE = 128
IDX_SLOT_BLOCKS = 4
WIN_SHORT = 48
WIN_ALIGN = 16
VMEM_LIMIT = 56 * 1024 * 1024


def _cparams(sem):
    return pltpu.CompilerParams(dimension_semantics=sem, vmem_limit_bytes=VMEM_LIMIT)


def _silu(x):
    return x * jax.nn.sigmoid(x)


def _mod_kernel(cs_ref, w_ref, b_ref, o_ref):
    a = _silu(cs_ref[...])
    o_ref[0] = jnp.dot(a, w_ref[0], precision=HIGHEST, preferred_element_type=F32) + b_ref[0]


def _modulation(cs, w_mod, b_mod):
    depth, d, n6 = w_mod.shape
    tn = 1024
    return pl.pallas_call(
        _mod_kernel,
        grid=(depth, n6 // tn),
        in_specs=[pl.BlockSpec((8, d), lambda l, j: (0, 0)),
                  pl.BlockSpec((1, d, tn), lambda l, j: (l, 0, j)),
                  pl.BlockSpec((1, 1, tn), lambda l, j: (l, 0, j))],
        out_specs=pl.BlockSpec((1, 8, tn), lambda l, j: (l, 0, j)),
        out_shape=jax.ShapeDtypeStruct((depth, 8, n6), F32),
        compiler_params=_cparams(("arbitrary", "arbitrary")),
        name="modulation",
    )(cs, w_mod, b_mod.reshape(depth, 1, n6))


def _inproj_kernel(x_ref, mod_ref, w_ref, rc_ref, ra_ref, rb_ref, o_ref, h_scr, *, m_ctx, rope_tiles):
    i = pl.program_id(1)
    j = pl.program_id(2)
    tm = x_ref.shape[1]

    @pl.when(j == 0)
    def _():
        row = i * tm + lax.broadcasted_iota(jnp.int32, (tm, 1), 0)
        is_ctx = row < m_ctx
        mod = mod_ref[0]
        sh = jnp.where(is_ctx, mod[0, 0:1], mod[1, 0:1])
        sc = jnp.where(is_ctx, mod[0, 1:2], mod[1, 1:2])
        h_scr[...] = (x_ref[0] * (1.0 + sc) + sh).astype(BF16)

    h = h_scr[...]
    if rope_tiles:
        rc, ra, rb = rc_ref[0], ra_ref[0], rb_ref[0]
        qscale = jnp.where(j == 0, ATT_HEAD_DIM ** -0.5 * math.log2(math.e), 1.0).astype(F32)

    def finish(c, acc):
        for k in range(INPROJ_SUB // HEAD_LANES):
            blk = acc[:, k * HEAD_LANES:(k + 1) * HEAD_LANES]
            if rope_tiles:
                blk = (blk * rc + pltpu.roll(blk, 16, 1) * ra + pltpu.roll(blk, HEAD_LANES - 16, 1) * rb) * qscale
            lo = c * INPROJ_SUB + k * HEAD_LANES
            o_ref[0, :, lo:lo + HEAD_LANES] = blk.astype(o_ref.dtype)

    pending = None
    for c in range(w_ref.shape[1] // INPROJ_SUB):
        acc = jnp.dot(h, w_ref[:, c * INPROJ_SUB:(c + 1) * INPROJ_SUB], preferred_element_type=F32)
        if pending is not None:
            finish(*pending)
        pending = (c, acc)
    finish(*pending)


def _inproj(xa, modtab, w_bf, tables, *, col0, ncols, out_dtype, rope_tiles, m_ctx):
    b, l, d = xa.shape
    tn = 1024
    tm = l // 8
    joff = col0 // tn
    kern = functools.partial(_inproj_kernel, m_ctx=m_ctx, rope_tiles=rope_tiles)
    tab_spec = pl.BlockSpec((1, tm, HEAD_LANES), lambda bb, i, j: (jnp.where(j < rope_tiles, 0, 1), i, 0))
    return pl.pallas_call(
        kern,
        grid=(b, l // tm, ncols // tn),
        in_specs=[pl.BlockSpec((1, tm, d), lambda bb, i, j: (bb, i, 0)),
                  pl.BlockSpec((1, 2, 6, d), lambda bb, i, j: (bb, 0, 0, 0)),
                  pl.BlockSpec((d, tn), lambda bb, i, j: (0, j + joff)),
                  tab_spec, tab_spec, tab_spec],
        out_specs=pl.BlockSpec((1, tm, tn), lambda bb, i, j: (bb, i, j)),
        out_shape=jax.ShapeDtypeStruct((b, l, ncols), out_dtype),
        scratch_shapes=[pltpu.VMEM((tm, d), BF16)],
        compiler_params=_cparams(("arbitrary", "arbitrary", "arbitrary")),
        name="inproj",
    )(xa, modtab, w_bf, *tables)


def _rope_tables(m_ctx, n_lat):
    nf = ATT_HEAD_DIM // 4
    inv = ROPE_THETA ** (-jnp.arange(nf, dtype=F32) / nf)
    rows = n_lat // GRID_W
    row = jnp.repeat(jnp.arange(rows, dtype=F32), GRID_W)
    col = jnp.tile(jnp.arange(GRID_W, dtype=F32), rows)
    ar, ac = row[:, None] * inv, col[:, None] * inv
    cr, sr, cc, sc = jnp.cos(ar), jnp.sin(ar), jnp.cos(ac), jnp.sin(ac)
    z = jnp.zeros_like(sr)
    c64 = jnp.concatenate([cr, cr, cc, cc], axis=1)
    a64 = jnp.concatenate([z, sr, z, sc], axis=1)
    b64 = jnp.concatenate([-sr, z, -sc, z], axis=1)
    reps = HEAD_LANES // ATT_HEAD_DIM

    def full(t64, fill):
        lat = jnp.tile(t64, (1, reps))
        rot = jnp.concatenate([jnp.full((m_ctx, HEAD_LANES), fill, F32), lat], axis=0)
        return jnp.stack([rot, jnp.full(rot.shape, fill, F32)])

    return full(c64, 1.0), full(a64, 0.0), full(b64, 0.0)


def _attn_kernel(lq_ref, sw_ref, q_ref, k_ref, v_ref, o_ref, vt_scr, m0_scr, l0_scr, acc0_scr, m1_scr, l1_scr,
                 acc1_scr, sa0_scr, sa1_scr, sb0_scr, sb1_scr, *, m_ctx, tk, lambda_init):
    i = pl.program_id(2)
    tq = q_ref.shape[1]
    n_lat = k_ref.shape[1] - m_ctx

    @pl.when(i == 0)
    def _():
        vt_scr[...] = v_ref[0].T

    q = q_ref[0]
    lane = lax.broadcasted_iota(jnp.int32, q.shape, 1)
    zero = jnp.zeros_like(q)
    qs = (jnp.where(lane < ATT_HEAD_DIM, q, zero), jnp.where(lane >= ATT_HEAD_DIM, q, zero))
    stats = ((m0_scr, l0_scr, acc0_scr), (m1_scr, l1_scr, acc1_scr))
    for m_scr, l_scr, acc_scr in stats:
        m_scr[...] = jnp.full(m_scr.shape, -jnp.inf, F32)
        l_scr[...] = jnp.zeros(l_scr.shape, F32)
        acc_scr[...] = jnp.zeros(acc_scr.shape, F32)

    def scores(start, size, st_refs):
        kc = k_ref[0, pl.ds(start, size), :]
        cmax = []
        for s in range(2):
            st = lax.dot_general(kc, qs[s], (((1,), (1,)), ((), ())), preferred_element_type=F32)
            st_refs[s][0:size, :] = st
            cmax.append(jnp.max(st, axis=0, keepdims=True))
        return tuple(cmax)

    def accumulate(start, size, st_refs, cmax):
        vt = vt_scr[:, pl.ds(start, size)]
        for s, (m_scr, l_scr, acc_scr) in enumerate(stats):
            m_old = m_scr[...]
            m_new = jnp.maximum(m_old, cmax[s])
            alpha = jnp.exp2(m_old - m_new)
            p = jnp.exp2(st_refs[s][0:size, :] - m_new)
            l_scr[...] = alpha * l_scr[...] + jnp.sum(p, axis=0, keepdims=True)
            acc_scr[...] = alpha * acc_scr[...] + jnp.dot(vt, p.astype(BF16), preferred_element_type=F32)
            m_scr[...] = m_new

    st_a, st_b = (sa0_scr, sa1_scr), (sb0_scr, sb1_scr)

    @pl.when(i * tq < m_ctx)
    def _():
        accumulate(0, m_ctx, st_a, scores(0, m_ctx, st_a))

    @pl.when(i * tq >= m_ctx)
    def _():
        def at(c):
            return pl.multiple_of(m_ctx + c * tk, math.gcd(m_ctx, tk))
        nck = n_lat // tk
        bufs = (st_b, st_a)

        def run(first, count, cm):
            for u in range(count):
                c = first + u
                last = isinstance(c, int) and c + 1 >= nck
                cm_next = None if last else scores(at(c + 1), tk, bufs[(u + 1) % 2])
                accumulate(at(c), tk, bufs[u % 2], cm)
                cm = cm_next
            return cm

        cm_ctx = scores(0, m_ctx, st_a)
        cm = scores(at(0), tk, st_b)
        accumulate(0, m_ctx, st_a, cm_ctx)
        trips = (nck - 1) // ATTN_UNROLL
        cm = lax.fori_loop(0, trips, lambda j, cm: run(j * ATTN_UNROLL, ATTN_UNROLL, cm), cm)
        run(trips * ATTN_UNROLL, nck - trips * ATTN_UNROLL, cm)

    lq = lq_ref[...]
    lam = (jnp.exp(jnp.sum(lq[0:1] * lq[1:2], axis=1, keepdims=True))
           - jnp.exp(jnp.sum(lq[2:3] * lq[3:4], axis=1, keepdims=True)) + lambda_init)
    ot = acc0_scr[...] / l0_scr[...] - lam * (acc1_scr[...] / l1_scr[...])
    ms = jnp.mean(ot * ot, axis=0, keepdims=True)
    o = (ot * lax.rsqrt(ms + EPS)).T * sw_ref[...] * (1.0 - lambda_init)
    o_ref[0] = o.astype(o_ref.dtype)


def _attention(qkv, lambda_qk, subln_w, *, m_ctx, lambda_init):
    b, l, w3 = qkv.shape
    heads = w3 // (3 * HEAD_LANES)
    tq = TILE
    tk = 512
    kern = functools.partial(_attn_kernel, m_ctx=m_ctx, tk=tk, lambda_init=lambda_init)
    return pl.pallas_call(
        kern,
        grid=(b, heads, l // tq),
        in_specs=[pl.BlockSpec((4, ATT_HEAD_DIM), lambda bb, h, i: (0, 0)),
                  pl.BlockSpec((1, HEAD_LANES), lambda bb, h, i: (0, 0)),
                  pl.BlockSpec((1, tq, HEAD_LANES), lambda bb, h, i: (bb, i, h)),
                  pl.BlockSpec((1, l, HEAD_LANES), lambda bb, h, i: (bb, 0, heads + h)),
                  pl.BlockSpec((1, l, HEAD_LANES), lambda bb, h, i: (bb, 0, 2 * heads + h))],
        out_specs=pl.BlockSpec((1, tq, HEAD_LANES), lambda bb, h, i: (bb, i, h)),
        out_shape=jax.ShapeDtypeStruct((b, l, heads * HEAD_LANES), BF16),
        scratch_shapes=[pltpu.VMEM((HEAD_LANES, l), BF16)]
        + [pltpu.VMEM((1, tq), F32), pltpu.VMEM((1, tq), F32), pltpu.VMEM((HEAD_LANES, tq), F32)] * 2
        + [pltpu.VMEM((max(tk, m_ctx), tq), F32)] * 4,
        compiler_params=_cparams(("arbitrary", "arbitrary", "arbitrary")),
        name="diff_attention",
    )(lambda_qk, subln_w.reshape(1, HEAD_LANES), qkv, qkv, qkv)


def _log_sigmoid(z):
    return jnp.minimum(z, 0.0) - jnp.log1p(jnp.exp(-jnp.abs(z)))


def _forget_gate(z, lb):
    ls = _log_sigmoid(z)
    key = jax.nn.sigmoid(-z)
    if lb is None:
        return ls, key
    a = jnp.log(lb)
    b = jnp.log1p(-lb) + ls
    logf = jnp.maximum(a, b) + jnp.log1p(jnp.exp(-jnp.abs(a - b)))
    return logf, (1.0 - lb) * key


def _hgrn_tile(dirs):
    ch = HGRN_CHUNK
    nt_dims = (((1,), (1,)), ((), ()))
    work = []
    for q_ref, v_ref, z_ref, o_ref, s_scr, lb, reverse in dirs:
        t = q_ref.shape[1]
        logf, key = _forget_gate(z_ref[0], lb)
        r = lax.broadcasted_iota(jnp.int32, (t, t), 0)
        c = lax.broadcasted_iota(jnp.int32, (t, t), 1)
        tri = ((r // ch) == (c // ch)) & ((c >= r) if reverse else (c <= r))
        a = jnp.dot(tri.astype(F32), logf, precision=HIGHEST, preferred_element_type=F32)
        work.append(dict(q=q_ref[0], v=v_ref[0].astype(BF16), key=key, a=a, tri=tri, o_ref=o_ref, s_scr=s_scr,
                         reverse=reverse, nch=t // ch))

    for w in work:
        a, nch = w["a"], w["nch"]

        def per_chunk(row_of):
            return jnp.concatenate([jnp.broadcast_to(row_of(a[k * ch:(k + 1) * ch]), (ch, a.shape[1]))
                                    for k in range(nch)], axis=0)
        a_mid = per_chunk(lambda ac: ac[ch // 2 - 1:ch // 2])
        a_end = per_chunk((lambda ac: ac[0:1]) if w["reverse"] else (lambda ac: ac[ch - 1:ch]))
        qe = (w["q"] * jnp.exp(a - a_mid)).astype(BF16)
        ke = (w["key"] * jnp.exp(a_mid - a)).astype(BF16)
        kd = (w["key"] * jnp.exp(a_end - a)).astype(BF16)
        w["qa"] = (w["q"] * jnp.exp(a)).astype(BF16)
        w["decay"] = jnp.exp(a_end)
        w["sc"] = lax.dot_general(qe, ke, nt_dims, preferred_element_type=F32)
        w["kv"] = [lax.dot_general(w["v"][k * ch:(k + 1) * ch], kd[k * ch:(k + 1) * ch], (((0,), (0,)), ((), ())),
                                   preferred_element_type=F32) for k in range(nch)]

    for w in work:
        w["intra"] = jnp.dot(jnp.where(w["tri"], w["sc"], 0.0).astype(BF16), w["v"], preferred_element_type=F32)

    for w in work:
        nch = w["nch"]
        st = w["s_scr"][...]
        states = [None] * nch
        for k in (range(nch - 1, -1, -1) if w["reverse"] else range(nch)):
            states[k] = st.astype(BF16)
            row = k * ch if w["reverse"] else (k + 1) * ch - 1
            st = w["decay"][row:row + 1] * st + w["kv"][k]
        w["s_scr"][...] = st
        w["states"] = states

    for w in work:
        inter = [lax.dot_general(w["qa"][k * ch:(k + 1) * ch], w["states"][k], nt_dims, preferred_element_type=F32)
                 for k in range(w["nch"])]
        w["o_ref"][0] = jnp.concatenate(inter, axis=0) + w["intra"]


def _hgrn_kernel(lbl_ref, qf_ref, vf_ref, zf_ref, qb_ref, vb_ref, zb_ref, of_ref, ob_ref, sf_scr, sb_scr, *, layer):
    @pl.when(pl.program_id(2) == 0)
    def _():
        sf_scr[...] = jnp.zeros(sf_scr.shape, F32)
        sb_scr[...] = jnp.zeros(sb_scr.shape, F32)

    if layer == 0:
        lbf = lbb = None
    else:
        lg = lbl_ref[...]
        ex = jnp.exp(lg - jnp.max(lg, axis=1, keepdims=True))
        sm = ex / jnp.sum(ex, axis=1, keepdims=True)
        lb = sm[:, 1]
        for k in range(2, layer + 1):
            lb = lb + sm[:, k]
        lbf, lbb = lb[0:1], lb[1:2]
    _hgrn_tile([(qf_ref, vf_ref, zf_ref, of_ref, sf_scr, lbf, False),
                (qb_ref, vb_ref, zb_ref, ob_ref, sb_scr, lbb, True)])


def _hgrn(cg, lb_logits, *, layer, hg_heads, col_q, col_i, col_ff, col_fb):
    b, l, _ = cg.shape
    t = TILE
    nb = l // t
    depth = lb_logits.shape[1]

    def fwd(col):
        return pl.BlockSpec((1, t, HEAD_LANES), lambda bb, h, i: (bb, i, col + h))

    def bwd(col):
        return pl.BlockSpec((1, t, HEAD_LANES), lambda bb, h, i: (bb, jnp.where(i == 0, 0, nb - i), col + h))

    out_f = pl.BlockSpec((1, t, HEAD_LANES), lambda bb, h, i: (bb, i, h))
    out_b = pl.BlockSpec((1, t, HEAD_LANES), lambda bb, h, i: (bb, jnp.where(i == 0, 0, nb - i), h))
    shp = jax.ShapeDtypeStruct((b, l, hg_heads * HEAD_LANES), F32)
    return pl.pallas_call(
        functools.partial(_hgrn_kernel, layer=layer),
        grid=(b, hg_heads, nb),
        in_specs=[pl.BlockSpec((2, depth, HEAD_LANES), lambda bb, h, i: (0, 0, h)),
                  fwd(col_q), fwd(col_i), fwd(col_ff), bwd(col_q), bwd(col_i), bwd(col_fb)],
        out_specs=[out_f, out_b],
        out_shape=[shp, shp],
        scratch_shapes=[pltpu.VMEM((HEAD_LANES, HEAD_LANES), F32), pltpu.VMEM((HEAD_LANES, HEAD_LANES), F32)],
        compiler_params=_cparams(("arbitrary", "arbitrary", "arbitrary")),
        name="hgrn2",
    )(lb_logits, cg, cg, cg, cg, cg, cg)


def _layer_norm(r, w, b):
    mu = jnp.mean(r, axis=1, keepdims=True)
    var = jnp.mean(jnp.square(r - mu), axis=1, keepdims=True)
    return (r - mu) * lax.rsqrt(var + EPS) * w + b


def _outproj_kernel(x_ref, att_ref, cx_ref, cb_ref, cc_ref, cxp_ref, ccp_ref, cxn_ref, ccn_ref, of_ref, ob_ref,
                    gg_ref, wconv_ref, hnw_ref, wout_ref, mod_ref, lnw_ref, lnb_ref, wr_ref,
                    x1_ref, h2_ref, aff_ref, *, ctx_tiles, alpha, n_experts):
    i = pl.program_id(1)
    nt = pl.num_programs(1)
    tm = x_ref.shape[1]
    aw = att_ref.shape[2]
    cw = cx_ref.shape[2]

    u = cc_ref[0] * cx_ref[0]
    prev_ok = jnp.logical_and(i != 0, i != ctx_tiles)
    next_ok = jnp.logical_and(i != ctx_tiles - 1, i != nt - 1)
    u_before = jnp.where(prev_ok, (ccp_ref[0] * cxp_ref[0])[7:8], 0.0)
    u_after = jnp.where(next_ok, (ccn_ref[0] * cxn_ref[0])[0:1], 0.0)
    row = lax.broadcasted_iota(jnp.int32, (tm, 1), 0)
    u_prev = jnp.where(row == 0, u_before, pltpu.roll(u, 1, 0))
    u_next = jnp.where(row == tm - 1, u_after, pltpu.roll(u, tm - 1, 0))
    wc = wconv_ref[...]
    conv = cb_ref[0] * (u_prev * wc[0:1] + u * wc[1:2] + u_next * wc[2:3])

    o = of_ref[0] + ob_ref[0]
    gg = gg_ref[0]
    recs = []
    for h in range(o.shape[1] // HEAD_LANES):
        oh = o[:, h * HEAD_LANES:(h + 1) * HEAD_LANES]
        ms = jnp.mean(oh * oh, axis=1, keepdims=True)
        recs.append(oh * lax.rsqrt(ms + EPS) * hnw_ref[...] * _silu(gg[:, h * HEAD_LANES:(h + 1) * HEAD_LANES]))
    rec = jnp.concatenate(recs, axis=1)

    y = jnp.dot(att_ref[0], wout_ref[0:aw, :], preferred_element_type=F32)
    y = y + jnp.dot(conv.astype(BF16), wout_ref[aw:aw + cw, :], preferred_element_type=F32)
    y = y + jnp.dot(rec.astype(BF16), wout_ref[aw + cw:, :], preferred_element_type=F32)

    mod = mod_ref[0, 0]
    x1 = _layer_norm(alpha * x_ref[0] + mod[2:3] * y, lnw_ref[...], lnb_ref[...])
    x1_ref[0] = x1
    h2 = x1 * (1.0 + mod[4:5]) + mod[3:4]
    d = h2.shape[1]
    h2_ref[0, :, :d] = h2
    wr = wr_ref[...]
    wr_hi = wr.astype(BF16)
    wr_lo = (wr - wr_hi.astype(F32)).astype(BF16)
    h2_hi = h2.astype(BF16)
    h2_lo = (h2 - h2_hi.astype(F32)).astype(BF16)
    logits = (jnp.dot(h2_hi, wr_hi, preferred_element_type=F32) + jnp.dot(h2_hi, wr_lo, preferred_element_type=F32)
              + jnp.dot(h2_lo, wr_hi, preferred_element_type=F32))
    lane = lax.broadcasted_iota(jnp.int32, logits.shape, 1)
    logits = jnp.where(lane < n_experts, logits, -jnp.inf)
    ex = jnp.exp(logits - jnp.max(logits, axis=1, keepdims=True))
    aff = ex / jnp.sum(ex, axis=1, keepdims=True)
    aff_ref[0] = aff[:, :n_experts]
    h2_ref[0, :, d:] = aff


def _outproj(xa, att, cg, o_f, o_b, w_conv, hgrn_norm_w, w_out_bf, modtab, ln_w, ln_b, w_router_pad, *,
             m_ctx, alpha, n_experts, cw, col_gg):
    b, l, d = xa.shape
    tm = TILE
    aw = att.shape[2]
    hw = o_f.shape[2]
    r8 = tm // 8
    last8 = l // 8 - 1
    ctx_tiles = m_ctx // tm

    def rows(width, col):
        return pl.BlockSpec((1, tm, width), lambda bb, i: (bb, i, col))

    def halo_prev(col):
        return pl.BlockSpec((1, 8, cw), lambda bb, i: (bb, jnp.maximum(i * r8 - 1, 0), col))

    def halo_next(col):
        return pl.BlockSpec((1, 8, cw), lambda bb, i: (bb, jnp.minimum((i + 1) * r8, last8), col))

    def const(shape):
        return pl.BlockSpec(shape, lambda bb, i: (0,) * len(shape))

    kern = functools.partial(_outproj_kernel, ctx_tiles=ctx_tiles, alpha=alpha, n_experts=n_experts)
    return pl.pallas_call(
        kern,
        grid=(b, l // tm),
        in_specs=[rows(d, 0), rows(aw, 0),
                  rows(cw, 0), rows(cw, 1), rows(cw, 2),
                  halo_prev(0), halo_prev(2), halo_next(0), halo_next(2),
                  rows(hw, 0), rows(hw, 0), rows(hw, col_gg),
                  const((3, cw)), const((1, HEAD_LANES)), const((d, d)),
                  pl.BlockSpec((1, 1, 6, d), lambda bb, i: (bb, jnp.minimum(i // ctx_tiles, 1), 0, 0)),
                  const((1, d)), const((1, d)), const((d, HEAD_LANES))],
        out_specs=[rows(d, 0), rows(d + HEAD_LANES, 0), rows(n_experts, 0)],
        out_shape=[jax.ShapeDtypeStruct((b, l, d), F32), jax.ShapeDtypeStruct((b, l, d + HEAD_LANES), F32),
                   jax.ShapeDtypeStruct((b, l, n_experts), F32)],
        compiler_params=_cparams(("arbitrary", "arbitrary")),
        name="outproj_ln_router",
    )(xa, att, cg, cg, cg, cg, cg, cg, cg, o_f, o_b, cg, w_conv, hgrn_norm_w.reshape(1, HEAD_LANES), w_out_bf,
      modtab, ln_w.reshape(1, d), ln_b.reshape(1, d), w_router_pad)


def _topk_kernel(aff_ref, pos_ref, off_ref, idx_ref, cnt_scr, *, m_ctx, cap_ctx, cap_lat):
    aff = aff_ref[0]
    ne, l = aff.shape
    ch = COMBINE_TILE
    nch = l // ch
    ctx_ch = m_ctx // ch
    bits = pltpu.bitcast(aff, jnp.int32)
    lane = lax.broadcasted_iota(jnp.int32, (ne, l), 1)
    in_ctx = lane < m_ctx

    def kth_largest(seg, k):
        def body(_, lohi):
            lo, hi = lohi
            mid = lo + lax.shift_right_logical(hi - lo, 1)
            cnt = jnp.sum(jnp.where(jnp.logical_and(seg, bits >= mid), 1.0, 0.0), axis=1, keepdims=True)
            ge = cnt >= k
            return jnp.where(ge, mid, lo), jnp.where(ge, hi, mid)
        lo0 = jnp.zeros((ne, 1), jnp.int32)
        hi0 = jnp.full((ne, 1), 0x7F800000, jnp.int32)
        return lax.fori_loop(0, 31, body, (lo0, hi0))[0]

    thr = jnp.where(in_ctx, kth_largest(in_ctx, cap_ctx), kth_largest(jnp.logical_not(in_ctx), cap_lat))
    gt = bits > thr
    eq = bits == thr
    gtf = jnp.where(gt, 1.0, 0.0)
    n_gt_ctx = jnp.sum(jnp.where(in_ctx, gtf, 0.0), axis=1, keepdims=True)
    n_gt_lat = jnp.sum(jnp.where(in_ctx, 0.0, gtf), axis=1, keepdims=True)
    need = jnp.where(in_ctx, cap_ctx - n_gt_ctx, cap_lat - n_gt_lat)

    tr = lax.broadcasted_iota(jnp.int32, (ch, ch), 0)
    tc = lax.broadcasted_iota(jnp.int32, (ch, ch), 1)
    tri = jnp.where(tr <= tc, 1.0, 0.0).astype(BF16)

    def seg_prefix(flag):
        excl, bases = [], []
        base = jnp.zeros((ne, 1), F32)
        for c in range(nch):
            if c == ctx_ch:
                base = jnp.zeros((ne, 1), F32)
            fc = flag[:, c * ch:(c + 1) * ch]
            incl = jnp.dot(fc.astype(BF16), tri, preferred_element_type=F32)
            excl.append(base + incl - fc)
            bases.append(base)
            base = base + incl[:, ch - 1:ch]
        bases.append(base)
        return jnp.concatenate(excl, axis=1), bases

    eq_excl, _ = seg_prefix(jnp.where(eq, 1.0, 0.0))
    sel = jnp.logical_or(gt, jnp.logical_and(eq, eq_excl < need))
    self_ = jnp.where(sel, 1.0, 0.0)
    sel_excl, bases = seg_prefix(self_)
    seg_off = jnp.where(in_ctx, 0.0, float(cap_ctx))
    pos_ref[0] = jnp.where(sel, sel_excl + seg_off, -1.0).astype(jnp.int32)

    olane = lax.broadcasted_iota(jnp.int32, (ne, HEAD_LANES), 1)
    off = jnp.zeros((ne, HEAD_LANES), F32)
    for c in range(nch + 1):
        off = jnp.where(olane == c, bases[c] + (0.0 if c < ctx_ch else float(cap_ctx)), off)
    off_ref[0] = off.astype(jnp.int32)

    cnt = sel_excl + self_ + seg_off
    for e in range(ne):
        cnt_scr[e] = cnt[e:e + 1]
    n_slots = idx_ref.shape[1]
    sb = n_slots // IDX_SLOT_BLOCKS
    ilane = lax.broadcasted_iota(jnp.int32, (sb, HEAD_LANES), 1)
    for blk in range(IDX_SLOT_BLOCKS):
        slot = (lax.broadcasted_iota(jnp.int32, (sb, HEAD_LANES), 0) + blk * sb).astype(F32)

        def per_expert(e, out):
            def per_chunk(c, acc):
                cnt_row = cnt_scr[e, :, pl.ds(pl.multiple_of(c * HEAD_LANES, HEAD_LANES), HEAD_LANES)]
                return acc + jnp.where(cnt_row <= slot, 1.0, 0.0)
            acc = lax.fori_loop(0, l // HEAD_LANES, per_chunk, jnp.zeros((sb, HEAD_LANES), F32))
            return jnp.where(ilane == e, jnp.sum(acc, axis=1, keepdims=True), out)
        out = lax.fori_loop(0, ne, per_expert, jnp.zeros((sb, HEAD_LANES), F32))
        idx_ref[0, blk * sb:(blk + 1) * sb, :] = out.astype(jnp.int32)


def _topk(aff_t, *, m_ctx, cap_ctx, cap_lat):
    b, ne, l = aff_t.shape
    n_slots = cap_ctx + cap_lat
    kern = functools.partial(_topk_kernel, m_ctx=m_ctx, cap_ctx=cap_ctx, cap_lat=cap_lat)
    return pl.pallas_call(
        kern,
        grid=(b,),
        in_specs=[pl.BlockSpec((1, ne, l), lambda bb: (bb, 0, 0))],
        out_specs=[pl.BlockSpec((1, ne, l), lambda bb: (bb, 0, 0)),
                   pl.BlockSpec((1, ne, HEAD_LANES), lambda bb: (bb, 0, 0)),
                   pl.BlockSpec((1, n_slots, HEAD_LANES), lambda bb: (bb, 0, 0))],
        out_shape=[jax.ShapeDtypeStruct((b, ne, l), jnp.int32),
                   jax.ShapeDtypeStruct((b, ne, HEAD_LANES), jnp.int32),
                   jax.ShapeDtypeStruct((b, n_slots, HEAD_LANES), jnp.int32)],
        scratch_shapes=[pltpu.VMEM((ne, 1, l), F32)],
        compiler_params=_cparams(("arbitrary",)),
        name="expert_choice_topk",
    )(aff_t)


def _ffn_kernel(idx_ref, h_hbm, wg_ref, wu_ref, wd_ref, y_ref, xg_scr, xb_scr, gate_scr, acc_scr, sem, *, n_experts,
                steps):
    e, b, f = pl.program_id(0), pl.program_id(1), pl.program_id(2)
    nb, nf = pl.num_programs(1), pl.num_programs(2)
    n_slots, d = xb_scr.shape

    def row_copy(bb, ee, r):
        tok = idx_ref[(bb * n_experts + ee) * n_slots + r]
        return pltpu.make_async_copy(h_hbm.at[bb, pl.ds(tok, 1)], xg_scr.at[pl.ds(r, 1)], sem.at[0])

    @pl.when(f == 0)
    def _():
        @pl.when(jnp.logical_and(e == 0, b == 0))
        def _():
            def issue(r, carry):
                row_copy(b, e, r).start()
                return carry
            lax.fori_loop(0, n_slots, issue, 0, unroll=8)
        pltpu.make_async_copy(h_hbm.at[b, pl.ds(0, n_slots)], xg_scr, sem.at[0]).wait()
        xb_scr[...] = xg_scr[:, :d].astype(BF16)
        tail = xg_scr[:, d:]
        lane = lax.broadcasted_iota(jnp.int32, tail.shape, 1)
        gate_scr[...] = jnp.sum(jnp.where(lane == e, tail, 0.0), axis=1, keepdims=True)
        acc_scr[...] = jnp.zeros(acc_scr.shape, F32)

    wrap = b + 1 == nb
    nxt_b = jnp.where(wrap, 0, b + 1)
    nxt_e = jnp.where(wrap, jnp.where(e + 1 == n_experts, 0, e + 1), e)
    share = n_slots // steps
    for r in range(share):
        row_copy(nxt_b, nxt_e, f * share + r).start()

    wg, wu, wd = wg_ref[0, 0].astype(BF16), wu_ref[0, 0].astype(BF16), wd_ref[0, 0].astype(BF16)
    half = n_slots // 2
    rows = [slice(0, half), slice(half, n_slots)]
    au = [(jnp.dot(xb_scr[r, :], wg, preferred_element_type=F32), jnp.dot(xb_scr[r, :], wu, preferred_element_type=F32))
          for r in rows]
    down = [jnp.dot((_silu(a) * u).astype(BF16), wd, preferred_element_type=F32) for a, u in au]
    for r, dn in zip(rows, down):
        acc_scr[r, :] += dn

    @pl.when(f == nf - 1)
    def _():
        y_ref[0, 0] = (acc_scr[...] * gate_scr[...]).astype(y_ref.dtype)

        @pl.when(jnp.logical_and(e == n_experts - 1, wrap))
        def _():
            pltpu.make_async_copy(h_hbm.at[b, pl.ds(0, n_slots)], xg_scr, sem.at[0]).wait()


def _expert_ffn(idx_flat, h2, w_gate, w_up, w_down, *, n_slots, layer):
    b, l, dx = h2.shape
    _, ne, d, ff = w_gate.shape
    tf = 256
    grid_spec = pltpu.PrefetchScalarGridSpec(
        num_scalar_prefetch=1,
        grid=(ne, b, ff // tf),
        in_specs=[pl.BlockSpec(memory_space=pl.ANY),
                  pl.BlockSpec((1, 1, d, tf), lambda e, bb, f, idx: (layer, e, 0, f)),
                  pl.BlockSpec((1, 1, d, tf), lambda e, bb, f, idx: (layer, e, 0, f)),
                  pl.BlockSpec((1, 1, tf, d), lambda e, bb, f, idx: (layer, e, f, 0))],
        out_specs=pl.BlockSpec((1, 1, n_slots, d), lambda e, bb, f, idx: (bb, e, 0, 0)),
        scratch_shapes=[pltpu.VMEM((n_slots, dx), F32), pltpu.VMEM((n_slots, d), BF16),
                        pltpu.VMEM((n_slots, 1), F32), pltpu.VMEM((n_slots, d), F32),
                        pltpu.SemaphoreType.DMA((1,))])
    assert ff % tf == 0 and n_slots % (ff // tf) == 0
    return pl.pallas_call(
        functools.partial(_ffn_kernel, n_experts=ne, steps=ff // tf),
        grid_spec=grid_spec,
        out_shape=jax.ShapeDtypeStruct((b, ne, n_slots, d), BF16),
        compiler_params=_cparams(("arbitrary", "arbitrary", "arbitrary")),
        name="expert_ffn",
    )(idx_flat, h2, w_gate, w_up, w_down)


def _combine_kernel(off_ref, x1_ref, pos_ref, mod_ref, lnw_ref, lnb_ref, y_hbm, o_ref, win_scr, sem, *,
                    alpha, n_experts, win):
    b, i = pl.program_id(0), pl.program_id(1)
    nb, nt = pl.num_programs(0), pl.num_programs(1)
    tm = x1_ref.shape[1]
    n_slots = y_hbm.shape[2]
    blk = win_scr.shape[1] // n_experts
    step = b * nt + i
    slot = step % 2

    def window(bb, ii, e):
        base = (bb * n_experts + e) * HEAD_LANES + ii
        start = jnp.minimum((off_ref[base] // WIN_ALIGN) * WIN_ALIGN, n_slots - win)
        return pl.multiple_of(start, WIN_ALIGN), off_ref[base + 1] - start <= WIN_SHORT

    def transfer(bb, ii, sl, go):
        for e in range(n_experts):
            start, short = window(bb, ii, e)
            for size, cond in ((WIN_SHORT, short), (win, jnp.logical_not(short))):
                @pl.when(cond)
                def _():
                    go(pltpu.make_async_copy(y_hbm.at[bb, e, pl.ds(start, size)],
                                             win_scr.at[sl, pl.ds(e * blk, size)], sem.at[sl, e]))

    @pl.when(step == 0)
    def _():
        win_scr[...] = jnp.zeros(win_scr.shape, win_scr.dtype)
        transfer(b, i, slot, lambda cp: cp.start())

    @pl.when(step + 1 < nb * nt)
    def _():
        wrap = i + 1 == nt
        transfer(jnp.where(wrap, b + 1, b), jnp.where(wrap, 0, i + 1), 1 - slot, lambda cp: cp.start())

    pos = pos_ref[0]
    scol = lax.broadcasted_iota(jnp.int32, (tm, blk), 1)
    onehot = jnp.concatenate([jnp.where((pos[:, e:e + 1] - window(b, i, e)[0]) == scol, 1.0, 0.0).astype(BF16)
                              for e in range(n_experts)], axis=1)
    transfer(b, i, slot, lambda cp: cp.wait())
    moe = jnp.dot(onehot, win_scr[slot], preferred_element_type=F32)
    mod = mod_ref[0, 0]
    o_ref[0] = _layer_norm(alpha * x1_ref[0] + mod[5:6] * moe, lnw_ref[...], lnb_ref[...])


def _combine(off_flat, x1, pos_tok, modtab, ln_w, ln_b, y, *, m_ctx, alpha, latent_only):
    b, l, d = x1.shape
    ne = pos_tok.shape[2]
    tm = COMBINE_TILE
    win = tm + WIN_ALIGN
    blk = -(-win // HEAD_LANES) * HEAD_LANES
    ctx_tiles = m_ctx // tm

    def rows(width):
        return pl.BlockSpec((1, tm, width), lambda bb, i, off: (bb, i, 0))

    out_rows = l - m_ctx if latent_only else l
    out_spec = (pl.BlockSpec((1, tm, d), lambda bb, i, off: (bb, jnp.maximum(i - ctx_tiles, 0), 0))
                if latent_only else rows(d))

    grid_spec = pltpu.PrefetchScalarGridSpec(
        num_scalar_prefetch=1,
        grid=(b, l // tm),
        in_specs=[rows(d), rows(ne),
                  pl.BlockSpec((1, 1, 6, d), lambda bb, i, off: (bb, jnp.minimum(i // ctx_tiles, 1), 0, 0)),
                  pl.BlockSpec((1, d), lambda bb, i, off: (0, 0)),
                  pl.BlockSpec((1, d), lambda bb, i, off: (0, 0)),
                  pl.BlockSpec(memory_space=pl.ANY)],
        out_specs=out_spec,
        scratch_shapes=[pltpu.VMEM((2, ne * blk, d), BF16), pltpu.SemaphoreType.DMA((2, ne))])
    return pl.pallas_call(
        functools.partial(_combine_kernel, alpha=alpha, n_experts=ne, win=win),
        grid_spec=grid_spec,
        out_shape=jax.ShapeDtypeStruct((b, out_rows, d), F32),
        compiler_params=_cparams(("arbitrary", "arbitrary")),
        name="moe_combine_ln",
    )(off_flat, x1, pos_tok, modtab, ln_w.reshape(1, d), ln_b.reshape(1, d), y)


def kernel(x, c, ctx, c_ctx, w_mod, b_mod, w_in, w_conv, lambda_qk, subln_w, hgrn_lb_logits, hgrn_norm_w, w_out,
           ln_w, ln_b, w_router, w_gate, w_up, w_down):
    bsz, n_lat, d = x.shape
    m_ctx = ctx.shape[1]
    depth = w_mod.shape[0]
    ne = w_router.shape[2]
    aw, cw, hw = d // 2, d // 4, d // 4
    assert m_ctx % TILE == 0 and n_lat % 1024 == 0 and n_lat % GRID_W == 0 and bsz + 1 <= 8
    assert (m_ctx + n_lat) % (8 * 16) == 0 and aw == 1024 and ne <= HEAD_LANES
    alpha = (2.0 * depth) ** 0.25
    cap_ctx = EC_CAPACITY_FACTOR * m_ctx // ne
    cap_lat = EC_CAPACITY_FACTOR * n_lat // ne
    n_slots = cap_ctx + cap_lat
    assert n_slots % WIN_ALIGN == 0 and n_slots >= COMBINE_TILE + WIN_ALIGN
    assert (m_ctx + n_lat) // COMBINE_TILE + 1 <= HEAD_LANES

    cs = jnp.zeros((8, d), F32).at[:bsz].set(c).at[bsz].set(c_ctx)
    mod = _modulation(cs, w_mod, b_mod)
    mod_lat = mod[:, :bsz].reshape(depth, bsz, 1, 6, d)
    mod_ctx = jnp.broadcast_to(mod[:, bsz].reshape(depth, 1, 1, 6, d), (depth, bsz, 1, 6, d))
    modtab = jnp.concatenate([mod_ctx, mod_lat], axis=2)

    tables = _rope_tables(m_ctx, n_lat)
    xa = jnp.concatenate([ctx, x], axis=1)
    w_router_pad = jnp.zeros((depth, d, HEAD_LANES), F32).at[:, :, :ne].set(w_router)
    hg_heads = hw // HEAD_LANES
    cb0 = 3 * cw // HEAD_LANES

    for l in range(depth):
        lambda_init = 0.8 - 0.6 * math.exp(-0.3 * l)
        w_in_bf = w_in[l].astype(BF16)
        qkv = _inproj(xa, modtab[l], w_in_bf, tables, col0=0, ncols=3 * aw, out_dtype=BF16, rope_tiles=2,
                      m_ctx=m_ctx)
        cg = _inproj(xa, modtab[l], w_in_bf, tables, col0=3 * aw, ncols=3 * cw + 5 * hw, out_dtype=F32,
                     rope_tiles=0, m_ctx=m_ctx)
        att = _attention(qkv, lambda_qk[l], subln_w[l], m_ctx=m_ctx, lambda_init=lambda_init)
        o_f, o_b = _hgrn(cg, hgrn_lb_logits, layer=l, hg_heads=hg_heads, col_q=cb0, col_i=cb0 + hg_heads,
                         col_ff=cb0 + 3 * hg_heads, col_fb=cb0 + 4 * hg_heads)
        x1, h2, aff = _outproj(xa, att, cg, o_f, o_b, w_conv[l], hgrn_norm_w[l], w_out[l].astype(BF16), modtab[l],
                               ln_w[l, 0], ln_b[l, 0], w_router_pad[l], m_ctx=m_ctx, alpha=alpha, n_experts=ne,
                               cw=cw, col_gg=(3 * cw + 2 * hw) // hw)
        pos, off, idx = _topk(jnp.swapaxes(aff, 1, 2), m_ctx=m_ctx, cap_ctx=cap_ctx, cap_lat=cap_lat)
        idx_flat = jnp.swapaxes(idx[:, :, :ne], 1, 2).reshape(-1)
        y = _expert_ffn(idx_flat, h2, w_gate, w_up, w_down, n_slots=n_slots, layer=l)
        xa = _combine(off.reshape(-1), x1, jnp.swapaxes(pos, 1, 2), modtab[l], ln_w[l, 1], ln_b[l, 1], y,
                      m_ctx=m_ctx, alpha=alpha, latent_only=l == depth - 1)
    return xa
```

```python
import functools
import math

import jax
import jax.numpy as jnp
from jax import lax
from jax.experimental import pallas as pl
from jax.experimental.pallas import tpu as pltpu

F32 = jnp.float32
BF16 = jnp.bfloat16
HIGHEST = lax.Precision.HIGHEST

GRID_W = 64
ROPE_THETA = 10000.0
ATT_HEAD_DIM = 64
HEAD_LANES = 128
HGRN_CHUNK = 64
HGRN_SAFE_EXP = 80.0
EC_CAPACITY_FACTOR = 2
EPS = 1e-6
TILE = 256
INPROJ_SUB = 256
ATTN_UNROLL = 4
COMBINE_TILE = 128
IDX_SLOT_BLOCKS = 4
WIN_SHORT = 48
WIN_ALIGN = 16
VMEM_LIMIT = 56 * 1024 * 1024


def _cparams(sem):
    return pltpu.CompilerParams(dimension_semantics=sem, vmem_limit_bytes=VMEM_LIMIT)


def _silu(x):
    return x * jax.nn.sigmoid(x)


def _mod_kernel(cs_ref, w_ref, b_ref, o_ref):
    a = _silu(cs_ref[...])
    o_ref[0] = jnp.dot(a, w_ref[0], precision=HIGHEST, preferred_element_type=F32) + b_ref[0]


def _modulation(cs, w_mod, b_mod):
    depth, d, n6 = w_mod.shape
    tn = 1024
    return pl.pallas_call(
        _mod_kernel,
        grid=(depth, n6 // tn),
        in_specs=[pl.BlockSpec((8, d), lambda l, j: (0, 0)),
                  pl.BlockSpec((1, d, tn), lambda l, j: (l, 0, j)),
                  pl.BlockSpec((1, 1, tn), lambda l, j: (l, 0, j))],
        out_specs=pl.BlockSpec((1, 8, tn), lambda l, j: (l, 0, j)),
        out_shape=jax.ShapeDtypeStruct((depth, 8, n6), F32),
        compiler_params=_cparams(("arbitrary", "arbitrary")),
        name="modulation",
    )(cs, w_mod, b_mod.reshape(depth, 1, n6))


def _inproj_kernel(x_ref, mod_ref, w_ref, rc_ref, ra_ref, rb_ref, o_ref, h_scr, *, m_ctx, rope_tiles):
    i = pl.program_id(1)
    j = pl.program_id(2)
    tm = x_ref.shape[1]

    @pl.when(j == 0)
    def _():
        row = i * tm + lax.broadcasted_iota(jnp.int32, (tm, 1), 0)
        is_ctx = row < m_ctx
        mod = mod_ref[0]
        sh = jnp.where(is_ctx, mod[0, 0:1], mod[1, 0:1])
        sc = jnp.where(is_ctx, mod[0, 1:2], mod[1, 1:2])
        h_scr[...] = (x_ref[0] * (1.0 + sc) + sh).astype(BF16)

    h = h_scr[...]
    if rope_tiles:
        rc, ra, rb = rc_ref[0], ra_ref[0], rb_ref[0]
        qscale = jnp.where(j == 0, ATT_HEAD_DIM ** -0.5 * math.log2(math.e), 1.0).astype(F32)

    def finish(c, acc):
        for k in range(INPROJ_SUB // HEAD_LANES):
            blk = acc[:, k * HEAD_LANES:(k + 1) * HEAD_LANES]
            if rope_tiles:
                blk = (blk * rc + pltpu.roll(blk, 16, 1) * ra + pltpu.roll(blk, HEAD_LANES - 16, 1) * rb) * qscale
            lo = c * INPROJ_SUB + k * HEAD_LANES
            o_ref[0, :, lo:lo + HEAD_LANES] = blk.astype(o_ref.dtype)

    pending = None
    for c in range(w_ref.shape[1] // INPROJ_SUB):
        acc = jnp.dot(h, w_ref[:, c * INPROJ_SUB:(c + 1) * INPROJ_SUB], preferred_element_type=F32)
        if pending is not None:
            finish(*pending)
        pending = (c, acc)
    finish(*pending)


def _inproj(xa, modtab, w_bf, tables, *, col0, ncols, out_dtype, rope_tiles, m_ctx):
    b, l, d = xa.shape
    tn = 1024
    tm = l // 8
    joff = col0 // tn
    kern = functools.partial(_inproj_kernel, m_ctx=m_ctx, rope_tiles=rope_tiles)
    tab_spec = pl.BlockSpec((1, tm, HEAD_LANES), lambda bb, i, j: (jnp.where(j < rope_tiles, 0, 1), i, 0))
    return pl.pallas_call(
        kern,
        grid=(b, l // tm, ncols // tn),
        in_specs=[pl.BlockSpec((1, tm, d), lambda bb, i, j: (bb, i, 0)),
                  pl.BlockSpec((1, 2, 6, d), lambda bb, i, j: (bb, 0, 0, 0)),
                  pl.BlockSpec((d, tn), lambda bb, i, j: (0, j + joff)),
                  tab_spec, tab_spec, tab_spec],
        out_specs=pl.BlockSpec((1, tm, tn), lambda bb, i, j: (bb, i, j)),
        out_shape=jax.ShapeDtypeStruct((b, l, ncols), out_dtype),
        scratch_shapes=[pltpu.VMEM((tm, d), BF16)],
        compiler_params=_cparams(("arbitrary", "arbitrary", "arbitrary")),
        name="inproj",
    )(xa, modtab, w_bf, *tables)


def _rope_tables(m_ctx, n_lat):
    nf = ATT_HEAD_DIM // 4
    inv = ROPE_THETA ** (-jnp.arange(nf, dtype=F32) / nf)
    rows = n_lat // GRID_W
    row = jnp.repeat(jnp.arange(rows, dtype=F32), GRID_W)
    col = jnp.tile(jnp.arange(GRID_W, dtype=F32), rows)
    ar, ac = row[:, None] * inv, col[:, None] * inv
    cr, sr, cc, sc = jnp.cos(ar), jnp.sin(ar), jnp.cos(ac), jnp.sin(ac)
    z = jnp.zeros_like(sr)
    c64 = jnp.concatenate([cr, cr, cc, cc], axis=1)
    a64 = jnp.concatenate([z, sr, z, sc], axis=1)
    b64 = jnp.concatenate([-sr, z, -sc, z], axis=1)
    reps = HEAD_LANES // ATT_HEAD_DIM

    def full(t64, fill):
        lat = jnp.tile(t64, (1, reps))
        rot = jnp.concatenate([jnp.full((m_ctx, HEAD_LANES), fill, F32), lat], axis=0)
        return jnp.stack([rot, jnp.full(rot.shape, fill, F32)])

    return full(c64, 1.0), full(a64, 0.0), full(b64, 0.0)


def _attn_kernel(lq_ref, sw_ref, q_ref, k_ref, v_ref, o_ref, vt_scr, m0_scr, l0_scr, acc0_scr, m1_scr, l1_scr,
                 acc1_scr, sa0_scr, sa1_scr, sb0_scr, sb1_scr, *, m_ctx, tk, lambda_init):
    i = pl.program_id(2)
    tq = q_ref.shape[1]
    n_lat = k_ref.shape[1] - m_ctx

    @pl.when(i == 0)
    def _():
        vt_scr[...] = v_ref[0].T

    q = q_ref[0]
    lane = lax.broadcasted_iota(jnp.int32, q.shape, 1)
    zero = jnp.zeros_like(q)
    qs = (jnp.where(lane < ATT_HEAD_DIM, q, zero), jnp.where(lane >= ATT_HEAD_DIM, q, zero))
    stats = ((m0_scr, l0_scr, acc0_scr), (m1_scr, l1_scr, acc1_scr))
    for m_scr, l_scr, acc_scr in stats:
        m_scr[...] = jnp.full(m_scr.shape, -jnp.inf, F32)
        l_scr[...] = jnp.zeros(l_scr.shape, F32)
        acc_scr[...] = jnp.zeros(acc_scr.shape, F32)

    def scores(start, size, st_refs):
        kc = k_ref[0, pl.ds(start, size), :]
        cmax = []
        for s in range(2):
            st = lax.dot_general(kc, qs[s], (((1,), (1,)), ((), ())), preferred_element_type=F32)
            st_refs[s][0:size, :] = st
            cmax.append(jnp.max(st, axis=0, keepdims=True))
        return tuple(cmax)

    def accumulate(start, size, st_refs, cmax):
        vt = vt_scr[:, pl.ds(start, size)]
        for s, (m_scr, l_scr, acc_scr) in enumerate(stats):
            m_old = m_scr[...]
            m_new = jnp.maximum(m_old, cmax[s])
            alpha = jnp.exp2(m_old - m_new)
            p = jnp.exp2(st_refs[s][0:size, :] - m_new)
            l_scr[...] = alpha * l_scr[...] + jnp.sum(p, axis=0, keepdims=True)
            acc_scr[...] = alpha * acc_scr[...] + jnp.dot(vt, p.astype(BF16), preferred_element_type=F32)
            m_scr[...] = m_new

    st_a, st_b = (sa0_scr, sa1_scr), (sb0_scr, sb1_scr)

    @pl.when(i * tq < m_ctx)
    def _():
        accumulate(0, m_ctx, st_a, scores(0, m_ctx, st_a))

    @pl.when(i * tq >= m_ctx)
    def _():
        def at(c):
            return pl.multiple_of(m_ctx + c * tk, math.gcd(m_ctx, tk))
        nck = n_lat // tk
        bufs = (st_b, st_a)

        def run(first, count, cm):
            for u in range(count):
                c = first + u
                last = isinstance(c, int) and c + 1 >= nck
                cm_next = None if last else scores(at(c + 1), tk, bufs[(u + 1) % 2])
                accumulate(at(c), tk, bufs[u % 2], cm)
                cm = cm_next
            return cm

        cm_ctx = scores(0, m_ctx, st_a)
        cm = scores(at(0), tk, st_b)
        accumulate(0, m_ctx, st_a, cm_ctx)
        trips = (nck - 1) // ATTN_UNROLL
        cm = lax.fori_loop(0, trips, lambda j, cm: run(j * ATTN_UNROLL, ATTN_UNROLL, cm), cm)
        run(trips * ATTN_UNROLL, nck - trips * ATTN_UNROLL, cm)

    lq = lq_ref[...]
    lam = (jnp.exp(jnp.sum(lq[0:1] * lq[1:2], axis=1, keepdims=True))
           - jnp.exp(jnp.sum(lq[2:3] * lq[3:4], axis=1, keepdims=True)) + lambda_init)
    ot = acc0_scr[...] / l0_scr[...] - lam * (acc1_scr[...] / l1_scr[...])
    ms = jnp.mean(ot * ot, axis=0, keepdims=True)
    o = (ot * lax.rsqrt(ms + EPS)).T * sw_ref[...] * (1.0 - lambda_init)
    o_ref[0] = o.astype(o_ref.dtype)


def _attention(qkv, lambda_qk, subln_w, *, m_ctx, lambda_init):
    b, l, w3 = qkv.shape
    heads = w3 // (3 * HEAD_LANES)
    tq = TILE
    tk = 512
    kern = functools.partial(_attn_kernel, m_ctx=m_ctx, tk=tk, lambda_init=lambda_init)
    return pl.pallas_call(
        kern,
        grid=(b, heads, l // tq),
        in_specs=[pl.BlockSpec((4, ATT_HEAD_DIM), lambda bb, h, i: (0, 0)),
                  pl.BlockSpec((1, HEAD_LANES), lambda bb, h, i: (0, 0)),
                  pl.BlockSpec((1, tq, HEAD_LANES), lambda bb, h, i: (bb, i, h)),
                  pl.BlockSpec((1, l, HEAD_LANES), lambda bb, h, i: (bb, 0, heads + h)),
                  pl.BlockSpec((1, l, HEAD_LANES), lambda bb, h, i: (bb, 0, 2 * heads + h))],
        out_specs=pl.BlockSpec((1, tq, HEAD_LANES), lambda bb, h, i: (bb, i, h)),
        out_shape=jax.ShapeDtypeStruct((b, l, heads * HEAD_LANES), BF16),
        scratch_shapes=[pltpu.VMEM((HEAD_LANES, l), BF16)]
        + [pltpu.VMEM((1, tq), F32), pltpu.VMEM((1, tq), F32), pltpu.VMEM((HEAD_LANES, tq), F32)] * 2
        + [pltpu.VMEM((max(tk, m_ctx), tq), F32)] * 4,
        compiler_params=_cparams(("arbitrary", "arbitrary", "arbitrary")),
        name="diff_attention",
    )(lambda_qk, subln_w.reshape(1, HEAD_LANES), qkv, qkv, qkv)


def _log_sigmoid(z):
    return jnp.minimum(z, 0.0) - jnp.log1p(jnp.exp(-jnp.abs(z)))


def _forget_gate(z, lb):
    ls = _log_sigmoid(z)
    key = jax.nn.sigmoid(-z)
    if lb is None:
        return ls, key
    a = jnp.log(lb)
    b = jnp.log1p(-lb) + ls
    logf = jnp.maximum(a, b) + jnp.log1p(jnp.exp(-jnp.abs(a - b)))
    return logf, (1.0 - lb) * key


def _hgrn_tile(dirs):
    ch = HGRN_CHUNK
    nt_dims = (((1,), (1,)), ((), ()))
    work = []
    for q_ref, v_ref, z_ref, o_ref, s_scr, lb, reverse in dirs:
        t = q_ref.shape[1]
        logf, key = _forget_gate(z_ref[0], lb)
        r = lax.broadcasted_iota(jnp.int32, (t, t), 0)
        c = lax.broadcasted_iota(jnp.int32, (t, t), 1)
        tri = ((r // ch) == (c // ch)) & ((c >= r) if reverse else (c <= r))
        a = jnp.dot(tri.astype(F32), logf, precision=HIGHEST, preferred_element_type=F32)
        nch = t // ch

        def per_chunk(row_of):
            return jnp.concatenate([jnp.broadcast_to(row_of(a[k * ch:(k + 1) * ch]), (ch, a.shape[1]))
                                    for k in range(nch)], axis=0)
        a_mid = per_chunk(lambda ac: ac[ch // 2 - 1:ch // 2])
        a_end = per_chunk((lambda ac: ac[0:1]) if reverse else (lambda ac: ac[ch - 1:ch]))
        work.append(dict(q=q_ref[0], v32=v_ref[0], v=v_ref[0].astype(BF16), key=key, logf=logf, a=a, a_mid=a_mid,
                         a_end=a_end, tri=tri, o_ref=o_ref, s_scr=s_scr, reverse=reverse, nch=nch))

    span = functools.reduce(jnp.maximum, [jnp.max(jnp.abs(w["a"] - w["a_mid"])) for w in work])
    safe = span <= HGRN_SAFE_EXP

    @pl.when(safe)
    def _():
        _hgrn_tile_chunked(work)

    @pl.when(jnp.logical_not(safe))
    def _():
        for w in work:
            _hgrn_tile_stepwise(w)


def _hgrn_tile_stepwise(w):
    t, hd = w["q"].shape
    q_t, k_t, f_t, v_t = w["q"].T, w["key"].T, jnp.exp(w["logf"]).T, w["v32"].T
    lane = lax.broadcasted_iota(jnp.int32, (hd, t), 1)
    eye = lax.broadcasted_iota(jnp.int32, (hd, hd), 0) == lax.broadcasted_iota(jnp.int32, (hd, hd), 1)

    def column(x_t, r):
        return jnp.sum(jnp.where(lane == r, x_t, 0.0), axis=1, keepdims=True)

    def as_row(col):
        return jnp.sum(jnp.where(eye, col, 0.0), axis=0, keepdims=True)

    def body(n, carry):
        st, o_t = carry
        r = t - 1 - n if w["reverse"] else n
        st = as_row(column(f_t, r)) * st + column(v_t, r) * as_row(column(k_t, r))
        o_col = jnp.sum(st * as_row(column(q_t, r)), axis=1, keepdims=True)
        return st, jnp.where(lane == r, o_col, o_t)

    st, o_t = lax.fori_loop(0, t, body, (w["s_scr"][...], jnp.zeros((hd, t), F32)))
    w["s_scr"][...] = st
    w["o_ref"][0] = o_t.T


def _hgrn_tile_chunked(work):
    ch = HGRN_CHUNK
    nt_dims = (((1,), (1,)), ((), ()))
    for w in work:
        a, nch, a_mid, a_end = w["a"], w["nch"], w["a_mid"], w["a_end"]
        qe = (w["q"] * jnp.exp(a - a_mid)).astype(BF16)
        ke = (w["key"] * jnp.exp(a_mid - a)).astype(BF16)
        kd = (w["key"] * jnp.exp(a_end - a)).astype(BF16)
        w["qa"] = (w["q"] * jnp.exp(a)).astype(BF16)
        w["decay"] = jnp.exp(a_end)
        w["sc"] = lax.dot_general(qe, ke, nt_dims, preferred_element_type=F32)
        w["kv"] = [lax.dot_general(w["v"][k * ch:(k + 1) * ch], kd[k * ch:(k + 1) * ch], (((0,), (0,)), ((), ())),
                                   preferred_element_type=F32) for k in range(nch)]

    for w in work:
        w["intra"] = jnp.dot(jnp.where(w["tri"], w["sc"], 0.0).astype(BF16), w["v"], preferred_element_type=F32)

    for w in work:
        nch = w["nch"]
        st = w["s_scr"][...]
        states = [None] * nch
        for k in (range(nch - 1, -1, -1) if w["reverse"] else range(nch)):
            states[k] = st.astype(BF16)
            row = k * ch if w["reverse"] else (k + 1) * ch - 1
            st = w["decay"][row:row + 1] * st + w["kv"][k]
        w["s_scr"][...] = st
        w["states"] = states

    for w in work:
        inter = [lax.dot_general(w["qa"][k * ch:(k + 1) * ch], w["states"][k], nt_dims, preferred_element_type=F32)
                 for k in range(w["nch"])]
        w["o_ref"][0] = jnp.concatenate(inter, axis=0) + w["intra"]


def _hgrn_kernel(lbl_ref, qf_ref, vf_ref, zf_ref, qb_ref, vb_ref, zb_ref, of_ref, ob_ref, sf_scr, sb_scr, *, layer):
    @pl.when(pl.program_id(2) == 0)
    def _():
        sf_scr[...] = jnp.zeros(sf_scr.shape, F32)
        sb_scr[...] = jnp.zeros(sb_scr.shape, F32)

    if layer == 0:
        lbf = lbb = None
    else:
        lg = lbl_ref[...]
        ex = jnp.exp(lg - jnp.max(lg, axis=1, keepdims=True))
        sm = ex / jnp.sum(ex, axis=1, keepdims=True)
        lb = sm[:, 1]
        for k in range(2, layer + 1):
            lb = lb + sm[:, k]
        lbf, lbb = lb[0:1], lb[1:2]
    _hgrn_tile([(qf_ref, vf_ref, zf_ref, of_ref, sf_scr, lbf, False),
                (qb_ref, vb_ref, zb_ref, ob_ref, sb_scr, lbb, True)])


def _hgrn(cg, lb_logits, *, layer, hg_heads, col_q, col_i, col_ff, col_fb):
    b, l, _ = cg.shape
    t = TILE
    nb = l // t
    depth = lb_logits.shape[1]

    def fwd(col):
        return pl.BlockSpec((1, t, HEAD_LANES), lambda bb, h, i: (bb, i, col + h))

    def bwd(col):
        return pl.BlockSpec((1, t, HEAD_LANES), lambda bb, h, i: (bb, jnp.where(i == 0, 0, nb - i), col + h))

    out_f = pl.BlockSpec((1, t, HEAD_LANES), lambda bb, h, i: (bb, i, h))
    out_b = pl.BlockSpec((1, t, HEAD_LANES), lambda bb, h, i: (bb, jnp.where(i == 0, 0, nb - i), h))
    shp = jax.ShapeDtypeStruct((b, l, hg_heads * HEAD_LANES), F32)
    return pl.pallas_call(
        functools.partial(_hgrn_kernel, layer=layer),
        grid=(b, hg_heads, nb),
        in_specs=[pl.BlockSpec((2, depth, HEAD_LANES), lambda bb, h, i: (0, 0, h)),
                  fwd(col_q), fwd(col_i), fwd(col_ff), bwd(col_q), bwd(col_i), bwd(col_fb)],
        out_specs=[out_f, out_b],
        out_shape=[shp, shp],
        scratch_shapes=[pltpu.VMEM((HEAD_LANES, HEAD_LANES), F32), pltpu.VMEM((HEAD_LANES, HEAD_LANES), F32)],
        compiler_params=_cparams(("arbitrary", "arbitrary", "arbitrary")),
        name="hgrn2",
    )(lb_logits, cg, cg, cg, cg, cg, cg)


def _layer_norm(r, w, b):
    mu = jnp.mean(r, axis=1, keepdims=True)
    var = jnp.mean(jnp.square(r - mu), axis=1, keepdims=True)
    return (r - mu) * lax.rsqrt(var + EPS) * w + b


def _outproj_kernel(x_ref, att_ref, cx_ref, cb_ref, cc_ref, cxp_ref, ccp_ref, cxn_ref, ccn_ref, of_ref, ob_ref,
                    gg_ref, wconv_ref, hnw_ref, wout_ref, mod_ref, lnw_ref, lnb_ref, wr_ref,
                    x1_ref, h2_ref, aff_ref, *, ctx_tiles, alpha, n_experts):
    i = pl.program_id(1)
    nt = pl.num_programs(1)
    tm = x_ref.shape[1]
    aw = att_ref.shape[2]
    cw = cx_ref.shape[2]

    u = cc_ref[0] * cx_ref[0]
    prev_ok = jnp.logical_and(i != 0, i != ctx_tiles)
    next_ok = jnp.logical_and(i != ctx_tiles - 1, i != nt - 1)
    u_before = jnp.where(prev_ok, (ccp_ref[0] * cxp_ref[0])[7:8], 0.0)
    u_after = jnp.where(next_ok, (ccn_ref[0] * cxn_ref[0])[0:1], 0.0)
    row = lax.broadcasted_iota(jnp.int32, (tm, 1), 0)
    u_prev = jnp.where(row == 0, u_before, pltpu.roll(u, 1, 0))
    u_next = jnp.where(row == tm - 1, u_after, pltpu.roll(u, tm - 1, 0))
    wc = wconv_ref[...]
    conv = cb_ref[0] * (u_prev * wc[0:1] + u * wc[1:2] + u_next * wc[2:3])

    o = of_ref[0] + ob_ref[0]
    gg = gg_ref[0]
    recs = []
    for h in range(o.shape[1] // HEAD_LANES):
        oh = o[:, h * HEAD_LANES:(h + 1) * HEAD_LANES]
        ms = jnp.mean(oh * oh, axis=1, keepdims=True)
        recs.append(oh * lax.rsqrt(ms + EPS) * hnw_ref[...] * _silu(gg[:, h * HEAD_LANES:(h + 1) * HEAD_LANES]))
    rec = jnp.concatenate(recs, axis=1)

    y = jnp.dot(att_ref[0], wout_ref[0:aw, :], preferred_element_type=F32)
    y = y + jnp.dot(conv.astype(BF16), wout_ref[aw:aw + cw, :], preferred_element_type=F32)
    y = y + jnp.dot(rec.astype(BF16), wout_ref[aw + cw:, :], preferred_element_type=F32)

    mod = mod_ref[0, 0]
    x1 = _layer_norm(alpha * x_ref[0] + mod[2:3] * y, lnw_ref[...], lnb_ref[...])
    x1_ref[0] = x1
    h2 = x1 * (1.0 + mod[4:5]) + mod[3:4]
    d = h2.shape[1]
    h2_ref[0, :, :d] = h2
    wr = wr_ref[...]
    wr_hi = wr.astype(BF16)
    wr_lo = (wr - wr_hi.astype(F32)).astype(BF16)
    h2_hi = h2.astype(BF16)
    h2_lo = (h2 - h2_hi.astype(F32)).astype(BF16)
    logits = (jnp.dot(h2_hi, wr_hi, preferred_element_type=F32) + jnp.dot(h2_hi, wr_lo, preferred_element_type=F32)
              + jnp.dot(h2_lo, wr_hi, preferred_element_type=F32))
    lane = lax.broadcasted_iota(jnp.int32, logits.shape, 1)
    logits = jnp.where(lane < n_experts, logits, -jnp.inf)
    ex = jnp.exp(logits - jnp.max(logits, axis=1, keepdims=True))
    aff = ex / jnp.sum(ex, axis=1, keepdims=True)
    aff_ref[0] = aff[:, :n_experts]
    h2_ref[0, :, d:] = aff


def _outproj(xa, att, cg, o_f, o_b, w_conv, hgrn_norm_w, w_out_bf, modtab, ln_w, ln_b, w_router_pad, *,
             m_ctx, alpha, n_experts, cw, col_gg):
    b, l, d = xa.shape
    tm = TILE
    aw = att.shape[2]
    hw = o_f.shape[2]
    r8 = tm // 8
    last8 = l // 8 - 1
    ctx_tiles = m_ctx // tm

    def rows(width, col):
        return pl.BlockSpec((1, tm, width), lambda bb, i: (bb, i, col))

    def halo_prev(col):
        return pl.BlockSpec((1, 8, cw), lambda bb, i: (bb, jnp.maximum(i * r8 - 1, 0), col))

    def halo_next(col):
        return pl.BlockSpec((1, 8, cw), lambda bb, i: (bb, jnp.minimum((i + 1) * r8, last8), col))

    def const(shape):
        return pl.BlockSpec(shape, lambda bb, i: (0,) * len(shape))

    kern = functools.partial(_outproj_kernel, ctx_tiles=ctx_tiles, alpha=alpha, n_experts=n_experts)
    return pl.pallas_call(
        kern,
        grid=(b, l // tm),
        in_specs=[rows(d, 0), rows(aw, 0),
                  rows(cw, 0), rows(cw, 1), rows(cw, 2),
                  halo_prev(0), halo_prev(2), halo_next(0), halo_next(2),
                  rows(hw, 0), rows(hw, 0), rows(hw, col_gg),
                  const((3, cw)), const((1, HEAD_LANES)), const((d, d)),
                  pl.BlockSpec((1, 1, 6, d), lambda bb, i: (bb, jnp.minimum(i // ctx_tiles, 1), 0, 0)),
                  const((1, d)), const((1, d)), const((d, HEAD_LANES))],
        out_specs=[rows(d, 0), rows(d + HEAD_LANES, 0), rows(n_experts, 0)],
        out_shape=[jax.ShapeDtypeStruct((b, l, d), F32), jax.ShapeDtypeStruct((b, l, d + HEAD_LANES), F32),
                   jax.ShapeDtypeStruct((b, l, n_experts), F32)],
        compiler_params=_cparams(("arbitrary", "arbitrary")),
        name="outproj_ln_router",
    )(xa, att, cg, cg, cg, cg, cg, cg, cg, o_f, o_b, cg, w_conv, hgrn_norm_w.reshape(1, HEAD_LANES), w_out_bf,
      modtab, ln_w.reshape(1, d), ln_b.reshape(1, d), w_router_pad)


def _topk_kernel(aff_ref, pos_ref, off_ref, idx_ref, cnt_scr, off_smem, dsem, *, m_ctx, cap_ctx, cap_lat):
    aff = aff_ref[0]
    ne, l = aff.shape
    ch = COMBINE_TILE
    nch = l // ch
    ctx_ch = m_ctx // ch
    bits = pltpu.bitcast(aff, jnp.int32)
    lane = lax.broadcasted_iota(jnp.int32, (ne, l), 1)
    in_ctx = lane < m_ctx

    def kth_largest(seg, k):
        def body(_, lohi):
            lo, hi = lohi
            mid = lo + lax.shift_right_logical(hi - lo, 1)
            cnt = jnp.sum(jnp.where(jnp.logical_and(seg, bits >= mid), 1.0, 0.0), axis=1, keepdims=True)
            ge = cnt >= k
            return jnp.where(ge, mid, lo), jnp.where(ge, hi, mid)
        lo0 = jnp.zeros((ne, 1), jnp.int32)
        hi0 = jnp.full((ne, 1), 0x7F800000, jnp.int32)
        return lax.fori_loop(0, 31, body, (lo0, hi0))[0]

    thr = jnp.where(in_ctx, kth_largest(in_ctx, cap_ctx), kth_largest(jnp.logical_not(in_ctx), cap_lat))
    gt = bits > thr
    eq = bits == thr
    gtf = jnp.where(gt, 1.0, 0.0)
    n_gt_ctx = jnp.sum(jnp.where(in_ctx, gtf, 0.0), axis=1, keepdims=True)
    n_gt_lat = jnp.sum(jnp.where(in_ctx, 0.0, gtf), axis=1, keepdims=True)
    need = jnp.where(in_ctx, cap_ctx - n_gt_ctx, cap_lat - n_gt_lat)

    tr = lax.broadcasted_iota(jnp.int32, (ch, ch), 0)
    tc = lax.broadcasted_iota(jnp.int32, (ch, ch), 1)
    tri = jnp.where(tr <= tc, 1.0, 0.0).astype(BF16)

    def seg_prefix(flag):
        excl, bases = [], []
        base = jnp.zeros((ne, 1), F32)
        for c in range(nch):
            if c == ctx_ch:
                base = jnp.zeros((ne, 1), F32)
            fc = flag[:, c * ch:(c + 1) * ch]
            incl = jnp.dot(fc.astype(BF16), tri, preferred_element_type=F32)
            excl.append(base + incl - fc)
            bases.append(base)
            base = base + incl[:, ch - 1:ch]
        bases.append(base)
        return jnp.concatenate(excl, axis=1), bases

    eq_excl, _ = seg_prefix(jnp.where(eq, 1.0, 0.0))
    sel = jnp.logical_or(gt, jnp.logical_and(eq, eq_excl < need))
    self_ = jnp.where(sel, 1.0, 0.0)
    sel_excl, bases = seg_prefix(self_)
    seg_off = jnp.where(in_ctx, 0.0, float(cap_ctx))
    pos_ref[0] = jnp.where(sel, sel_excl + seg_off, -1.0).astype(jnp.int32)

    olane = lax.broadcasted_iota(jnp.int32, (ne, HEAD_LANES), 1)
    off = jnp.zeros((ne, HEAD_LANES), F32)
    for c in range(nch + 1):
        off = jnp.where(olane == c, bases[c] + (0.0 if c < ctx_ch else float(cap_ctx)), off)
    off_ref[0] = off.astype(jnp.int32)

    cnt = sel_excl + self_ + seg_off
    for e in range(ne):
        cnt_scr[e] = cnt[e:e + 1]
    n_slots = idx_ref.shape[1]
    sb = n_slots // IDX_SLOT_BLOCKS
    ilane = lax.broadcasted_iota(jnp.int32, (sb, HEAD_LANES), 1)
    off_copy = pltpu.make_async_copy(off_ref.at[0], off_smem, dsem.at[0])
    off_copy.start()
    off_copy.wait()
    for blk in range(IDX_SLOT_BLOCKS):
        slot = (lax.broadcasted_iota(jnp.int32, (sb, HEAD_LANES), 0) + blk * sb).astype(F32)
        p_lo, p_hi = blk * sb, blk * sb + sb - 1

        def per_expert(e, out):
            def scan(c, lohi):
                return (lohi[0] + jnp.where(off_smem[e, c + 1] <= p_lo, 1, 0),
                        lohi[1] + jnp.where(off_smem[e, c] <= p_hi, 1, 0))
            c_lo, c_hi = lax.fori_loop(0, nch, scan, (jnp.int32(0), jnp.int32(0)))

            def per_chunk(c, acc):
                cnt_row = cnt_scr[e, :, pl.ds(pl.multiple_of(c * ch, ch), ch)]
                return acc + jnp.where(cnt_row <= slot, 1.0, 0.0)
            acc = lax.fori_loop(c_lo, c_hi, per_chunk, jnp.zeros((sb, ch), F32))
            full = (c_lo * ch).astype(F32)
            return jnp.where(ilane == e, jnp.sum(acc, axis=1, keepdims=True) + full, out)
        out = lax.fori_loop(0, ne, per_expert, jnp.zeros((sb, HEAD_LANES), F32))
        idx_ref[0, blk * sb:(blk + 1) * sb, :] = out.astype(jnp.int32)


def _topk(aff_t, *, m_ctx, cap_ctx, cap_lat):
    b, ne, l = aff_t.shape
    n_slots = cap_ctx + cap_lat
    kern = functools.partial(_topk_kernel, m_ctx=m_ctx, cap_ctx=cap_ctx, cap_lat=cap_lat)
    return pl.pallas_call(
        kern,
        grid=(b,),
        in_specs=[pl.BlockSpec((1, ne, l), lambda bb: (bb, 0, 0))],
        out_specs=[pl.BlockSpec((1, ne, l), lambda bb: (bb, 0, 0)),
                   pl.BlockSpec((1, ne, HEAD_LANES), lambda bb: (bb, 0, 0)),
                   pl.BlockSpec((1, n_slots, HEAD_LANES), lambda bb: (bb, 0, 0))],
        out_shape=[jax.ShapeDtypeStruct((b, ne, l), jnp.int32),
                   jax.ShapeDtypeStruct((b, ne, HEAD_LANES), jnp.int32),
                   jax.ShapeDtypeStruct((b, n_slots, HEAD_LANES), jnp.int32)],
        scratch_shapes=[pltpu.VMEM((ne, 1, l), F32), pltpu.SMEM((ne, HEAD_LANES), jnp.int32),
                        pltpu.SemaphoreType.DMA((1,))],
        compiler_params=_cparams(("arbitrary",)),
        name="expert_choice_topk",
    )(aff_t)


def _ffn_kernel(idx_ref, h_hbm, wg_ref, wu_ref, wd_ref, y_ref, xg_scr, xb_scr, gate_scr, acc_scr, sem, *, n_experts,
                steps):
    e, b, f = pl.program_id(0), pl.program_id(1), pl.program_id(2)
    nb, nf = pl.num_programs(1), pl.num_programs(2)
    n_slots, d = xb_scr.shape

    def row_copy(bb, ee, r):
        tok = idx_ref[(bb * n_experts + ee) * n_slots + r]
        return pltpu.make_async_copy(h_hbm.at[bb, pl.ds(tok, 1)], xg_scr.at[pl.ds(r, 1)], sem.at[0])

    @pl.when(f == 0)
    def _():
        @pl.when(jnp.logical_and(e == 0, b == 0))
        def _():
            def issue(r, carry):
                row_copy(b, e, r).start()
                return carry
            lax.fori_loop(0, n_slots, issue, 0, unroll=8)
        pltpu.make_async_copy(h_hbm.at[b, pl.ds(0, n_slots)], xg_scr, sem.at[0]).wait()
        xb_scr[...] = xg_scr[:, :d].astype(BF16)
        tail = xg_scr[:, d:]
        lane = lax.broadcasted_iota(jnp.int32, tail.shape, 1)
        gate_scr[...] = jnp.sum(jnp.where(lane == e, tail, 0.0), axis=1, keepdims=True)
        acc_scr[...] = jnp.zeros(acc_scr.shape, F32)

    wrap = b + 1 == nb
    nxt_b = jnp.where(wrap, 0, b + 1)
    nxt_e = jnp.where(wrap, jnp.where(e + 1 == n_experts, 0, e + 1), e)
    share = n_slots // steps
    for r in range(share):
        row_copy(nxt_b, nxt_e, f * share + r).start()

    wg, wu, wd = wg_ref[0, 0].astype(BF16), wu_ref[0, 0].astype(BF16), wd_ref[0, 0].astype(BF16)
    half = n_slots // 2
    rows = [slice(0, half), slice(half, n_slots)]
    au = [(jnp.dot(xb_scr[r, :], wg, preferred_element_type=F32), jnp.dot(xb_scr[r, :], wu, preferred_element_type=F32))
          for r in rows]
    down = [jnp.dot((_silu(a) * u).astype(BF16), wd, preferred_element_type=F32) for a, u in au]
    for r, dn in zip(rows, down):
        acc_scr[r, :] += dn

    @pl.when(f == nf - 1)
    def _():
        y_ref[0, 0] = (acc_scr[...] * gate_scr[...]).astype(y_ref.dtype)

        @pl.when(jnp.logical_and(e == n_experts - 1, wrap))
        def _():
            pltpu.make_async_copy(h_hbm.at[b, pl.ds(0, n_slots)], xg_scr, sem.at[0]).wait()


def _expert_ffn(idx_flat, h2, w_gate, w_up, w_down, *, n_slots, layer):
    b, l, dx = h2.shape
    _, ne, d, ff = w_gate.shape
    tf = 256
    grid_spec = pltpu.PrefetchScalarGridSpec(
        num_scalar_prefetch=1,
        grid=(ne, b, ff // tf),
        in_specs=[pl.BlockSpec(memory_space=pl.ANY),
                  pl.BlockSpec((1, 1, d, tf), lambda e, bb, f, idx: (layer, e, 0, f)),
                  pl.BlockSpec((1, 1, d, tf), lambda e, bb, f, idx: (layer, e, 0, f)),
                  pl.BlockSpec((1, 1, tf, d), lambda e, bb, f, idx: (layer, e, f, 0))],
        out_specs=pl.BlockSpec((1, 1, n_slots, d), lambda e, bb, f, idx: (bb, e, 0, 0)),
        scratch_shapes=[pltpu.VMEM((n_slots, dx), F32), pltpu.VMEM((n_slots, d), BF16),
                        pltpu.VMEM((n_slots, 1), F32), pltpu.VMEM((n_slots, d), F32),
                        pltpu.SemaphoreType.DMA((1,))])
    assert ff % tf == 0 and n_slots % (ff // tf) == 0
    return pl.pallas_call(
        functools.partial(_ffn_kernel, n_experts=ne, steps=ff // tf),
        grid_spec=grid_spec,
        out_shape=jax.ShapeDtypeStruct((b, ne, n_slots, d), BF16),
        compiler_params=_cparams(("arbitrary", "arbitrary", "arbitrary")),
        name="expert_ffn",
    )(idx_flat, h2, w_gate, w_up, w_down)


def _combine_kernel(off_ref, x1_ref, pos_ref, mod_ref, lnw_ref, lnb_ref, y_hbm, o_ref, win_scr, sem, *,
                    alpha, n_experts, win):
    b, i = pl.program_id(0), pl.program_id(1)
    nb, nt = pl.num_programs(0), pl.num_programs(1)
    tm = x1_ref.shape[1]
    n_slots = y_hbm.shape[2]
    blk = win_scr.shape[1] // n_experts
    step = b * nt + i
    slot = step % 2

    def window(bb, ii, e):
        base = (bb * n_experts + e) * HEAD_LANES + ii
        start = jnp.minimum((off_ref[base] // WIN_ALIGN) * WIN_ALIGN, n_slots - win)
        return pl.multiple_of(start, WIN_ALIGN), off_ref[base + 1] - start <= WIN_SHORT

    def transfer(bb, ii, sl, go):
        for e in range(n_experts):
            start, short = window(bb, ii, e)
            for size, cond in ((WIN_SHORT, short), (win, jnp.logical_not(short))):
                @pl.when(cond)
                def _():
                    go(pltpu.make_async_copy(y_hbm.at[bb, e, pl.ds(start, size)],
                                             win_scr.at[sl, pl.ds(e * blk, size)], sem.at[sl, e]))

    @pl.when(step == 0)
    def _():
        win_scr[...] = jnp.zeros(win_scr.shape, win_scr.dtype)
        transfer(b, i, slot, lambda cp: cp.start())

    @pl.when(step + 1 < nb * nt)
    def _():
        wrap = i + 1 == nt
        transfer(jnp.where(wrap, b + 1, b), jnp.where(wrap, 0, i + 1), 1 - slot, lambda cp: cp.start())

    pos = pos_ref[0]
    scol = lax.broadcasted_iota(jnp.int32, (tm, blk), 1)
    onehot = jnp.concatenate([jnp.where((pos[:, e:e + 1] - window(b, i, e)[0]) == scol, 1.0, 0.0).astype(BF16)
                              for e in range(n_experts)], axis=1)
    transfer(b, i, slot, lambda cp: cp.wait())
    moe = jnp.dot(onehot, win_scr[slot], preferred_element_type=F32)
    mod = mod_ref[0, 0]
    o_ref[0] = _layer_norm(alpha * x1_ref[0] + mod[5:6] * moe, lnw_ref[...], lnb_ref[...])


def _combine(off_flat, x1, pos_tok, modtab, ln_w, ln_b, y, *, m_ctx, alpha, latent_only):
    b, l, d = x1.shape
    ne = pos_tok.shape[2]
    tm = COMBINE_TILE
    win = tm + WIN_ALIGN
    blk = -(-win // HEAD_LANES) * HEAD_LANES
    ctx_tiles = m_ctx // tm

    def rows(width):
        return pl.BlockSpec((1, tm, width), lambda bb, i, off: (bb, i, 0))

    out_rows = l - m_ctx if latent_only else l
    out_spec = (pl.BlockSpec((1, tm, d), lambda bb, i, off: (bb, jnp.maximum(i - ctx_tiles, 0), 0))
                if latent_only else rows(d))

    grid_spec = pltpu.PrefetchScalarGridSpec(
        num_scalar_prefetch=1,
        grid=(b, l // tm),
        in_specs=[rows(d), rows(ne),
                  pl.BlockSpec((1, 1, 6, d), lambda bb, i, off: (bb, jnp.minimum(i // ctx_tiles, 1), 0, 0)),
                  pl.BlockSpec((1, d), lambda bb, i, off: (0, 0)),
                  pl.BlockSpec((1, d), lambda bb, i, off: (0, 0)),
                  pl.BlockSpec(memory_space=pl.ANY)],
        out_specs=out_spec,
        scratch_shapes=[pltpu.VMEM((2, ne * blk, d), BF16), pltpu.SemaphoreType.DMA((2, ne))])
    return pl.pallas_call(
        functools.partial(_combine_kernel, alpha=alpha, n_experts=ne, win=win),
        grid_spec=grid_spec,
        out_shape=jax.ShapeDtypeStruct((b, out_rows, d), F32),
        compiler_params=_cparams(("arbitrary", "arbitrary")),
        name="moe_combine_ln",
    )(off_flat, x1, pos_tok, modtab, ln_w.reshape(1, d), ln_b.reshape(1, d), y)


def kernel(x, c, ctx, c_ctx, w_mod, b_mod, w_in, w_conv, lambda_qk, subln_w, hgrn_lb_logits, hgrn_norm_w, w_out,
           ln_w, ln_b, w_router, w_gate, w_up, w_down):
    bsz, n_lat, d = x.shape
    m_ctx = ctx.shape[1]
    depth = w_mod.shape[0]
    ne = w_router.shape[2]
    aw, cw, hw = d // 2, d // 4, d // 4
    assert m_ctx % TILE == 0 and n_lat % 1024 == 0 and n_lat % GRID_W == 0 and bsz + 1 <= 8
    assert (m_ctx + n_lat) % (8 * 16) == 0 and aw == 1024 and ne <= HEAD_LANES
    alpha = (2.0 * depth) ** 0.25
    cap_ctx = EC_CAPACITY_FACTOR * m_ctx // ne
    cap_lat = EC_CAPACITY_FACTOR * n_lat // ne
    n_slots = cap_ctx + cap_lat
    assert n_slots % WIN_ALIGN == 0 and n_slots >= COMBINE_TILE + WIN_ALIGN
    assert (m_ctx + n_lat) // COMBINE_TILE + 1 <= HEAD_LANES

    cs = jnp.zeros((8, d), F32).at[:bsz].set(c).at[bsz].set(c_ctx)
    mod = _modulation(cs, w_mod, b_mod)
    mod_lat = mod[:, :bsz].reshape(depth, bsz, 1, 6, d)
    mod_ctx = jnp.broadcast_to(mod[:, bsz].reshape(depth, 1, 1, 6, d), (depth, bsz, 1, 6, d))
    modtab = jnp.concatenate([mod_ctx, mod_lat], axis=2)

    tables = _rope_tables(m_ctx, n_lat)
    xa = jnp.concatenate([ctx, x], axis=1)
    w_router_pad = jnp.zeros((depth, d, HEAD_LANES), F32).at[:, :, :ne].set(w_router)
    hg_heads = hw // HEAD_LANES
    cb0 = 3 * cw // HEAD_LANES

    for l in range(depth):
        lambda_init = 0.8 - 0.6 * math.exp(-0.3 * l)
        w_in_bf = w_in[l].astype(BF16)
        qkv = _inproj(xa, modtab[l], w_in_bf, tables, col0=0, ncols=3 * aw, out_dtype=BF16, rope_tiles=2,
                      m_ctx=m_ctx)
        cg = _inproj(xa, modtab[l], w_in_bf, tables, col0=3 * aw, ncols=3 * cw + 5 * hw, out_dtype=F32,
                     rope_tiles=0, m_ctx=m_ctx)
        att = _attention(qkv, lambda_qk[l], subln_w[l], m_ctx=m_ctx, lambda_init=lambda_init)
        o_f, o_b = _hgrn(cg, hgrn_lb_logits, layer=l, hg_heads=hg_heads, col_q=cb0, col_i=cb0 + hg_heads,
                         col_ff=cb0 + 3 * hg_heads, col_fb=cb0 + 4 * hg_heads)
        x1, h2, aff = _outproj(xa, att, cg, o_f, o_b, w_conv[l], hgrn_norm_w[l], w_out[l].astype(BF16), modtab[l],
                               ln_w[l, 0], ln_b[l, 0], w_router_pad[l], m_ctx=m_ctx, alpha=alpha, n_experts=ne,
                               cw=cw, col_gg=(3 * cw + 2 * hw) // hw)
        pos, off, idx = _topk(jnp.swapaxes(aff, 1, 2), m_ctx=m_ctx, cap_ctx=cap_ctx, cap_lat=cap_lat)
        idx_flat = jnp.minimum(jnp.swapaxes(idx[:, :, :ne], 1, 2).reshape(-1), m_ctx + n_lat - 1)
        y = _expert_ffn(idx_flat, h2, w_gate, w_up, w_down, n_slots=n_slots, layer=l)
        xa = _combine(off.reshape(-1), x1, jnp.swapaxes(pos, 1, 2), modtab[l], ln_w[l, 1], ln_b[l, 1], y,
                      m_ctx=m_ctx, alpha=alpha, latent_only=l == depth - 1)
    return xa
```

```python
import functools
import math

import jax
import jax.numpy as jnp
from jax import lax
from jax.experimental import pallas as pl
from jax.experimental.pallas import tpu as pltpu

F32 = jnp.float32
BF16 = jnp.bfloat16
HIGHEST = lax.Precision.HIGHEST

GRID_W = 64
ROPE_THETA = 10000.0
ATT_HEAD_DIM = 64
HEAD_LANES = 128
HGRN_CHUNK = 64
HGRN_SAFE_EXP = 80.0
EC_CAPACITY_FACTOR = 2
EPS = 1e-6
TILE = 256
INPROJ_SUB = 256
ATTN_UNROLL = 4
COMBINE_TILE = 128
IDX_SLOT_BLOCKS = 4
WIN_SHORT = 48
WIN_ALIGN = 16
VMEM_LIMIT = 56 * 1024 * 1024


def _cparams(sem):
    return pltpu.CompilerParams(dimension_semantics=sem, vmem_limit_bytes=VMEM_LIMIT)


def _silu(x):
    return x * jax.nn.sigmoid(x)


def _mod_kernel(cs_ref, w_ref, b_ref, o_ref):
    a = _silu(cs_ref[...])
    o_ref[0] = jnp.dot(a, w_ref[0], precision=HIGHEST, preferred_element_type=F32) + b_ref[0]


def _modulation(cs, w_mod, b_mod):
    depth, d, n6 = w_mod.shape
    tn = 1024
    return pl.pallas_call(
        _mod_kernel,
        grid=(depth, n6 // tn),
        in_specs=[pl.BlockSpec((8, d), lambda l, j: (0, 0)),
                  pl.BlockSpec((1, d, tn), lambda l, j: (l, 0, j)),
                  pl.BlockSpec((1, 1, tn), lambda l, j: (l, 0, j))],
        out_specs=pl.BlockSpec((1, 8, tn), lambda l, j: (l, 0, j)),
        out_shape=jax.ShapeDtypeStruct((depth, 8, n6), F32),
        compiler_params=_cparams(("arbitrary", "arbitrary")),
        name="modulation",
    )(cs, w_mod, b_mod.reshape(depth, 1, n6))


def _inproj_kernel(x_ref, mod_ref, w_ref, rc_ref, ra_ref, rb_ref, o_ref, h_scr, *, m_ctx, rope_tiles):
    i = pl.program_id(1)
    j = pl.program_id(2)
    tm = x_ref.shape[1]

    @pl.when(j == 0)
    def _():
        row = i * tm + lax.broadcasted_iota(jnp.int32, (tm, 1), 0)
        is_ctx = row < m_ctx
        mod = mod_ref[0]
        sh = jnp.where(is_ctx, mod[0, 0:1], mod[1, 0:1])
        sc = jnp.where(is_ctx, mod[0, 1:2], mod[1, 1:2])
        h_scr[...] = (x_ref[0] * (1.0 + sc) + sh).astype(BF16)

    h = h_scr[...]
    if rope_tiles:
        rc, ra, rb = rc_ref[0], ra_ref[0], rb_ref[0]
        qscale = jnp.where(j == 0, ATT_HEAD_DIM ** -0.5 * math.log2(math.e), 1.0).astype(F32)

    def finish(c, acc):
        for k in range(INPROJ_SUB // HEAD_LANES):
            blk = acc[:, k * HEAD_LANES:(k + 1) * HEAD_LANES]
            if rope_tiles:
                blk = (blk * rc + pltpu.roll(blk, 16, 1) * ra + pltpu.roll(blk, HEAD_LANES - 16, 1) * rb) * qscale
            lo = c * INPROJ_SUB + k * HEAD_LANES
            o_ref[0, :, lo:lo + HEAD_LANES] = blk.astype(o_ref.dtype)

    pending = None
    for c in range(w_ref.shape[1] // INPROJ_SUB):
        acc = jnp.dot(h, w_ref[:, c * INPROJ_SUB:(c + 1) * INPROJ_SUB], preferred_element_type=F32)
        if pending is not None:
            finish(*pending)
        pending = (c, acc)
    finish(*pending)


def _inproj(xa, modtab, w_bf, tables, *, col0, ncols, out_dtype, rope_tiles, m_ctx):
    b, l, d = xa.shape
    tn = 1024
    tm = l // 8
    joff = col0 // tn
    kern = functools.partial(_inproj_kernel, m_ctx=m_ctx, rope_tiles=rope_tiles)
    tab_spec = pl.BlockSpec((1, tm, HEAD_LANES), lambda bb, i, j: (jnp.where(j < rope_tiles, 0, 1), i, 0))
    return pl.pallas_call(
        kern,
        grid=(b, l // tm, ncols // tn),
        in_specs=[pl.BlockSpec((1, tm, d), lambda bb, i, j: (bb, i, 0)),
                  pl.BlockSpec((1, 2, 6, d), lambda bb, i, j: (bb, 0, 0, 0)),
                  pl.BlockSpec((d, tn), lambda bb, i, j: (0, j + joff)),
                  tab_spec, tab_spec, tab_spec],
        out_specs=pl.BlockSpec((1, tm, tn), lambda bb, i, j: (bb, i, j)),
        out_shape=jax.ShapeDtypeStruct((b, l, ncols), out_dtype),
        scratch_shapes=[pltpu.VMEM((tm, d), BF16)],
        compiler_params=_cparams(("arbitrary", "arbitrary", "arbitrary")),
        name="inproj",
    )(xa, modtab, w_bf, *tables)


def _rope_tables(m_ctx, n_lat):
    nf = ATT_HEAD_DIM // 4
    inv = ROPE_THETA ** (-jnp.arange(nf, dtype=F32) / nf)
    rows = n_lat // GRID_W
    row = jnp.repeat(jnp.arange(rows, dtype=F32), GRID_W)
    col = jnp.tile(jnp.arange(GRID_W, dtype=F32), rows)
    ar, ac = row[:, None] * inv, col[:, None] * inv
    cr, sr, cc, sc = jnp.cos(ar), jnp.sin(ar), jnp.cos(ac), jnp.sin(ac)
    z = jnp.zeros_like(sr)
    c64 = jnp.concatenate([cr, cr, cc, cc], axis=1)
    a64 = jnp.concatenate([z, sr, z, sc], axis=1)
    b64 = jnp.concatenate([-sr, z, -sc, z], axis=1)
    reps = HEAD_LANES // ATT_HEAD_DIM

    def full(t64, fill):
        lat = jnp.tile(t64, (1, reps))
        rot = jnp.concatenate([jnp.full((m_ctx, HEAD_LANES), fill, F32), lat], axis=0)
        return jnp.stack([rot, jnp.full(rot.shape, fill, F32)])

    return full(c64, 1.0), full(a64, 0.0), full(b64, 0.0)


def _attn_kernel(lq_ref, sw_ref, q_ref, k_ref, v_ref, o_ref, vt_scr, m0_scr, l0_scr, acc0_scr, m1_scr, l1_scr,
                 acc1_scr, sa0_scr, sa1_scr, sb0_scr, sb1_scr, *, m_ctx, tk, lambda_init):
    i = pl.program_id(2)
    tq = q_ref.shape[1]
    n_lat = k_ref.shape[1] - m_ctx

    @pl.when(i == 0)
    def _():
        vt_scr[...] = v_ref[0].T

    q = q_ref[0]
    lane = lax.broadcasted_iota(jnp.int32, q.shape, 1)
    zero = jnp.zeros_like(q)
    qs = (jnp.where(lane < ATT_HEAD_DIM, q, zero), jnp.where(lane >= ATT_HEAD_DIM, q, zero))
    stats = ((m0_scr, l0_scr, acc0_scr), (m1_scr, l1_scr, acc1_scr))
    for m_scr, l_scr, acc_scr in stats:
        m_scr[...] = jnp.full(m_scr.shape, -jnp.inf, F32)
        l_scr[...] = jnp.zeros(l_scr.shape, F32)
        acc_scr[...] = jnp.zeros(acc_scr.shape, F32)

    def scores(start, size, st_refs):
        kc = k_ref[0, pl.ds(start, size), :]
        cmax = []
        for s in range(2):
            st = lax.dot_general(kc, qs[s], (((1,), (1,)), ((), ())), preferred_element_type=F32)
            st_refs[s][0:size, :] = st
            cmax.append(jnp.max(st, axis=0, keepdims=True))
        return tuple(cmax)

    def accumulate(start, size, st_refs, cmax):
        vt = vt_scr[:, pl.ds(start, size)]
        for s, (m_scr, l_scr, acc_scr) in enumerate(stats):
            m_old = m_scr[...]
            m_new = jnp.maximum(m_old, cmax[s])
            alpha = jnp.exp2(m_old - m_new)
            p = jnp.exp2(st_refs[s][0:size, :] - m_new)
            l_scr[...] = alpha * l_scr[...] + jnp.sum(p, axis=0, keepdims=True)
            acc_scr[...] = alpha * acc_scr[...] + jnp.dot(vt, p.astype(BF16), preferred_element_type=F32)
            m_scr[...] = m_new

    st_a, st_b = (sa0_scr, sa1_scr), (sb0_scr, sb1_scr)

    @pl.when(i * tq < m_ctx)
    def _():
        accumulate(0, m_ctx, st_a, scores(0, m_ctx, st_a))

    @pl.when(i * tq >= m_ctx)
    def _():
        def at(c):
            return pl.multiple_of(m_ctx + c * tk, math.gcd(m_ctx, tk))
        nck = n_lat // tk
        bufs = (st_b, st_a)

        def run(first, count, cm):
            for u in range(count):
                c = first + u
                last = isinstance(c, int) and c + 1 >= nck
                cm_next = None if last else scores(at(c + 1), tk, bufs[(u + 1) % 2])
                accumulate(at(c), tk, bufs[u % 2], cm)
                cm = cm_next
            return cm

        cm_ctx = scores(0, m_ctx, st_a)
        cm = scores(at(0), tk, st_b)
        accumulate(0, m_ctx, st_a, cm_ctx)
        trips = (nck - 1) // ATTN_UNROLL
        cm = lax.fori_loop(0, trips, lambda j, cm: run(j * ATTN_UNROLL, ATTN_UNROLL, cm), cm)
        run(trips * ATTN_UNROLL, nck - trips * ATTN_UNROLL, cm)

    lq = lq_ref[...]
    lam = (jnp.exp(jnp.sum(lq[0:1] * lq[1:2], axis=1, keepdims=True))
           - jnp.exp(jnp.sum(lq[2:3] * lq[3:4], axis=1, keepdims=True)) + lambda_init)
    ot = acc0_scr[...] / l0_scr[...] - lam * (acc1_scr[...] / l1_scr[...])
    ms = jnp.mean(ot * ot, axis=0, keepdims=True)
    o = (ot * lax.rsqrt(ms + EPS)).T * sw_ref[...] * (1.0 - lambda_init)
    o_ref[0] = o.astype(o_ref.dtype)


def _attention(qkv, lambda_qk, subln_w, *, m_ctx, lambda_init):
    b, l, w3 = qkv.shape
    heads = w3 // (3 * HEAD_LANES)
    tq = TILE
    tk = 1024
    kern = functools.partial(_attn_kernel, m_ctx=m_ctx, tk=tk, lambda_init=lambda_init)
    return pl.pallas_call(
        kern,
        grid=(b, heads, l // tq),
        in_specs=[pl.BlockSpec((4, ATT_HEAD_DIM), lambda bb, h, i: (0, 0)),
                  pl.BlockSpec((1, HEAD_LANES), lambda bb, h, i: (0, 0)),
                  pl.BlockSpec((1, tq, HEAD_LANES), lambda bb, h, i: (bb, i, h)),
                  pl.BlockSpec((1, l, HEAD_LANES), lambda bb, h, i: (bb, 0, heads + h)),
                  pl.BlockSpec((1, l, HEAD_LANES), lambda bb, h, i: (bb, 0, 2 * heads + h))],
        out_specs=pl.BlockSpec((1, tq, HEAD_LANES), lambda bb, h, i: (bb, i, h)),
        out_shape=jax.ShapeDtypeStruct((b, l, heads * HEAD_LANES), BF16),
        scratch_shapes=[pltpu.VMEM((HEAD_LANES, l), BF16)]
        + [pltpu.VMEM((1, tq), F32), pltpu.VMEM((1, tq), F32), pltpu.VMEM((HEAD_LANES, tq), F32)] * 2
        + [pltpu.VMEM((max(tk, m_ctx), tq), F32)] * 4,
        compiler_params=_cparams(("arbitrary", "arbitrary", "arbitrary")),
        name="diff_attention",
    )(lambda_qk, subln_w.reshape(1, HEAD_LANES), qkv, qkv, qkv)


def _log_sigmoid(z):
    return jnp.minimum(z, 0.0) - jnp.log1p(jnp.exp(-jnp.abs(z)))


def _forget_gate(z, lb):
    ls = _log_sigmoid(z)
    key = jax.nn.sigmoid(-z)
    if lb is None:
        return ls, key
    a = jnp.log(lb)
    b = jnp.log1p(-lb) + ls
    logf = jnp.maximum(a, b) + jnp.log1p(jnp.exp(-jnp.abs(a - b)))
    return logf, (1.0 - lb) * key


def _hgrn_tile(dirs):
    ch = HGRN_CHUNK
    nt_dims = (((1,), (1,)), ((), ()))
    work = []
    for q_ref, v_ref, z_ref, o_ref, s_scr, lb, reverse in dirs:
        t = q_ref.shape[1]
        logf, key = _forget_gate(z_ref[0], lb)
        r = lax.broadcasted_iota(jnp.int32, (t, t), 0)
        c = lax.broadcasted_iota(jnp.int32, (t, t), 1)
        tri = ((r // ch) == (c // ch)) & ((c >= r) if reverse else (c <= r))
        a = jnp.dot(tri.astype(F32), logf, precision=HIGHEST, preferred_element_type=F32)
        nch = t // ch

        def per_chunk(row_of):
            return jnp.concatenate([jnp.broadcast_to(row_of(a[k * ch:(k + 1) * ch]), (ch, a.shape[1]))
                                    for k in range(nch)], axis=0)
        a_mid = per_chunk(lambda ac: ac[ch // 2 - 1:ch // 2])
        a_end = per_chunk((lambda ac: ac[0:1]) if reverse else (lambda ac: ac[ch - 1:ch]))
        work.append(dict(q=q_ref[0], v32=v_ref[0], v=v_ref[0].astype(BF16), key=key, logf=logf, a=a, a_mid=a_mid,
                         a_end=a_end, tri=tri, o_ref=o_ref, s_scr=s_scr, reverse=reverse, nch=nch))

    ends = [jnp.abs(w["a"][r:r + 1] - w["a"][k * ch + ch // 2 - 1:k * ch + ch // 2])
            for w in work for k in range(w["nch"]) for r in (k * ch, (k + 1) * ch - 1)]
    safe = jnp.max(functools.reduce(jnp.maximum, ends)) <= HGRN_SAFE_EXP

    @pl.when(safe)
    def _():
        _hgrn_tile_chunked(work)

    @pl.when(jnp.logical_not(safe))
    def _():
        for w in work:
            _hgrn_tile_stepwise(w)


def _hgrn_tile_stepwise(w):
    t, hd = w["q"].shape
    q_t, k_t, f_t, v_t = w["q"].T, w["key"].T, jnp.exp(w["logf"]).T, w["v32"].T
    lane = lax.broadcasted_iota(jnp.int32, (hd, t), 1)
    eye = lax.broadcasted_iota(jnp.int32, (hd, hd), 0) == lax.broadcasted_iota(jnp.int32, (hd, hd), 1)

    def column(x_t, r):
        return jnp.sum(jnp.where(lane == r, x_t, 0.0), axis=1, keepdims=True)

    def as_row(col):
        return jnp.sum(jnp.where(eye, col, 0.0), axis=0, keepdims=True)

    def body(n, carry):
        st, o_t = carry
        r = t - 1 - n if w["reverse"] else n
        st = as_row(column(f_t, r)) * st + column(v_t, r) * as_row(column(k_t, r))
        o_col = jnp.sum(st * as_row(column(q_t, r)), axis=1, keepdims=True)
        return st, jnp.where(lane == r, o_col, o_t)

    st, o_t = lax.fori_loop(0, t, body, (w["s_scr"][...], jnp.zeros((hd, t), F32)))
    w["s_scr"][...] = st
    w["o_ref"][0] = o_t.T


def _hgrn_tile_chunked(work):
    ch = HGRN_CHUNK
    nt_dims = (((1,), (1,)), ((), ()))
    for w in work:
        a, nch, a_mid, a_end = w["a"], w["nch"], w["a_mid"], w["a_end"]
        qe = (w["q"] * jnp.exp(a - a_mid)).astype(BF16)
        ke = (w["key"] * jnp.exp(a_mid - a)).astype(BF16)
        kd = (w["key"] * jnp.exp(a_end - a)).astype(BF16)
        w["qa"] = (w["q"] * jnp.exp(a)).astype(BF16)
        w["decay"] = jnp.exp(a_end)
        w["sc"] = lax.dot_general(qe, ke, nt_dims, preferred_element_type=F32)
        w["kv"] = [lax.dot_general(w["v"][k * ch:(k + 1) * ch], kd[k * ch:(k + 1) * ch], (((0,), (0,)), ((), ())),
                                   preferred_element_type=F32) for k in range(nch)]

    for w in work:
        w["intra"] = jnp.dot(jnp.where(w["tri"], w["sc"], 0.0).astype(BF16), w["v"], preferred_element_type=F32)

    for w in work:
        nch = w["nch"]
        st = w["s_scr"][...]
        states = [None] * nch
        for k in (range(nch - 1, -1, -1) if w["reverse"] else range(nch)):
            states[k] = st.astype(BF16)
            row = k * ch if w["reverse"] else (k + 1) * ch - 1
            st = w["decay"][row:row + 1] * st + w["kv"][k]
        w["s_scr"][...] = st
        w["states"] = states

    for w in work:
        inter = [lax.dot_general(w["qa"][k * ch:(k + 1) * ch], w["states"][k], nt_dims, preferred_element_type=F32)
                 for k in range(w["nch"])]
        w["o_ref"][0] = jnp.concatenate(inter, axis=0) + w["intra"]


def _hgrn_kernel(lbl_ref, qf_ref, vf_ref, zf_ref, qb_ref, vb_ref, zb_ref, of_ref, ob_ref, sf_scr, sb_scr, *, layer):
    @pl.when(pl.program_id(2) == 0)
    def _():
        sf_scr[...] = jnp.zeros(sf_scr.shape, F32)
        sb_scr[...] = jnp.zeros(sb_scr.shape, F32)

    if layer == 0:
        lbf = lbb = None
    else:
        lg = lbl_ref[...]
        ex = jnp.exp(lg - jnp.max(lg, axis=1, keepdims=True))
        sm = ex / jnp.sum(ex, axis=1, keepdims=True)
        lb = sm[:, 1]
        for k in range(2, layer + 1):
            lb = lb + sm[:, k]
        lbf, lbb = lb[0:1], lb[1:2]
    _hgrn_tile([(qf_ref, vf_ref, zf_ref, of_ref, sf_scr, lbf, False),
                (qb_ref, vb_ref, zb_ref, ob_ref, sb_scr, lbb, True)])


def _hgrn(cg, lb_logits, *, layer, hg_heads, col_q, col_i, col_ff, col_fb):
    b, l, _ = cg.shape
    t = TILE
    nb = l // t
    depth = lb_logits.shape[1]

    def fwd(col):
        return pl.BlockSpec((1, t, HEAD_LANES), lambda bb, h, i: (bb, i, col + h))

    def bwd(col):
        return pl.BlockSpec((1, t, HEAD_LANES), lambda bb, h, i: (bb, jnp.where(i == 0, 0, nb - i), col + h))

    out_f = pl.BlockSpec((1, t, HEAD_LANES), lambda bb, h, i: (bb, i, h))
    out_b = pl.BlockSpec((1, t, HEAD_LANES), lambda bb, h, i: (bb, jnp.where(i == 0, 0, nb - i), h))
    shp = jax.ShapeDtypeStruct((b, l, hg_heads * HEAD_LANES), F32)
    return pl.pallas_call(
        functools.partial(_hgrn_kernel, layer=layer),
        grid=(b, hg_heads, nb),
        in_specs=[pl.BlockSpec((2, depth, HEAD_LANES), lambda bb, h, i: (0, 0, h)),
                  fwd(col_q), fwd(col_i), fwd(col_ff), bwd(col_q), bwd(col_i), bwd(col_fb)],
        out_specs=[out_f, out_b],
        out_shape=[shp, shp],
        scratch_shapes=[pltpu.VMEM((HEAD_LANES, HEAD_LANES), F32), pltpu.VMEM((HEAD_LANES, HEAD_LANES), F32)],
        compiler_params=_cparams(("arbitrary", "arbitrary", "arbitrary")),
        name="hgrn2",
    )(lb_logits, cg, cg, cg, cg, cg, cg)


def _layer_norm(r, w, b):
    mu = jnp.mean(r, axis=1, keepdims=True)
    var = jnp.mean(jnp.square(r - mu), axis=1, keepdims=True)
    return (r - mu) * lax.rsqrt(var + EPS) * w + b


def _outproj_kernel(x_ref, att_ref, cx_ref, cb_ref, cc_ref, cxp_ref, ccp_ref, cxn_ref, ccn_ref, of_ref, ob_ref,
                    gg_ref, wconv_ref, hnw_ref, wout_ref, mod_ref, lnw_ref, lnb_ref, wr_ref,
                    x1_ref, h2_ref, aff_ref, *, ctx_tiles, alpha, n_experts):
    i = pl.program_id(1)
    nt = pl.num_programs(1)
    tm = x_ref.shape[1]
    aw = att_ref.shape[2]
    cw = cx_ref.shape[2]

    u = cc_ref[0] * cx_ref[0]
    prev_ok = jnp.logical_and(i != 0, i != ctx_tiles)
    next_ok = jnp.logical_and(i != ctx_tiles - 1, i != nt - 1)
    u_before = jnp.where(prev_ok, (ccp_ref[0] * cxp_ref[0])[7:8], 0.0)
    u_after = jnp.where(next_ok, (ccn_ref[0] * cxn_ref[0])[0:1], 0.0)
    row = lax.broadcasted_iota(jnp.int32, (tm, 1), 0)
    u_prev = jnp.where(row == 0, u_before, pltpu.roll(u, 1, 0))
    u_next = jnp.where(row == tm - 1, u_after, pltpu.roll(u, tm - 1, 0))
    wc = wconv_ref[...]
    conv = cb_ref[0] * (u_prev * wc[0:1] + u * wc[1:2] + u_next * wc[2:3])

    o = of_ref[0] + ob_ref[0]
    gg = gg_ref[0]
    recs = []
    for h in range(o.shape[1] // HEAD_LANES):
        oh = o[:, h * HEAD_LANES:(h + 1) * HEAD_LANES]
        ms = jnp.mean(oh * oh, axis=1, keepdims=True)
        recs.append(oh * lax.rsqrt(ms + EPS) * hnw_ref[...] * _silu(gg[:, h * HEAD_LANES:(h + 1) * HEAD_LANES]))
    rec = jnp.concatenate(recs, axis=1)

    y = jnp.dot(att_ref[0], wout_ref[0:aw, :], preferred_element_type=F32)
    y = y + jnp.dot(conv.astype(BF16), wout_ref[aw:aw + cw, :], preferred_element_type=F32)
    y = y + jnp.dot(rec.astype(BF16), wout_ref[aw + cw:, :], preferred_element_type=F32)

    mod = mod_ref[0, 0]
    x1 = _layer_norm(alpha * x_ref[0] + mod[2:3] * y, lnw_ref[...], lnb_ref[...])
    x1_ref[0] = x1
    h2 = x1 * (1.0 + mod[4:5]) + mod[3:4]
    d = h2.shape[1]
    h2_ref[0, :, :d] = h2
    wr = wr_ref[...]
    wr_hi = wr.astype(BF16)
    wr_lo = (wr - wr_hi.astype(F32)).astype(BF16)
    h2_hi = h2.astype(BF16)
    h2_lo = (h2 - h2_hi.astype(F32)).astype(BF16)
    logits = (jnp.dot(h2_hi, wr_hi, preferred_element_type=F32) + jnp.dot(h2_hi, wr_lo, preferred_element_type=F32)
              + jnp.dot(h2_lo, wr_hi, preferred_element_type=F32))
    lane = lax.broadcasted_iota(jnp.int32, logits.shape, 1)
    logits = jnp.where(lane < n_experts, logits, -jnp.inf)
    ex = jnp.exp(logits - jnp.max(logits, axis=1, keepdims=True))
    aff = ex / jnp.sum(ex, axis=1, keepdims=True)
    aff_ref[0] = aff[:, :n_experts]
    h2_ref[0, :, d:] = aff


def _outproj(xa, att, cg, o_f, o_b, w_conv, hgrn_norm_w, w_out_bf, modtab, ln_w, ln_b, w_router_pad, *,
             m_ctx, alpha, n_experts, cw, col_gg):
    b, l, d = xa.shape
    tm = TILE
    aw = att.shape[2]
    hw = o_f.shape[2]
    r8 = tm // 8
    last8 = l // 8 - 1
    ctx_tiles = m_ctx // tm

    def rows(width, col):
        return pl.BlockSpec((1, tm, width), lambda bb, i: (bb, i, col))

    def halo_prev(col):
        return pl.BlockSpec((1, 8, cw), lambda bb, i: (bb, jnp.maximum(i * r8 - 1, 0), col))

    def halo_next(col):
        return pl.BlockSpec((1, 8, cw), lambda bb, i: (bb, jnp.minimum((i + 1) * r8, last8), col))

    def const(shape):
        return pl.BlockSpec(shape, lambda bb, i: (0,) * len(shape))

    kern = functools.partial(_outproj_kernel, ctx_tiles=ctx_tiles, alpha=alpha, n_experts=n_experts)
    return pl.pallas_call(
        kern,
        grid=(b, l // tm),
        in_specs=[rows(d, 0), rows(aw, 0),
                  rows(cw, 0), rows(cw, 1), rows(cw, 2),
                  halo_prev(0), halo_prev(2), halo_next(0), halo_next(2),
                  rows(hw, 0), rows(hw, 0), rows(hw, col_gg),
                  const((3, cw)), const((1, HEAD_LANES)), const((d, d)),
                  pl.BlockSpec((1, 1, 6, d), lambda bb, i: (bb, jnp.minimum(i // ctx_tiles, 1), 0, 0)),
                  const((1, d)), const((1, d)), const((d, HEAD_LANES))],
        out_specs=[rows(d, 0), rows(d + HEAD_LANES, 0), rows(n_experts, 0)],
        out_shape=[jax.ShapeDtypeStruct((b, l, d), F32), jax.ShapeDtypeStruct((b, l, d + HEAD_LANES), F32),
                   jax.ShapeDtypeStruct((b, l, n_experts), F32)],
        compiler_params=_cparams(("arbitrary", "arbitrary")),
        name="outproj_ln_router",
    )(xa, att, cg, cg, cg, cg, cg, cg, cg, o_f, o_b, cg, w_conv, hgrn_norm_w.reshape(1, HEAD_LANES), w_out_bf,
      modtab, ln_w.reshape(1, d), ln_b.reshape(1, d), w_router_pad)


def _topk_kernel(aff_ref, pos_ref, off_ref, idx_ref, cnt_scr, off_smem, dsem, *, m_ctx, cap_ctx, cap_lat):
    aff = aff_ref[0]
    ne, l = aff.shape
    ch = COMBINE_TILE
    nch = l // ch
    ctx_ch = m_ctx // ch
    bits = pltpu.bitcast(aff, jnp.int32)
    lane = lax.broadcasted_iota(jnp.int32, (ne, l), 1)
    in_ctx = lane < m_ctx

    def kth_largest(seg, k):
        def body(_, lohi):
            lo, hi = lohi
            mid = lo + lax.shift_right_logical(hi - lo, 1)
            cnt = jnp.sum(jnp.where(jnp.logical_and(seg, bits >= mid), 1.0, 0.0), axis=1, keepdims=True)
            ge = cnt >= k
            return jnp.where(ge, mid, lo), jnp.where(ge, hi, mid)
        lo0 = jnp.zeros((ne, 1), jnp.int32)
        hi0 = jnp.full((ne, 1), 0x7F800000, jnp.int32)
        return lax.fori_loop(0, 31, body, (lo0, hi0))[0]

    thr = jnp.where(in_ctx, kth_largest(in_ctx, cap_ctx), kth_largest(jnp.logical_not(in_ctx), cap_lat))
    gt = bits > thr
    eq = bits == thr
    gtf = jnp.where(gt, 1.0, 0.0)
    n_gt_ctx = jnp.sum(jnp.where(in_ctx, gtf, 0.0), axis=1, keepdims=True)
    n_gt_lat = jnp.sum(jnp.where(in_ctx, 0.0, gtf), axis=1, keepdims=True)
    need = jnp.where(in_ctx, cap_ctx - n_gt_ctx, cap_lat - n_gt_lat)

    tr = lax.broadcasted_iota(jnp.int32, (ch, ch), 0)
    tc = lax.broadcasted_iota(jnp.int32, (ch, ch), 1)
    tri = jnp.where(tr <= tc, 1.0, 0.0).astype(BF16)

    def seg_prefix(flag):
        excl, bases = [], []
        base = jnp.zeros((ne, 1), F32)
        for c in range(nch):
            if c == ctx_ch:
                base = jnp.zeros((ne, 1), F32)
            fc = flag[:, c * ch:(c + 1) * ch]
            incl = jnp.dot(fc.astype(BF16), tri, preferred_element_type=F32)
            excl.append(base + incl - fc)
            bases.append(base)
            base = base + incl[:, ch - 1:ch]
        bases.append(base)
        return jnp.concatenate(excl, axis=1), bases

    eq_excl, _ = seg_prefix(jnp.where(eq, 1.0, 0.0))
    sel = jnp.logical_or(gt, jnp.logical_and(eq, eq_excl < need))
    self_ = jnp.where(sel, 1.0, 0.0)
    sel_excl, bases = seg_prefix(self_)
    seg_off = jnp.where(in_ctx, 0.0, float(cap_ctx))
    pos_ref[0] = jnp.where(sel, sel_excl + seg_off, -1.0).astype(jnp.int32)

    olane = lax.broadcasted_iota(jnp.int32, (ne, HEAD_LANES), 1)
    off = jnp.zeros((ne, HEAD_LANES), F32)
    for c in range(nch + 1):
        off = jnp.where(olane == c, bases[c] + (0.0 if c < ctx_ch else float(cap_ctx)), off)
    off_ref[0] = off.astype(jnp.int32)

    cnt = sel_excl + self_ + seg_off
    for e in range(ne):
        cnt_scr[e] = cnt[e:e + 1]
    n_slots = idx_ref.shape[1]
    sb = n_slots // IDX_SLOT_BLOCKS
    ilane = lax.broadcasted_iota(jnp.int32, (sb, HEAD_LANES), 1)
    off_copy = pltpu.make_async_copy(off_ref.at[0], off_smem, dsem.at[0])
    off_copy.start()
    off_copy.wait()
    for blk in range(IDX_SLOT_BLOCKS):
        slot = (lax.broadcasted_iota(jnp.int32, (sb, HEAD_LANES), 0) + blk * sb).astype(F32)
        p_lo, p_hi = blk * sb, blk * sb + sb - 1

        def per_expert(e, out):
            def scan(c, lohi):
                return (lohi[0] + jnp.where(off_smem[e, c + 1] <= p_lo, 1, 0),
                        lohi[1] + jnp.where(off_smem[e, c] <= p_hi, 1, 0))
            c_lo, c_hi = lax.fori_loop(0, nch, scan, (jnp.int32(0), jnp.int32(0)))

            def per_chunk(c, acc):
                cnt_row = cnt_scr[e, :, pl.ds(pl.multiple_of(c * ch, ch), ch)]
                return acc + jnp.where(cnt_row <= slot, 1.0, 0.0)
            acc = lax.fori_loop(c_lo, c_hi, per_chunk, jnp.zeros((sb, ch), F32))
            full = (c_lo * ch).astype(F32)
            return jnp.where(ilane == e, jnp.sum(acc, axis=1, keepdims=True) + full, out)
        out = lax.fori_loop(0, ne, per_expert, jnp.zeros((sb, HEAD_LANES), F32))
        idx_ref[0, blk * sb:(blk + 1) * sb, :] = out.astype(jnp.int32)


def _topk(aff_t, *, m_ctx, cap_ctx, cap_lat):
    b, ne, l = aff_t.shape
    n_slots = cap_ctx + cap_lat
    kern = functools.partial(_topk_kernel, m_ctx=m_ctx, cap_ctx=cap_ctx, cap_lat=cap_lat)
    return pl.pallas_call(
        kern,
        grid=(b,),
        in_specs=[pl.BlockSpec((1, ne, l), lambda bb: (bb, 0, 0))],
        out_specs=[pl.BlockSpec((1, ne, l), lambda bb: (bb, 0, 0)),
                   pl.BlockSpec((1, ne, HEAD_LANES), lambda bb: (bb, 0, 0)),
                   pl.BlockSpec((1, n_slots, HEAD_LANES), lambda bb: (bb, 0, 0))],
        out_shape=[jax.ShapeDtypeStruct((b, ne, l), jnp.int32),
                   jax.ShapeDtypeStruct((b, ne, HEAD_LANES), jnp.int32),
                   jax.ShapeDtypeStruct((b, n_slots, HEAD_LANES), jnp.int32)],
        scratch_shapes=[pltpu.VMEM((ne, 1, l), F32), pltpu.SMEM((ne, HEAD_LANES), jnp.int32),
                        pltpu.SemaphoreType.DMA((1,))],
        compiler_params=_cparams(("arbitrary",)),
        name="expert_choice_topk",
    )(aff_t)


def _ffn_kernel(idx_ref, h_hbm, wg_ref, wu_ref, wd_ref, y_ref, xg_scr, xb_scr, gate_scr, acc_scr, sem, *, n_experts,
                steps):
    e, b, f = pl.program_id(0), pl.program_id(1), pl.program_id(2)
    nb, nf = pl.num_programs(1), pl.num_programs(2)
    n_slots, d = xb_scr.shape

    def row_copy(bb, ee, r):
        tok = idx_ref[(bb * n_experts + ee) * n_slots + r]
        return pltpu.make_async_copy(h_hbm.at[bb, pl.ds(tok, 1)], xg_scr.at[pl.ds(r, 1)], sem.at[0])

    @pl.when(f == 0)
    def _():
        @pl.when(jnp.logical_and(e == 0, b == 0))
        def _():
            def issue(r, carry):
                row_copy(b, e, r).start()
                return carry
            lax.fori_loop(0, n_slots, issue, 0, unroll=8)
        pltpu.make_async_copy(h_hbm.at[b, pl.ds(0, n_slots)], xg_scr, sem.at[0]).wait()
        xb_scr[...] = xg_scr[:, :d].astype(BF16)
        tail = xg_scr[:, d:]
        lane = lax.broadcasted_iota(jnp.int32, tail.shape, 1)
        gate_scr[...] = jnp.sum(jnp.where(lane == e, tail, 0.0), axis=1, keepdims=True)
        acc_scr[...] = jnp.zeros(acc_scr.shape, F32)

    wrap = b + 1 == nb
    nxt_b = jnp.where(wrap, 0, b + 1)
    nxt_e = jnp.where(wrap, jnp.where(e + 1 == n_experts, 0, e + 1), e)
    share = n_slots // steps
    for r in range(share):
        row_copy(nxt_b, nxt_e, f * share + r).start()

    wg, wu, wd = wg_ref[0, 0].astype(BF16), wu_ref[0, 0].astype(BF16), wd_ref[0, 0].astype(BF16)
    half = n_slots // 2
    rows = [slice(0, half), slice(half, n_slots)]
    au = [(jnp.dot(xb_scr[r, :], wg, preferred_element_type=F32), jnp.dot(xb_scr[r, :], wu, preferred_element_type=F32))
          for r in rows]
    down = [jnp.dot((_silu(a) * u).astype(BF16), wd, preferred_element_type=F32) for a, u in au]
    for r, dn in zip(rows, down):
        acc_scr[r, :] += dn

    @pl.when(f == nf - 1)
    def _():
        y_ref[0, 0] = (acc_scr[...] * gate_scr[...]).astype(y_ref.dtype)

        @pl.when(jnp.logical_and(e == n_experts - 1, wrap))
        def _():
            pltpu.make_async_copy(h_hbm.at[b, pl.ds(0, n_slots)], xg_scr, sem.at[0]).wait()


def _expert_ffn(idx_flat, h2, w_gate, w_up, w_down, *, n_slots, layer):
    b, l, dx = h2.shape
    _, ne, d, ff = w_gate.shape
    tf = 256
    grid_spec = pltpu.PrefetchScalarGridSpec(
        num_scalar_prefetch=1,
        grid=(ne, b, ff // tf),
        in_specs=[pl.BlockSpec(memory_space=pl.ANY),
                  pl.BlockSpec((1, 1, d, tf), lambda e, bb, f, idx: (layer, e, 0, f)),
                  pl.BlockSpec((1, 1, d, tf), lambda e, bb, f, idx: (layer, e, 0, f)),
                  pl.BlockSpec((1, 1, tf, d), lambda e, bb, f, idx: (layer, e, f, 0))],
        out_specs=pl.BlockSpec((1, 1, n_slots, d), lambda e, bb, f, idx: (bb, e, 0, 0)),
        scratch_shapes=[pltpu.VMEM((n_slots, dx), F32), pltpu.VMEM((n_slots, d), BF16),
                        pltpu.VMEM((n_slots, 1), F32), pltpu.VMEM((n_slots, d), F32),
                        pltpu.SemaphoreType.DMA((1,))])
    assert ff % tf == 0 and n_slots % (ff // tf) == 0
    return pl.pallas_call(
        functools.partial(_ffn_kernel, n_experts=ne, steps=ff // tf),
        grid_spec=grid_spec,
        out_shape=jax.ShapeDtypeStruct((b, ne, n_slots, d), BF16),
        compiler_params=_cparams(("arbitrary", "arbitrary", "arbitrary")),
        name="expert_ffn",
    )(idx_flat, h2, w_gate, w_up, w_down)


def _combine_kernel(off_ref, x1_ref, pos_ref, mod_ref, lnw_ref, lnb_ref, y_hbm, o_ref, win_scr, sem, *,
                    alpha, n_experts, win):
    b, i = pl.program_id(0), pl.program_id(1)
    nb, nt = pl.num_programs(0), pl.num_programs(1)
    tm = x1_ref.shape[1]
    n_slots = y_hbm.shape[2]
    blk = win_scr.shape[1] // n_experts
    step = b * nt + i
    slot = step % 2

    def window(bb, ii, e):
        base = (bb * n_experts + e) * HEAD_LANES + ii
        start = jnp.minimum((off_ref[base] // WIN_ALIGN) * WIN_ALIGN, n_slots - win)
        return pl.multiple_of(start, WIN_ALIGN), off_ref[base + 1] - start <= WIN_SHORT

    def transfer(bb, ii, sl, go):
        for e in range(n_experts):
            start, short = window(bb, ii, e)
            for size, cond in ((WIN_SHORT, short), (win, jnp.logical_not(short))):
                @pl.when(cond)
                def _():
                    go(pltpu.make_async_copy(y_hbm.at[bb, e, pl.ds(start, size)],
                                             win_scr.at[sl, pl.ds(e * blk, size)], sem.at[sl, e]))

    @pl.when(step == 0)
    def _():
        win_scr[...] = jnp.zeros(win_scr.shape, win_scr.dtype)
        transfer(b, i, slot, lambda cp: cp.start())

    @pl.when(step + 1 < nb * nt)
    def _():
        wrap = i + 1 == nt
        transfer(jnp.where(wrap, b + 1, b), jnp.where(wrap, 0, i + 1), 1 - slot, lambda cp: cp.start())

    pos = pos_ref[0]
    scol = lax.broadcasted_iota(jnp.int32, (tm, blk), 1)
    onehot = jnp.concatenate([jnp.where((pos[:, e:e + 1] - window(b, i, e)[0]) == scol, 1.0, 0.0).astype(BF16)
                              for e in range(n_experts)], axis=1)
    transfer(b, i, slot, lambda cp: cp.wait())
    moe = jnp.dot(onehot, win_scr[slot], preferred_element_type=F32)
    mod = mod_ref[0, 0]
    o_ref[0] = _layer_norm(alpha * x1_ref[0] + mod[5:6] * moe, lnw_ref[...], lnb_ref[...])


def _combine(off_flat, x1, pos_tok, modtab, ln_w, ln_b, y, *, m_ctx, alpha, latent_only):
    b, l, d = x1.shape
    ne = pos_tok.shape[2]
    tm = COMBINE_TILE
    win = tm + WIN_ALIGN
    blk = -(-win // HEAD_LANES) * HEAD_LANES
    ctx_tiles = m_ctx // tm

    def rows(width):
        return pl.BlockSpec((1, tm, width), lambda bb, i, off: (bb, i, 0))

    out_rows = l - m_ctx if latent_only else l
    out_spec = (pl.BlockSpec((1, tm, d), lambda bb, i, off: (bb, jnp.maximum(i - ctx_tiles, 0), 0))
                if latent_only else rows(d))

    grid_spec = pltpu.PrefetchScalarGridSpec(
        num_scalar_prefetch=1,
        grid=(b, l // tm),
        in_specs=[rows(d), rows(ne),
                  pl.BlockSpec((1, 1, 6, d), lambda bb, i, off: (bb, jnp.minimum(i // ctx_tiles, 1), 0, 0)),
                  pl.BlockSpec((1, d), lambda bb, i, off: (0, 0)),
                  pl.BlockSpec((1, d), lambda bb, i, off: (0, 0)),
                  pl.BlockSpec(memory_space=pl.ANY)],
        out_specs=out_spec,
        scratch_shapes=[pltpu.VMEM((2, ne * blk, d), BF16), pltpu.SemaphoreType.DMA((2, ne))])
    return pl.pallas_call(
        functools.partial(_combine_kernel, alpha=alpha, n_experts=ne, win=win),
        grid_spec=grid_spec,
        out_shape=jax.ShapeDtypeStruct((b, out_rows, d), F32),
        compiler_params=_cparams(("arbitrary", "arbitrary")),
        name="moe_combine_ln",
    )(off_flat, x1, pos_tok, modtab, ln_w.reshape(1, d), ln_b.reshape(1, d), y)


def kernel(x, c, ctx, c_ctx, w_mod, b_mod, w_in, w_conv, lambda_qk, subln_w, hgrn_lb_logits, hgrn_norm_w, w_out,
           ln_w, ln_b, w_router, w_gate, w_up, w_down):
    bsz, n_lat, d = x.shape
    m_ctx = ctx.shape[1]
    depth = w_mod.shape[0]
    ne = w_router.shape[2]
    aw, cw, hw = d // 2, d // 4, d // 4
    assert m_ctx % TILE == 0 and n_lat % 1024 == 0 and n_lat % GRID_W == 0 and bsz + 1 <= 8
    assert (m_ctx + n_lat) % (8 * 16) == 0 and aw == 1024 and ne <= HEAD_LANES
    alpha = (2.0 * depth) ** 0.25
    cap_ctx = EC_CAPACITY_FACTOR * m_ctx // ne
    cap_lat = EC_CAPACITY_FACTOR * n_lat // ne
    n_slots = cap_ctx + cap_lat
    assert n_slots % WIN_ALIGN == 0 and n_slots >= COMBINE_TILE + WIN_ALIGN
    assert (m_ctx + n_lat) // COMBINE_TILE + 1 <= HEAD_LANES

    cs = jnp.zeros((8, d), F32).at[:bsz].set(c).at[bsz].set(c_ctx)
    mod = _modulation(cs, w_mod, b_mod)
    mod_lat = mod[:, :bsz].reshape(depth, bsz, 1, 6, d)
    mod_ctx = jnp.broadcast_to(mod[:, bsz].reshape(depth, 1, 1, 6, d), (depth, bsz, 1, 6, d))
    modtab = jnp.concatenate([mod_ctx, mod_lat], axis=2)

    tables = _rope_tables(m_ctx, n_lat)
    xa = jnp.concatenate([ctx, x], axis=1)
    w_router_pad = jnp.zeros((depth, d, HEAD_LANES), F32).at[:, :, :ne].set(w_router)
    hg_heads = hw // HEAD_LANES
    cb0 = 3 * cw // HEAD_LANES

    for l in range(depth):
        lambda_init = 0.8 - 0.6 * math.exp(-0.3 * l)
        w_in_bf = w_in[l].astype(BF16)
        qkv = _inproj(xa, modtab[l], w_in_bf, tables, col0=0, ncols=3 * aw, out_dtype=BF16, rope_tiles=2,
                      m_ctx=m_ctx)
        cg = _inproj(xa, modtab[l], w_in_bf, tables, col0=3 * aw, ncols=3 * cw + 5 * hw, out_dtype=F32,
                     rope_tiles=0, m_ctx=m_ctx)
        att = _attention(qkv, lambda_qk[l], subln_w[l], m_ctx=m_ctx, lambda_init=lambda_init)
        o_f, o_b = _hgrn(cg, hgrn_lb_logits, layer=l, hg_heads=hg_heads, col_q=cb0, col_i=cb0 + hg_heads,
                         col_ff=cb0 + 3 * hg_heads, col_fb=cb0 + 4 * hg_heads)
        x1, h2, aff = _outproj(xa, att, cg, o_f, o_b, w_conv[l], hgrn_norm_w[l], w_out[l].astype(BF16), modtab[l],
                               ln_w[l, 0], ln_b[l, 0], w_router_pad[l], m_ctx=m_ctx, alpha=alpha, n_experts=ne,
                               cw=cw, col_gg=(3 * cw + 2 * hw) // hw)
        pos, off, idx = _topk(jnp.swapaxes(aff, 1, 2), m_ctx=m_ctx, cap_ctx=cap_ctx, cap_lat=cap_lat)
        idx_flat = jnp.minimum(jnp.swapaxes(idx[:, :, :ne], 1, 2).reshape(-1), m_ctx + n_lat - 1)
        y = _expert_ffn(idx_flat, h2, w_gate, w_up, w_down, n_slots=n_slots, layer=l)
        xa = _combine(off.reshape(-1), x1, jnp.swapaxes(pos, 1, 2), modtab[l], ln_w[l, 1], ln_b[l, 1], y,
                      m_ctx=m_ctx, alpha=alpha, latent_only=l == depth - 1)
    return xa
```

```python
import functools
import math

import jax
import jax.numpy as jnp
from jax import lax
from jax.experimental import pallas as pl
from jax.experimental.pallas import tpu as pltpu

F32 = jnp.float32
BF16 = jnp.bfloat16
HIGHEST = lax.Precision.HIGHEST

GRID_W = 64
ROPE_THETA = 10000.0
ATT_HEAD_DIM = 64
HEAD_LANES = 128
HGRN_CHUNK = 64
HGRN_SAFE_EXP = 80.0
EC_CAPACITY_FACTOR = 2
EPS = 1e-6
TILE = 256
INPROJ_SUB = 256
ATTN_UNROLL = 4
COMBINE_TILE = 128
IDX_SLOT_BLOCKS = 4
WIN_SHORT = 48
WIN_ALIGN = 16
VMEM_LIMIT = 56 * 1024 * 1024


def _cparams(sem):
    return pltpu.CompilerParams(dimension_semantics=sem, vmem_limit_bytes=VMEM_LIMIT)


def _silu(x):
    return x * jax.nn.sigmoid(x)


def _mod_kernel(cs_ref, w_ref, b_ref, o_ref):
    a = _silu(cs_ref[...])
    o_ref[0] = jnp.dot(a, w_ref[0], precision=HIGHEST, preferred_element_type=F32) + b_ref[0]


def _modulation(cs, w_mod, b_mod):
    depth, d, n6 = w_mod.shape
    tn = 1024
    return pl.pallas_call(
        _mod_kernel,
        grid=(depth, n6 // tn),
        in_specs=[pl.BlockSpec((8, d), lambda l, j: (0, 0)),
                  pl.BlockSpec((1, d, tn), lambda l, j: (l, 0, j)),
                  pl.BlockSpec((1, 1, tn), lambda l, j: (l, 0, j))],
        out_specs=pl.BlockSpec((1, 8, tn), lambda l, j: (l, 0, j)),
        out_shape=jax.ShapeDtypeStruct((depth, 8, n6), F32),
        compiler_params=_cparams(("arbitrary", "arbitrary")),
        name="modulation",
    )(cs, w_mod, b_mod.reshape(depth, 1, n6))


def _inproj_kernel(x_ref, mod_ref, w_ref, rc_ref, ra_ref, rb_ref, o_ref, h_scr, *, m_ctx, rope_tiles):
    i = pl.program_id(1)
    j = pl.program_id(2)
    tm = x_ref.shape[1]

    @pl.when(j == 0)
    def _():
        row = i * tm + lax.broadcasted_iota(jnp.int32, (tm, 1), 0)
        is_ctx = row < m_ctx
        mod = mod_ref[0]
        sh = jnp.where(is_ctx, mod[0, 0:1], mod[1, 0:1])
        sc = jnp.where(is_ctx, mod[0, 1:2], mod[1, 1:2])
        h_scr[...] = (x_ref[0] * (1.0 + sc) + sh).astype(BF16)

    h = h_scr[...]
    if rope_tiles:
        rc, ra, rb = rc_ref[0], ra_ref[0], rb_ref[0]
        qscale = jnp.where(j == 0, ATT_HEAD_DIM ** -0.5 * math.log2(math.e), 1.0).astype(F32)

    def finish(c, acc):
        for k in range(INPROJ_SUB // HEAD_LANES):
            blk = acc[:, k * HEAD_LANES:(k + 1) * HEAD_LANES]
            if rope_tiles:
                blk = (blk * rc + pltpu.roll(blk, 16, 1) * ra + pltpu.roll(blk, HEAD_LANES - 16, 1) * rb) * qscale
            lo = c * INPROJ_SUB + k * HEAD_LANES
            o_ref[0, :, lo:lo + HEAD_LANES] = blk.astype(o_ref.dtype)

    pending = None
    for c in range(w_ref.shape[1] // INPROJ_SUB):
        acc = jnp.dot(h, w_ref[:, c * INPROJ_SUB:(c + 1) * INPROJ_SUB], preferred_element_type=F32)
        if pending is not None:
            finish(*pending)
        pending = (c, acc)
    finish(*pending)


def _inproj(xa, modtab, w_bf, tables, *, col0, ncols, out_dtype, rope_tiles, m_ctx):
    b, l, d = xa.shape
    tn = 1024
    tm = l // 8
    joff = col0 // tn
    kern = functools.partial(_inproj_kernel, m_ctx=m_ctx, rope_tiles=rope_tiles)
    tab_spec = pl.BlockSpec((1, tm, HEAD_LANES), lambda bb, i, j: (jnp.where(j < rope_tiles, 0, 1), i, 0))
    return pl.pallas_call(
        kern,
        grid=(b, l // tm, ncols // tn),
        in_specs=[pl.BlockSpec((1, tm, d), lambda bb, i, j: (bb, i, 0)),
                  pl.BlockSpec((1, 2, 6, d), lambda bb, i, j: (bb, 0, 0, 0)),
                  pl.BlockSpec((d, tn), lambda bb, i, j: (0, j + joff)),
                  tab_spec, tab_spec, tab_spec],
        out_specs=pl.BlockSpec((1, tm, tn), lambda bb, i, j: (bb, i, j)),
        out_shape=jax.ShapeDtypeStruct((b, l, ncols), out_dtype),
        scratch_shapes=[pltpu.VMEM((tm, d), BF16)],
        compiler_params=_cparams(("arbitrary", "arbitrary", "arbitrary")),
        name="inproj",
    )(xa, modtab, w_bf, *tables)


def _rope_tables(m_ctx, n_lat):
    nf = ATT_HEAD_DIM // 4
    inv = ROPE_THETA ** (-jnp.arange(nf, dtype=F32) / nf)
    rows = n_lat // GRID_W
    row = jnp.repeat(jnp.arange(rows, dtype=F32), GRID_W)
    col = jnp.tile(jnp.arange(GRID_W, dtype=F32), rows)
    ar, ac = row[:, None] * inv, col[:, None] * inv
    cr, sr, cc, sc = jnp.cos(ar), jnp.sin(ar), jnp.cos(ac), jnp.sin(ac)
    z = jnp.zeros_like(sr)
    c64 = jnp.concatenate([cr, cr, cc, cc], axis=1)
    a64 = jnp.concatenate([z, sr, z, sc], axis=1)
    b64 = jnp.concatenate([-sr, z, -sc, z], axis=1)
    reps = HEAD_LANES // ATT_HEAD_DIM

    def full(t64, fill):
        lat = jnp.tile(t64, (1, reps))
        rot = jnp.concatenate([jnp.full((m_ctx, HEAD_LANES), fill, F32), lat], axis=0)
        return jnp.stack([rot, jnp.full(rot.shape, fill, F32)])

    return full(c64, 1.0), full(a64, 0.0), full(b64, 0.0)


def _attn_kernel(lq_ref, sw_ref, q_ref, k_ref, v_ref, o_ref, vt_scr, m0_scr, l0_scr, acc0_scr, m1_scr, l1_scr,
                 acc1_scr, sa0_scr, sa1_scr, sb0_scr, sb1_scr, *, m_ctx, tk, lambda_init):
    i = pl.program_id(2)
    tq = q_ref.shape[1]
    n_lat = k_ref.shape[1] - m_ctx

    @pl.when(i == 0)
    def _():
        vt_scr[...] = v_ref[0].T

    q = q_ref[0]
    lane = lax.broadcasted_iota(jnp.int32, q.shape, 1)
    zero = jnp.zeros_like(q)
    qs = (jnp.where(lane < ATT_HEAD_DIM, q, zero), jnp.where(lane >= ATT_HEAD_DIM, q, zero))
    stats = ((m0_scr, l0_scr, acc0_scr), (m1_scr, l1_scr, acc1_scr))
    for m_scr, l_scr, acc_scr in stats:
        m_scr[...] = jnp.full(m_scr.shape, -jnp.inf, F32)
        l_scr[...] = jnp.zeros(l_scr.shape, F32)
        acc_scr[...] = jnp.zeros(acc_scr.shape, F32)

    def scores(start, size, st_refs):
        kc = k_ref[0, pl.ds(start, size), :]
        cmax = []
        for s in range(2):
            st = lax.dot_general(kc, qs[s], (((1,), (1,)), ((), ())), preferred_element_type=F32)
            st_refs[s][0:size, :] = st
            cmax.append(jnp.max(st, axis=0, keepdims=True))
        return tuple(cmax)

    def accumulate(start, size, st_refs, cmax):
        vt = vt_scr[:, pl.ds(start, size)]
        for s, (m_scr, l_scr, acc_scr) in enumerate(stats):
            m_old = m_scr[...]
            m_new = jnp.maximum(m_old, cmax[s])
            alpha = jnp.exp2(m_old - m_new)
            p = jnp.exp2(st_refs[s][0:size, :] - m_new)
            l_scr[...] = alpha * l_scr[...] + jnp.sum(p, axis=0, keepdims=True)
            acc_scr[...] = alpha * acc_scr[...] + jnp.dot(vt, p.astype(BF16), preferred_element_type=F32)
            m_scr[...] = m_new

    st_a, st_b = (sa0_scr, sa1_scr), (sb0_scr, sb1_scr)

    @pl.when(i * tq < m_ctx)
    def _():
        accumulate(0, m_ctx, st_a, scores(0, m_ctx, st_a))

    @pl.when(i * tq >= m_ctx)
    def _():
        def at(c):
            return pl.multiple_of(m_ctx + c * tk, math.gcd(m_ctx, tk))
        nck = n_lat // tk
        bufs = (st_b, st_a)

        def run(first, count, cm):
            for u in range(count):
                c = first + u
                last = isinstance(c, int) and c + 1 >= nck
                cm_next = None if last else scores(at(c + 1), tk, bufs[(u + 1) % 2])
                accumulate(at(c), tk, bufs[u % 2], cm)
                cm = cm_next
            return cm

        cm_ctx = scores(0, m_ctx, st_a)
        cm = scores(at(0), tk, st_b)
        accumulate(0, m_ctx, st_a, cm_ctx)
        trips = (nck - 1) // ATTN_UNROLL
        cm = lax.fori_loop(0, trips, lambda j, cm: run(j * ATTN_UNROLL, ATTN_UNROLL, cm), cm)
        run(trips * ATTN_UNROLL, nck - trips * ATTN_UNROLL, cm)

    lq = lq_ref[...]
    lam = (jnp.exp(jnp.sum(lq[0:1] * lq[1:2], axis=1, keepdims=True))
           - jnp.exp(jnp.sum(lq[2:3] * lq[3:4], axis=1, keepdims=True)) + lambda_init)
    ot = acc0_scr[...] / l0_scr[...] - lam * (acc1_scr[...] / l1_scr[...])
    ms = jnp.mean(ot * ot, axis=0, keepdims=True)
    o = (ot * lax.rsqrt(ms + EPS)).T * sw_ref[...] * (1.0 - lambda_init)
    o_ref[0] = o.astype(o_ref.dtype)


def _attention(qkv, lambda_qk, subln_w, *, m_ctx, lambda_init):
    b, l, w3 = qkv.shape
    heads = w3 // (3 * HEAD_LANES)
    tq = TILE
    tk = 2048
    kern = functools.partial(_attn_kernel, m_ctx=m_ctx, tk=tk, lambda_init=lambda_init)
    return pl.pallas_call(
        kern,
        grid=(b, heads, l // tq),
        in_specs=[pl.BlockSpec((4, ATT_HEAD_DIM), lambda bb, h, i: (0, 0)),
                  pl.BlockSpec((1, HEAD_LANES), lambda bb, h, i: (0, 0)),
                  pl.BlockSpec((1, tq, HEAD_LANES), lambda bb, h, i: (bb, i, h)),
                  pl.BlockSpec((1, l, HEAD_LANES), lambda bb, h, i: (bb, 0, heads + h)),
                  pl.BlockSpec((1, l, HEAD_LANES), lambda bb, h, i: (bb, 0, 2 * heads + h))],
        out_specs=pl.BlockSpec((1, tq, HEAD_LANES), lambda bb, h, i: (bb, i, h)),
        out_shape=jax.ShapeDtypeStruct((b, l, heads * HEAD_LANES), BF16),
        scratch_shapes=[pltpu.VMEM((HEAD_LANES, l), BF16)]
        + [pltpu.VMEM((1, tq), F32), pltpu.VMEM((1, tq), F32), pltpu.VMEM((HEAD_LANES, tq), F32)] * 2
        + [pltpu.VMEM((max(tk, m_ctx), tq), F32)] * 4,
        compiler_params=_cparams(("arbitrary", "arbitrary", "arbitrary")),
        name="diff_attention",
    )(lambda_qk, subln_w.reshape(1, HEAD_LANES), qkv, qkv, qkv)


def _log_sigmoid(z):
    return jnp.minimum(z, 0.0) - jnp.log1p(jnp.exp(-jnp.abs(z)))


def _forget_gate(z, lb):
    ls = _log_sigmoid(z)
    key = jax.nn.sigmoid(-z)
    if lb is None:
        return ls, key
    a = jnp.log(lb)
    b = jnp.log1p(-lb) + ls
    logf = jnp.maximum(a, b) + jnp.log1p(jnp.exp(-jnp.abs(a - b)))
    return logf, (1.0 - lb) * key


def _hgrn_tile(dirs):
    ch = HGRN_CHUNK
    nt_dims = (((1,), (1,)), ((), ()))
    work = []
    for q_ref, v_ref, z_ref, o_ref, s_scr, lb, reverse in dirs:
        t = q_ref.shape[1]
        logf, key = _forget_gate(z_ref[0], lb)
        r = lax.broadcasted_iota(jnp.int32, (t, t), 0)
        c = lax.broadcasted_iota(jnp.int32, (t, t), 1)
        tri = ((r // ch) == (c // ch)) & ((c >= r) if reverse else (c <= r))
        a = jnp.dot(tri.astype(F32), logf, precision=HIGHEST, preferred_element_type=F32)
        nch = t // ch

        def per_chunk(row_of):
            return jnp.concatenate([jnp.broadcast_to(row_of(a[k * ch:(k + 1) * ch]), (ch, a.shape[1]))
                                    for k in range(nch)], axis=0)
        a_mid = per_chunk(lambda ac: ac[ch // 2 - 1:ch // 2])
        a_end = per_chunk((lambda ac: ac[0:1]) if reverse else (lambda ac: ac[ch - 1:ch]))
        work.append(dict(q=q_ref[0], v32=v_ref[0], v=v_ref[0].astype(BF16), key=key, logf=logf, a=a, a_mid=a_mid,
                         a_end=a_end, tri=tri, o_ref=o_ref, s_scr=s_scr, reverse=reverse, nch=nch))

    ends = [jnp.abs(w["a"][r:r + 1] - w["a"][k * ch + ch // 2 - 1:k * ch + ch // 2])
            for w in work for k in range(w["nch"]) for r in (k * ch, (k + 1) * ch - 1)]
    safe = jnp.max(functools.reduce(jnp.maximum, ends)) <= HGRN_SAFE_EXP

    start_states = [w["s_scr"][...] for w in work]
    _hgrn_tile_chunked(work)

    @pl.when(jnp.logical_not(safe))
    def _():
        for w, st0 in zip(work, start_states):
            _hgrn_tile_stepwise(w, st0)


def _hgrn_tile_stepwise(w, st0):
    t, hd = w["q"].shape
    q_t, k_t, f_t, v_t = w["q"].T, w["key"].T, jnp.exp(w["logf"]).T, w["v32"].T
    lane = lax.broadcasted_iota(jnp.int32, (hd, t), 1)
    eye = lax.broadcasted_iota(jnp.int32, (hd, hd), 0) == lax.broadcasted_iota(jnp.int32, (hd, hd), 1)

    def column(x_t, r):
        return jnp.sum(jnp.where(lane == r, x_t, 0.0), axis=1, keepdims=True)

    def as_row(col):
        return jnp.sum(jnp.where(eye, col, 0.0), axis=0, keepdims=True)

    def body(n, carry):
        st, o_t = carry
        r = t - 1 - n if w["reverse"] else n
        st = as_row(column(f_t, r)) * st + column(v_t, r) * as_row(column(k_t, r))
        o_col = jnp.sum(st * as_row(column(q_t, r)), axis=1, keepdims=True)
        return st, jnp.where(lane == r, o_col, o_t)

    st, o_t = lax.fori_loop(0, t, body, (st0, jnp.zeros((hd, t), F32)))
    w["s_scr"][...] = st
    w["o_ref"][0] = o_t.T


def _hgrn_tile_chunked(work):
    ch = HGRN_CHUNK
    nt_dims = (((1,), (1,)), ((), ()))
    for w in work:
        a, nch, a_mid, a_end = w["a"], w["nch"], w["a_mid"], w["a_end"]
        qe = (w["q"] * jnp.exp(a - a_mid)).astype(BF16)
        ke = (w["key"] * jnp.exp(a_mid - a)).astype(BF16)
        kd = (w["key"] * jnp.exp(a_end - a)).astype(BF16)
        w["qa"] = (w["q"] * jnp.exp(a)).astype(BF16)
        w["decay"] = jnp.exp(a_end)
        w["sc"] = lax.dot_general(qe, ke, nt_dims, preferred_element_type=F32)
        w["kv"] = [lax.dot_general(w["v"][k * ch:(k + 1) * ch], kd[k * ch:(k + 1) * ch], (((0,), (0,)), ((), ())),
                                   preferred_element_type=F32) for k in range(nch)]

    for w in work:
        w["intra"] = jnp.dot(jnp.where(w["tri"], w["sc"], 0.0).astype(BF16), w["v"], preferred_element_type=F32)

    for w in work:
        nch = w["nch"]
        st = w["s_scr"][...]
        states = [None] * nch
        for k in (range(nch - 1, -1, -1) if w["reverse"] else range(nch)):
            states[k] = st.astype(BF16)
            row = k * ch if w["reverse"] else (k + 1) * ch - 1
            st = w["decay"][row:row + 1] * st + w["kv"][k]
        w["s_scr"][...] = st
        w["states"] = states

    for w in work:
        inter = [lax.dot_general(w["qa"][k * ch:(k + 1) * ch], w["states"][k], nt_dims, preferred_element_type=F32)
                 for k in range(w["nch"])]
        w["o_ref"][0] = jnp.concatenate(inter, axis=0) + w["intra"]


def _hgrn_kernel(lbl_ref, qf_ref, vf_ref, zf_ref, qb_ref, vb_ref, zb_ref, of_ref, ob_ref, sf_scr, sb_scr, *, layer):
    @pl.when(pl.program_id(2) == 0)
    def _():
        sf_scr[...] = jnp.zeros(sf_scr.shape, F32)
        sb_scr[...] = jnp.zeros(sb_scr.shape, F32)

    if layer == 0:
        lbf = lbb = None
    else:
        lg = lbl_ref[...]
        ex = jnp.exp(lg - jnp.max(lg, axis=1, keepdims=True))
        sm = ex / jnp.sum(ex, axis=1, keepdims=True)
        lb = sm[:, 1]
        for k in range(2, layer + 1):
            lb = lb + sm[:, k]
        lbf, lbb = lb[0:1], lb[1:2]
    _hgrn_tile([(qf_ref, vf_ref, zf_ref, of_ref, sf_scr, lbf, False),
                (qb_ref, vb_ref, zb_ref, ob_ref, sb_scr, lbb, True)])


def _hgrn(cg, lb_logits, *, layer, hg_heads, col_q, col_i, col_ff, col_fb):
    b, l, _ = cg.shape
    t = TILE
    nb = l // t
    depth = lb_logits.shape[1]

    def fwd(col):
        return pl.BlockSpec((1, t, HEAD_LANES), lambda bb, h, i: (bb, i, col + h))

    def bwd(col):
        return pl.BlockSpec((1, t, HEAD_LANES), lambda bb, h, i: (bb, jnp.where(i == 0, 0, nb - i), col + h))

    out_f = pl.BlockSpec((1, t, HEAD_LANES), lambda bb, h, i: (bb, i, h))
    out_b = pl.BlockSpec((1, t, HEAD_LANES), lambda bb, h, i: (bb, jnp.where(i == 0, 0, nb - i), h))
    shp = jax.ShapeDtypeStruct((b, l, hg_heads * HEAD_LANES), F32)
    return pl.pallas_call(
        functools.partial(_hgrn_kernel, layer=layer),
        grid=(b, hg_heads, nb),
        in_specs=[pl.BlockSpec((2, depth, HEAD_LANES), lambda bb, h, i: (0, 0, h)),
                  fwd(col_q), fwd(col_i), fwd(col_ff), bwd(col_q), bwd(col_i), bwd(col_fb)],
        out_specs=[out_f, out_b],
        out_shape=[shp, shp],
        scratch_shapes=[pltpu.VMEM((HEAD_LANES, HEAD_LANES), F32), pltpu.VMEM((HEAD_LANES, HEAD_LANES), F32)],
        compiler_params=_cparams(("arbitrary", "arbitrary", "arbitrary")),
        name="hgrn2",
    )(lb_logits, cg, cg, cg, cg, cg, cg)


def _layer_norm(r, w, b):
    mu = jnp.mean(r, axis=1, keepdims=True)
    var = jnp.mean(jnp.square(r - mu), axis=1, keepdims=True)
    return (r - mu) * lax.rsqrt(var + EPS) * w + b


def _outproj_kernel(x_ref, att_ref, cx_ref, cb_ref, cc_ref, cxp_ref, ccp_ref, cxn_ref, ccn_ref, of_ref, ob_ref,
                    gg_ref, wconv_ref, hnw_ref, wout_ref, mod_ref, lnw_ref, lnb_ref, wr_ref,
                    x1_ref, h2_ref, aff_ref, *, ctx_tiles, alpha, n_experts):
    i = pl.program_id(1)
    nt = pl.num_programs(1)
    tm = x_ref.shape[1]
    aw = att_ref.shape[2]
    cw = cx_ref.shape[2]

    u = cc_ref[0] * cx_ref[0]
    prev_ok = jnp.logical_and(i != 0, i != ctx_tiles)
    next_ok = jnp.logical_and(i != ctx_tiles - 1, i != nt - 1)
    u_before = jnp.where(prev_ok, (ccp_ref[0] * cxp_ref[0])[7:8], 0.0)
    u_after = jnp.where(next_ok, (ccn_ref[0] * cxn_ref[0])[0:1], 0.0)
    row = lax.broadcasted_iota(jnp.int32, (tm, 1), 0)
    u_prev = jnp.where(row == 0, u_before, pltpu.roll(u, 1, 0))
    u_next = jnp.where(row == tm - 1, u_after, pltpu.roll(u, tm - 1, 0))
    wc = wconv_ref[...]
    conv = cb_ref[0] * (u_prev * wc[0:1] + u * wc[1:2] + u_next * wc[2:3])

    o = of_ref[0] + ob_ref[0]
    gg = gg_ref[0]
    recs = []
    for h in range(o.shape[1] // HEAD_LANES):
        oh = o[:, h * HEAD_LANES:(h + 1) * HEAD_LANES]
        ms = jnp.mean(oh * oh, axis=1, keepdims=True)
        recs.append(oh * lax.rsqrt(ms + EPS) * hnw_ref[...] * _silu(gg[:, h * HEAD_LANES:(h + 1) * HEAD_LANES]))
    rec = jnp.concatenate(recs, axis=1)

    y = jnp.dot(att_ref[0], wout_ref[0:aw, :], preferred_element_type=F32)
    y = y + jnp.dot(conv.astype(BF16), wout_ref[aw:aw + cw, :], preferred_element_type=F32)
    y = y + jnp.dot(rec.astype(BF16), wout_ref[aw + cw:, :], preferred_element_type=F32)

    mod = mod_ref[0, 0]
    x1 = _layer_norm(alpha * x_ref[0] + mod[2:3] * y, lnw_ref[...], lnb_ref[...])
    x1_ref[0] = x1
    h2 = x1 * (1.0 + mod[4:5]) + mod[3:4]
    d = h2.shape[1]
    h2_ref[0, :, :d] = h2
    wr = wr_ref[...]
    wr_hi = wr.astype(BF16)
    wr_lo = (wr - wr_hi.astype(F32)).astype(BF16)
    h2_hi = h2.astype(BF16)
    h2_lo = (h2 - h2_hi.astype(F32)).astype(BF16)
    logits = (jnp.dot(h2_hi, wr_hi, preferred_element_type=F32) + jnp.dot(h2_hi, wr_lo, preferred_element_type=F32)
              + jnp.dot(h2_lo, wr_hi, preferred_element_type=F32))
    lane = lax.broadcasted_iota(jnp.int32, logits.shape, 1)
    logits = jnp.where(lane < n_experts, logits, -jnp.inf)
    ex = jnp.exp(logits - jnp.max(logits, axis=1, keepdims=True))
    aff = ex / jnp.sum(ex, axis=1, keepdims=True)
    aff_ref[0] = aff[:, :n_experts]
    h2_ref[0, :, d:] = aff


def _outproj(xa, att, cg, o_f, o_b, w_conv, hgrn_norm_w, w_out_bf, modtab, ln_w, ln_b, w_router_pad, *,
             m_ctx, alpha, n_experts, cw, col_gg):
    b, l, d = xa.shape
    tm = TILE
    aw = att.shape[2]
    hw = o_f.shape[2]
    r8 = tm // 8
    last8 = l // 8 - 1
    ctx_tiles = m_ctx // tm

    def rows(width, col):
        return pl.BlockSpec((1, tm, width), lambda bb, i: (bb, i, col))

    def halo_prev(col):
        return pl.BlockSpec((1, 8, cw), lambda bb, i: (bb, jnp.maximum(i * r8 - 1, 0), col))

    def halo_next(col):
        return pl.BlockSpec((1, 8, cw), lambda bb, i: (bb, jnp.minimum((i + 1) * r8, last8), col))

    def const(shape):
        return pl.BlockSpec(shape, lambda bb, i: (0,) * len(shape))

    kern = functools.partial(_outproj_kernel, ctx_tiles=ctx_tiles, alpha=alpha, n_experts=n_experts)
    return pl.pallas_call(
        kern,
        grid=(b, l // tm),
        in_specs=[rows(d, 0), rows(aw, 0),
                  rows(cw, 0), rows(cw, 1), rows(cw, 2),
                  halo_prev(0), halo_prev(2), halo_next(0), halo_next(2),
                  rows(hw, 0), rows(hw, 0), rows(hw, col_gg),
                  const((3, cw)), const((1, HEAD_LANES)), const((d, d)),
                  pl.BlockSpec((1, 1, 6, d), lambda bb, i: (bb, jnp.minimum(i // ctx_tiles, 1), 0, 0)),
                  const((1, d)), const((1, d)), const((d, HEAD_LANES))],
        out_specs=[rows(d, 0), rows(d + HEAD_LANES, 0), rows(n_experts, 0)],
        out_shape=[jax.ShapeDtypeStruct((b, l, d), F32), jax.ShapeDtypeStruct((b, l, d + HEAD_LANES), F32),
                   jax.ShapeDtypeStruct((b, l, n_experts), F32)],
        compiler_params=_cparams(("arbitrary", "arbitrary")),
        name="outproj_ln_router",
    )(xa, att, cg, cg, cg, cg, cg, cg, cg, o_f, o_b, cg, w_conv, hgrn_norm_w.reshape(1, HEAD_LANES), w_out_bf,
      modtab, ln_w.reshape(1, d), ln_b.reshape(1, d), w_router_pad)


def _topk_kernel(aff_ref, pos_ref, off_ref, idx_ref, cnt_scr, off_smem, dsem, *, m_ctx, cap_ctx, cap_lat):
    aff = aff_ref[0]
    ne, l = aff.shape
    ch = COMBINE_TILE
    nch = l // ch
    ctx_ch = m_ctx // ch
    bits = pltpu.bitcast(aff, jnp.int32)
    lane = lax.broadcasted_iota(jnp.int32, (ne, l), 1)
    in_ctx = lane < m_ctx

    def kth_largest(seg, k):
        def body(_, lohi):
            lo, hi = lohi
            mid = lo + lax.shift_right_logical(hi - lo, 1)
            cnt = jnp.sum(jnp.where(jnp.logical_and(seg, bits >= mid), 1.0, 0.0), axis=1, keepdims=True)
            ge = cnt >= k
            return jnp.where(ge, mid, lo), jnp.where(ge, hi, mid)
        lo0 = jnp.zeros((ne, 1), jnp.int32)
        hi0 = jnp.full((ne, 1), 0x7F800000, jnp.int32)
        return lax.fori_loop(0, 31, body, (lo0, hi0))[0]

    thr = jnp.where(in_ctx, kth_largest(in_ctx, cap_ctx), kth_largest(jnp.logical_not(in_ctx), cap_lat))
    gt = bits > thr
    eq = bits == thr
    gtf = jnp.where(gt, 1.0, 0.0)
    n_gt_ctx = jnp.sum(jnp.where(in_ctx, gtf, 0.0), axis=1, keepdims=True)
    n_gt_lat = jnp.sum(jnp.where(in_ctx, 0.0, gtf), axis=1, keepdims=True)
    need = jnp.where(in_ctx, cap_ctx - n_gt_ctx, cap_lat - n_gt_lat)

    tr = lax.broadcasted_iota(jnp.int32, (ch, ch), 0)
    tc = lax.broadcasted_iota(jnp.int32, (ch, ch), 1)
    tri = jnp.where(tr <= tc, 1.0, 0.0).astype(BF16)

    def seg_prefix(flag):
        excl, bases = [], []
        base = jnp.zeros((ne, 1), F32)
        for c in range(nch):
            if c == ctx_ch:
                base = jnp.zeros((ne, 1), F32)
            fc = flag[:, c * ch:(c + 1) * ch]
            incl = jnp.dot(fc.astype(BF16), tri, preferred_element_type=F32)
            excl.append(base + incl - fc)
            bases.append(base)
            base = base + incl[:, ch - 1:ch]
        bases.append(base)
        return jnp.concatenate(excl, axis=1), bases

    eq_excl, _ = seg_prefix(jnp.where(eq, 1.0, 0.0))
    sel = jnp.logical_or(gt, jnp.logical_and(eq, eq_excl < need))
    self_ = jnp.where(sel, 1.0, 0.0)
    sel_excl, bases = seg_prefix(self_)
    seg_off = jnp.where(in_ctx, 0.0, float(cap_ctx))
    pos_ref[0] = jnp.where(sel, sel_excl + seg_off, -1.0).astype(jnp.int32)

    olane = lax.broadcasted_iota(jnp.int32, (ne, HEAD_LANES), 1)
    off = jnp.zeros((ne, HEAD_LANES), F32)
    for c in range(nch + 1):
        off = jnp.where(olane == c, bases[c] + (0.0 if c < ctx_ch else float(cap_ctx)), off)
    off_ref[0] = off.astype(jnp.int32)

    cnt = sel_excl + self_ + seg_off
    for e in range(ne):
        cnt_scr[e] = cnt[e:e + 1]
    n_slots = idx_ref.shape[1]
    sb = n_slots // IDX_SLOT_BLOCKS
    ilane = lax.broadcasted_iota(jnp.int32, (sb, HEAD_LANES), 1)
    off_copy = pltpu.make_async_copy(off_ref.at[0], off_smem, dsem.at[0])
    off_copy.start()
    off_copy.wait()
    for blk in range(IDX_SLOT_BLOCKS):
        slot = (lax.broadcasted_iota(jnp.int32, (sb, HEAD_LANES), 0) + blk * sb).astype(F32)
        p_lo, p_hi = blk * sb, blk * sb + sb - 1

        def per_expert(e, out):
            def scan(c, lohi):
                return (lohi[0] + jnp.where(off_smem[e, c + 1] <= p_lo, 1, 0),
                        lohi[1] + jnp.where(off_smem[e, c] <= p_hi, 1, 0))
            c_lo, c_hi = lax.fori_loop(0, nch, scan, (jnp.int32(0), jnp.int32(0)))

            def per_chunk(c, acc):
                cnt_row = cnt_scr[e, :, pl.ds(pl.multiple_of(c * ch, ch), ch)]
                return acc + jnp.where(cnt_row <= slot, 1.0, 0.0)
            acc = lax.fori_loop(c_lo, c_hi, per_chunk, jnp.zeros((sb, ch), F32))
            full = (c_lo * ch).astype(F32)
            return jnp.where(ilane == e, jnp.sum(acc, axis=1, keepdims=True) + full, out)
        out = lax.fori_loop(0, ne, per_expert, jnp.zeros((sb, HEAD_LANES), F32))
        idx_ref[0, blk * sb:(blk + 1) * sb, :] = out.astype(jnp.int32)


def _topk(aff_t, *, m_ctx, cap_ctx, cap_lat):
    b, ne, l = aff_t.shape
    n_slots = cap_ctx + cap_lat
    kern = functools.partial(_topk_kernel, m_ctx=m_ctx, cap_ctx=cap_ctx, cap_lat=cap_lat)
    return pl.pallas_call(
        kern,
        grid=(b,),
        in_specs=[pl.BlockSpec((1, ne, l), lambda bb: (bb, 0, 0))],
        out_specs=[pl.BlockSpec((1, ne, l), lambda bb: (bb, 0, 0)),
                   pl.BlockSpec((1, ne, HEAD_LANES), lambda bb: (bb, 0, 0)),
                   pl.BlockSpec((1, n_slots, HEAD_LANES), lambda bb: (bb, 0, 0))],
        out_shape=[jax.ShapeDtypeStruct((b, ne, l), jnp.int32),
                   jax.ShapeDtypeStruct((b, ne, HEAD_LANES), jnp.int32),
                   jax.ShapeDtypeStruct((b, n_slots, HEAD_LANES), jnp.int32)],
        scratch_shapes=[pltpu.VMEM((ne, 1, l), F32), pltpu.SMEM((ne, HEAD_LANES), jnp.int32),
                        pltpu.SemaphoreType.DMA((1,))],
        compiler_params=_cparams(("arbitrary",)),
        name="expert_choice_topk",
    )(aff_t)


def _ffn_kernel(idx_ref, h_hbm, wg_ref, wu_ref, wd_ref, y_ref, xg_scr, xb_scr, gate_scr, acc_scr, sem, *, n_experts,
                steps):
    e, b, f = pl.program_id(0), pl.program_id(1), pl.program_id(2)
    nb, nf = pl.num_programs(1), pl.num_programs(2)
    n_slots, d = xb_scr.shape

    def row_copy(bb, ee, r):
        tok = idx_ref[(bb * n_experts + ee) * n_slots + r]
        return pltpu.make_async_copy(h_hbm.at[bb, pl.ds(tok, 1)], xg_scr.at[pl.ds(r, 1)], sem.at[0])

    @pl.when(f == 0)
    def _():
        @pl.when(jnp.logical_and(e == 0, b == 0))
        def _():
            def issue(r, carry):
                row_copy(b, e, r).start()
                return carry
            lax.fori_loop(0, n_slots, issue, 0, unroll=8)
        pltpu.make_async_copy(h_hbm.at[b, pl.ds(0, n_slots)], xg_scr, sem.at[0]).wait()
        xb_scr[...] = xg_scr[:, :d].astype(BF16)
        tail = xg_scr[:, d:]
        lane = lax.broadcasted_iota(jnp.int32, tail.shape, 1)
        gate_scr[...] = jnp.sum(jnp.where(lane == e, tail, 0.0), axis=1, keepdims=True)
        acc_scr[...] = jnp.zeros(acc_scr.shape, F32)

    wrap = b + 1 == nb
    nxt_b = jnp.where(wrap, 0, b + 1)
    nxt_e = jnp.where(wrap, jnp.where(e + 1 == n_experts, 0, e + 1), e)
    share = n_slots // steps
    for r in range(share):
        row_copy(nxt_b, nxt_e, f * share + r).start()

    wg, wu, wd = wg_ref[0, 0].astype(BF16), wu_ref[0, 0].astype(BF16), wd_ref[0, 0].astype(BF16)
    half = n_slots // 2
    rows = [slice(0, half), slice(half, n_slots)]
    au = [(jnp.dot(xb_scr[r, :], wg, preferred_element_type=F32), jnp.dot(xb_scr[r, :], wu, preferred_element_type=F32))
          for r in rows]
    down = [jnp.dot((_silu(a) * u).astype(BF16), wd, preferred_element_type=F32) for a, u in au]
    for r, dn in zip(rows, down):
        acc_scr[r, :] += dn

    @pl.when(f == nf - 1)
    def _():
        y_ref[0, 0] = (acc_scr[...] * gate_scr[...]).astype(y_ref.dtype)

        @pl.when(jnp.logical_and(e == n_experts - 1, wrap))
        def _():
            pltpu.make_async_copy(h_hbm.at[b, pl.ds(0, n_slots)], xg_scr, sem.at[0]).wait()


def _expert_ffn(idx_flat, h2, w_gate, w_up, w_down, *, n_slots, layer):
    b, l, dx = h2.shape
    _, ne, d, ff = w_gate.shape
    tf = 256
    grid_spec = pltpu.PrefetchScalarGridSpec(
        num_scalar_prefetch=1,
        grid=(ne, b, ff // tf),
        in_specs=[pl.BlockSpec(memory_space=pl.ANY),
                  pl.BlockSpec((1, 1, d, tf), lambda e, bb, f, idx: (layer, e, 0, f)),
                  pl.BlockSpec((1, 1, d, tf), lambda e, bb, f, idx: (layer, e, 0, f)),
                  pl.BlockSpec((1, 1, tf, d), lambda e, bb, f, idx: (layer, e, f, 0))],
        out_specs=pl.BlockSpec((1, 1, n_slots, d), lambda e, bb, f, idx: (bb, e, 0, 0)),
        scratch_shapes=[pltpu.VMEM((n_slots, dx), F32), pltpu.VMEM((n_slots, d), BF16),
                        pltpu.VMEM((n_slots, 1), F32), pltpu.VMEM((n_slots, d), F32),
                        pltpu.SemaphoreType.DMA((1,))])
    assert ff % tf == 0 and n_slots % (ff // tf) == 0
    return pl.pallas_call(
        functools.partial(_ffn_kernel, n_experts=ne, steps=ff // tf),
        grid_spec=grid_spec,
        out_shape=jax.ShapeDtypeStruct((b, ne, n_slots, d), BF16),
        compiler_params=_cparams(("arbitrary", "arbitrary", "arbitrary")),
        name="expert_ffn",
    )(idx_flat, h2, w_gate, w_up, w_down)


def _combine_kernel(off_ref, x1_ref, pos_ref, mod_ref, lnw_ref, lnb_ref, y_hbm, o_ref, win_scr, sem, *,
                    alpha, n_experts, win):
    b, i = pl.program_id(0), pl.program_id(1)
    nb, nt = pl.num_programs(0), pl.num_programs(1)
    tm = x1_ref.shape[1]
    n_slots = y_hbm.shape[2]
    blk = win_scr.shape[1] // n_experts
    step = b * nt + i
    slot = step % 2

    def window(bb, ii, e):
        base = (bb * n_experts + e) * HEAD_LANES + ii
        start = jnp.minimum((off_ref[base] // WIN_ALIGN) * WIN_ALIGN, n_slots - win)
        return pl.multiple_of(start, WIN_ALIGN), off_ref[base + 1] - start <= WIN_SHORT

    def transfer(bb, ii, sl, go):
        for e in range(n_experts):
            start, short = window(bb, ii, e)
            for size, cond in ((WIN_SHORT, short), (win, jnp.logical_not(short))):
                @pl.when(cond)
                def _():
                    go(pltpu.make_async_copy(y_hbm.at[bb, e, pl.ds(start, size)],
                                             win_scr.at[sl, pl.ds(e * blk, size)], sem.at[sl, e]))

    @pl.when(step == 0)
    def _():
        win_scr[...] = jnp.zeros(win_scr.shape, win_scr.dtype)
        transfer(b, i, slot, lambda cp: cp.start())

    @pl.when(step + 1 < nb * nt)
    def _():
        wrap = i + 1 == nt
        transfer(jnp.where(wrap, b + 1, b), jnp.where(wrap, 0, i + 1), 1 - slot, lambda cp: cp.start())

    pos = pos_ref[0]
    scol = lax.broadcasted_iota(jnp.int32, (tm, blk), 1)
    onehot = jnp.concatenate([jnp.where((pos[:, e:e + 1] - window(b, i, e)[0]) == scol, 1.0, 0.0).astype(BF16)
                              for e in range(n_experts)], axis=1)
    transfer(b, i, slot, lambda cp: cp.wait())
    moe = jnp.dot(onehot, win_scr[slot], preferred_element_type=F32)
    mod = mod_ref[0, 0]
    o_ref[0] = _layer_norm(alpha * x1_ref[0] + mod[5:6] * moe, lnw_ref[...], lnb_ref[...])


def _combine(off_flat, x1, pos_tok, modtab, ln_w, ln_b, y, *, m_ctx, alpha, latent_only):
    b, l, d = x1.shape
    ne = pos_tok.shape[2]
    tm = COMBINE_TILE
    win = tm + WIN_ALIGN
    blk = -(-win // HEAD_LANES) * HEAD_LANES
    ctx_tiles = m_ctx // tm

    def rows(width):
        return pl.BlockSpec((1, tm, width), lambda bb, i, off: (bb, i, 0))

    out_rows = l - m_ctx if latent_only else l
    out_spec = (pl.BlockSpec((1, tm, d), lambda bb, i, off: (bb, jnp.maximum(i - ctx_tiles, 0), 0))
                if latent_only else rows(d))

    grid_spec = pltpu.PrefetchScalarGridSpec(
        num_scalar_prefetch=1,
        grid=(b, l // tm),
        in_specs=[rows(d), rows(ne),
                  pl.BlockSpec((1, 1, 6, d), lambda bb, i, off: (bb, jnp.minimum(i // ctx_tiles, 1), 0, 0)),
                  pl.BlockSpec((1, d), lambda bb, i, off: (0, 0)),
                  pl.BlockSpec((1, d), lambda bb, i, off: (0, 0)),
                  pl.BlockSpec(memory_space=pl.ANY)],
        out_specs=out_spec,
        scratch_shapes=[pltpu.VMEM((2, ne * blk, d), BF16), pltpu.SemaphoreType.DMA((2, ne))])
    return pl.pallas_call(
        functools.partial(_combine_kernel, alpha=alpha, n_experts=ne, win=win),
        grid_spec=grid_spec,
        out_shape=jax.ShapeDtypeStruct((b, out_rows, d), F32),
        compiler_params=_cparams(("arbitrary", "arbitrary")),
        name="moe_combine_ln",
    )(off_flat, x1, pos_tok, modtab, ln_w.reshape(1, d), ln_b.reshape(1, d), y)


def kernel(x, c, ctx, c_ctx, w_mod, b_mod, w_in, w_conv, lambda_qk, subln_w, hgrn_lb_logits, hgrn_norm_w, w_out,
           ln_w, ln_b, w_router, w_gate, w_up, w_down):
    bsz, n_lat, d = x.shape
    m_ctx = ctx.shape[1]
    depth = w_mod.shape[0]
    ne = w_router.shape[2]
    aw, cw, hw = d // 2, d // 4, d // 4
    assert m_ctx % TILE == 0 and n_lat % 1024 == 0 and n_lat % GRID_W == 0 and bsz + 1 <= 8
    assert (m_ctx + n_lat) % (8 * 16) == 0 and aw == 1024 and ne <= HEAD_LANES
    alpha = (2.0 * depth) ** 0.25
    cap_ctx = EC_CAPACITY_FACTOR * m_ctx // ne
    cap_lat = EC_CAPACITY_FACTOR * n_lat // ne
    n_slots = cap_ctx + cap_lat
    assert n_slots % WIN_ALIGN == 0 and n_slots >= COMBINE_TILE + WIN_ALIGN
    assert (m_ctx + n_lat) // COMBINE_TILE + 1 <= HEAD_LANES

    cs = jnp.zeros((8, d), F32).at[:bsz].set(c).at[bsz].set(c_ctx)
    mod = _modulation(cs, w_mod, b_mod)
    mod_lat = mod[:, :bsz].reshape(depth, bsz, 1, 6, d)
    mod_ctx = jnp.broadcast_to(mod[:, bsz].reshape(depth, 1, 1, 6, d), (depth, bsz, 1, 6, d))
    modtab = jnp.concatenate([mod_ctx, mod_lat], axis=2)

    tables = _rope_tables(m_ctx, n_lat)
    xa = jnp.concatenate([ctx, x], axis=1)
    w_router_pad = jnp.zeros((depth, d, HEAD_LANES), F32).at[:, :, :ne].set(w_router)
    hg_heads = hw // HEAD_LANES
    cb0 = 3 * cw // HEAD_LANES

    for l in range(depth):
        lambda_init = 0.8 - 0.6 * math.exp(-0.3 * l)
        w_in_bf = w_in[l].astype(BF16)
        qkv = _inproj(xa, modtab[l], w_in_bf, tables, col0=0, ncols=3 * aw, out_dtype=BF16, rope_tiles=2,
                      m_ctx=m_ctx)
        cg = _inproj(xa, modtab[l], w_in_bf, tables, col0=3 * aw, ncols=3 * cw + 5 * hw, out_dtype=F32,
                     rope_tiles=0, m_ctx=m_ctx)
        att = _attention(qkv, lambda_qk[l], subln_w[l], m_ctx=m_ctx, lambda_init=lambda_init)
        o_f, o_b = _hgrn(cg, hgrn_lb_logits, layer=l, hg_heads=hg_heads, col_q=cb0, col_i=cb0 + hg_heads,
                         col_ff=cb0 + 3 * hg_heads, col_fb=cb0 + 4 * hg_heads)
        x1, h2, aff = _outproj(xa, att, cg, o_f, o_b, w_conv[l], hgrn_norm_w[l], w_out[l].astype(BF16), modtab[l],
                               ln_w[l, 0], ln_b[l, 0], w_router_pad[l], m_ctx=m_ctx, alpha=alpha, n_experts=ne,
                               cw=cw, col_gg=(3 * cw + 2 * hw) // hw)
        pos, off, idx = _topk(jnp.swapaxes(aff, 1, 2), m_ctx=m_ctx, cap_ctx=cap_ctx, cap_lat=cap_lat)
        idx_flat = jnp.minimum(jnp.swapaxes(idx[:, :, :ne], 1, 2).reshape(-1), m_ctx + n_lat - 1)
        y = _expert_ffn(idx_flat, h2, w_gate, w_up, w_down, n_slots=n_slots, layer=l)
        xa = _combine(off.reshape(-1), x1, jnp.swapaxes(pos, 1, 2), modtab[l], ln_w[l, 1], ln_b[l, 1], y,
                      m_ctx=m_ctx, alpha=alpha, latent_only=l == depth - 1)
    return xa
```

```python
import functools
import math

import jax
import jax.numpy as jnp
from jax import lax
from jax.experimental import pallas as pl
from jax.experimental.pallas import tpu as pltpu

F32 = jnp.float32
BF16 = jnp.bfloat16
HIGHEST = lax.Precision.HIGHEST

GRID_W = 64
ROPE_THETA = 10000.0
ATT_HEAD_DIM = 64
ROPE_PAIR = ATT_HEAD_DIM // 4
F32_INF_BITS = 0x7F800000
F32_ORDER_BITS = 31
HEAD_LANES = 128
HGRN_CHUNK = 64
HGRN_SAFE_EXP = 80.0
EC_CAPACITY_FACTOR = 2
EPS = 1e-6
TILE = 256
INPROJ_SUB = 256
ATTN_KEY_CHUNK = 1024
ATTN_UNROLL = 4
COMBINE_TILE = 128
IDX_SLOT_BLOCKS = 4
WIN_SHORT = 48
WIN_ALIGN = 16
V7X_VMEM_BYTES = 64 * 1024 * 1024
VMEM_LIMIT = V7X_VMEM_BYTES - 8 * 1024 * 1024


def _cparams(sem):
    return pltpu.CompilerParams(dimension_semantics=sem, vmem_limit_bytes=VMEM_LIMIT)


def _silu(x):
    return x * jax.nn.sigmoid(x)


def _mod_kernel(cs_ref, w_ref, b_ref, o_ref):
    a = _silu(cs_ref[...])
    o_ref[0] = jnp.dot(a, w_ref[0], precision=HIGHEST, preferred_element_type=F32) + b_ref[0]


def _modulation(cs, w_mod, b_mod):
    depth, d, n6 = w_mod.shape
    tn = 1024
    return pl.pallas_call(
        _mod_kernel,
        grid=(depth, n6 // tn),
        in_specs=[pl.BlockSpec((8, d), lambda l, j: (0, 0)),
                  pl.BlockSpec((1, d, tn), lambda l, j: (l, 0, j)),
                  pl.BlockSpec((1, 1, tn), lambda l, j: (l, 0, j))],
        out_specs=pl.BlockSpec((1, 8, tn), lambda l, j: (l, 0, j)),
        out_shape=jax.ShapeDtypeStruct((depth, 8, n6), F32),
        compiler_params=_cparams(("arbitrary", "arbitrary")),
        name="modulation",
    )(cs, w_mod, b_mod.reshape(depth, 1, n6))


def _inproj_kernel(x_ref, mod_ref, w_ref, rc_ref, ra_ref, rb_ref, o_ref, h_scr, *, m_ctx, rope_tiles):
    i = pl.program_id(1)
    j = pl.program_id(2)
    tm = x_ref.shape[1]

    @pl.when(j == 0)
    def _():
        row = i * tm + lax.broadcasted_iota(jnp.int32, (tm, 1), 0)
        is_ctx = row < m_ctx
        mod = mod_ref[0]
        sh = jnp.where(is_ctx, mod[0, 0:1], mod[1, 0:1])
        sc = jnp.where(is_ctx, mod[0, 1:2], mod[1, 1:2])
        h_scr[...] = (x_ref[0] * (1.0 + sc) + sh).astype(BF16)

    h = h_scr[...]
    if rope_tiles:
        rc, ra, rb = rc_ref[0], ra_ref[0], rb_ref[0]
        qscale = jnp.where(j == 0, ATT_HEAD_DIM ** -0.5 * math.log2(math.e), 1.0).astype(F32)

    def finish(c, acc):
        for k in range(INPROJ_SUB // HEAD_LANES):
            blk = acc[:, k * HEAD_LANES:(k + 1) * HEAD_LANES]
            if rope_tiles:
                blk = (blk * rc + pltpu.roll(blk, ROPE_PAIR, 1) * ra
                       + pltpu.roll(blk, HEAD_LANES - ROPE_PAIR, 1) * rb) * qscale
            lo = c * INPROJ_SUB + k * HEAD_LANES
            o_ref[0, :, lo:lo + HEAD_LANES] = blk.astype(o_ref.dtype)

    pending = None
    for c in range(w_ref.shape[1] // INPROJ_SUB):
        acc = jnp.dot(h, w_ref[:, c * INPROJ_SUB:(c + 1) * INPROJ_SUB], preferred_element_type=F32)
        if pending is not None:
            finish(*pending)
        pending = (c, acc)
    finish(*pending)


def _inproj(xa, modtab, w_bf, tables, *, col0, ncols, out_dtype, rope_tiles, m_ctx):
    b, l, d = xa.shape
    tn = 1024
    tm = l // 8
    joff = col0 // tn
    kern = functools.partial(_inproj_kernel, m_ctx=m_ctx, rope_tiles=rope_tiles)
    tab_spec = pl.BlockSpec((1, tm, HEAD_LANES), lambda bb, i, j: (jnp.where(j < rope_tiles, 0, 1), i, 0))
    return pl.pallas_call(
        kern,
        grid=(b, l // tm, ncols // tn),
        in_specs=[pl.BlockSpec((1, tm, d), lambda bb, i, j: (bb, i, 0)),
                  pl.BlockSpec((1, 2, 6, d), lambda bb, i, j: (bb, 0, 0, 0)),
                  pl.BlockSpec((d, tn), lambda bb, i, j: (0, j + joff)),
                  tab_spec, tab_spec, tab_spec],
        out_specs=pl.BlockSpec((1, tm, tn), lambda bb, i, j: (bb, i, j)),
        out_shape=jax.ShapeDtypeStruct((b, l, ncols), out_dtype),
        scratch_shapes=[pltpu.VMEM((tm, d), BF16)],
        compiler_params=_cparams(("arbitrary", "arbitrary", "arbitrary")),
        name="inproj",
    )(xa, modtab, w_bf, *tables)


def _rope_tables(m_ctx, n_lat):
    nf = ATT_HEAD_DIM // 4
    inv = ROPE_THETA ** (-jnp.arange(nf, dtype=F32) / nf)
    rows = n_lat // GRID_W
    row = jnp.repeat(jnp.arange(rows, dtype=F32), GRID_W)
    col = jnp.tile(jnp.arange(GRID_W, dtype=F32), rows)
    ar, ac = row[:, None] * inv, col[:, None] * inv
    cr, sr, cc, sc = jnp.cos(ar), jnp.sin(ar), jnp.cos(ac), jnp.sin(ac)
    z = jnp.zeros_like(sr)
    c64 = jnp.concatenate([cr, cr, cc, cc], axis=1)
    a64 = jnp.concatenate([z, sr, z, sc], axis=1)
    b64 = jnp.concatenate([-sr, z, -sc, z], axis=1)
    reps = HEAD_LANES // ATT_HEAD_DIM

    def full(t64, fill):
        lat = jnp.tile(t64, (1, reps))
        rot = jnp.concatenate([jnp.full((m_ctx, HEAD_LANES), fill, F32), lat], axis=0)
        return jnp.stack([rot, jnp.full(rot.shape, fill, F32)])

    return full(c64, 1.0), full(a64, 0.0), full(b64, 0.0)


def _attn_kernel(lq_ref, sw_ref, q_ref, k_ref, v_ref, o_ref, vt_scr, m0_scr, l0_scr, acc0_scr, m1_scr, l1_scr,
                 acc1_scr, sa0_scr, sa1_scr, sb0_scr, sb1_scr, *, m_ctx, tk, lambda_init):
    i = pl.program_id(2)
    tq = q_ref.shape[1]
    n_lat = k_ref.shape[1] - m_ctx

    @pl.when(i == 0)
    def _():
        vt_scr[...] = v_ref[0].T

    q = q_ref[0]
    lane = lax.broadcasted_iota(jnp.int32, q.shape, 1)
    zero = jnp.zeros_like(q)
    qs = (jnp.where(lane < ATT_HEAD_DIM, q, zero), jnp.where(lane >= ATT_HEAD_DIM, q, zero))
    stats = ((m0_scr, l0_scr, acc0_scr), (m1_scr, l1_scr, acc1_scr))
    for m_scr, l_scr, acc_scr in stats:
        m_scr[...] = jnp.full(m_scr.shape, -jnp.inf, F32)
        l_scr[...] = jnp.zeros(l_scr.shape, F32)
        acc_scr[...] = jnp.zeros(acc_scr.shape, F32)

    def scores(start, size, st_refs):
        kc = k_ref[0, pl.ds(start, size), :]
        cmax = []
        for s in range(2):
            st = lax.dot_general(kc, qs[s], (((1,), (1,)), ((), ())), preferred_element_type=F32)
            st_refs[s][0:size, :] = st
            cmax.append(jnp.max(st, axis=0, keepdims=True))
        return tuple(cmax)

    def accumulate(start, size, st_refs, cmax):
        vt = vt_scr[:, pl.ds(start, size)]
        for s, (m_scr, l_scr, acc_scr) in enumerate(stats):
            m_old = m_scr[...]
            m_new = jnp.maximum(m_old, cmax[s])
            alpha = jnp.exp2(m_old - m_new)
            p = jnp.exp2(st_refs[s][0:size, :] - m_new)
            l_scr[...] = alpha * l_scr[...] + jnp.sum(p, axis=0, keepdims=True)
            acc_scr[...] = alpha * acc_scr[...] + jnp.dot(vt, p.astype(BF16), preferred_element_type=F32)
            m_scr[...] = m_new

    st_a, st_b = (sa0_scr, sa1_scr), (sb0_scr, sb1_scr)

    @pl.when(i * tq < m_ctx)
    def _():
        accumulate(0, m_ctx, st_a, scores(0, m_ctx, st_a))

    @pl.when(i * tq >= m_ctx)
    def _():
        def at(c):
            return pl.multiple_of(m_ctx + c * tk, math.gcd(m_ctx, tk))
        nck = n_lat // tk
        bufs = (st_b, st_a)

        def run(first, count, cm):
            for u in range(count):
                c = first + u
                last = isinstance(c, int) and c + 1 >= nck
                cm_next = None if last else scores(at(c + 1), tk, bufs[(u + 1) % 2])
                accumulate(at(c), tk, bufs[u % 2], cm)
                cm = cm_next
            return cm

        cm_ctx = scores(0, m_ctx, st_a)
        cm = scores(at(0), tk, st_b)
        accumulate(0, m_ctx, st_a, cm_ctx)
        trips = (nck - 1) // ATTN_UNROLL
        cm = lax.fori_loop(0, trips, lambda j, cm: run(j * ATTN_UNROLL, ATTN_UNROLL, cm), cm)
        run(trips * ATTN_UNROLL, nck - trips * ATTN_UNROLL, cm)

    lq = lq_ref[...]
    lam = (jnp.exp(jnp.sum(lq[0:1] * lq[1:2], axis=1, keepdims=True))
           - jnp.exp(jnp.sum(lq[2:3] * lq[3:4], axis=1, keepdims=True)) + lambda_init)
    ot = acc0_scr[...] / l0_scr[...] - lam * (acc1_scr[...] / l1_scr[...])
    ms = jnp.mean(ot * ot, axis=0, keepdims=True)
    o = (ot * lax.rsqrt(ms + EPS)).T * sw_ref[...] * (1.0 - lambda_init)
    o_ref[0] = o.astype(o_ref.dtype)


def _attention(qkv, lambda_qk, subln_w, *, m_ctx, lambda_init):
    b, l, w3 = qkv.shape
    heads = w3 // (3 * HEAD_LANES)
    tq = TILE
    tk = ATTN_KEY_CHUNK
    kern = functools.partial(_attn_kernel, m_ctx=m_ctx, tk=tk, lambda_init=lambda_init)
    return pl.pallas_call(
        kern,
        grid=(b, heads, l // tq),
        in_specs=[pl.BlockSpec((4, ATT_HEAD_DIM), lambda bb, h, i: (0, 0)),
                  pl.BlockSpec((1, HEAD_LANES), lambda bb, h, i: (0, 0)),
                  pl.BlockSpec((1, tq, HEAD_LANES), lambda bb, h, i: (bb, i, h)),
                  pl.BlockSpec((1, l, HEAD_LANES), lambda bb, h, i: (bb, 0, heads + h)),
                  pl.BlockSpec((1, l, HEAD_LANES), lambda bb, h, i: (bb, 0, 2 * heads + h))],
        out_specs=pl.BlockSpec((1, tq, HEAD_LANES), lambda bb, h, i: (bb, i, h)),
        out_shape=jax.ShapeDtypeStruct((b, l, heads * HEAD_LANES), BF16),
        scratch_shapes=[pltpu.VMEM((HEAD_LANES, l), BF16)]
        + [pltpu.VMEM((1, tq), F32), pltpu.VMEM((1, tq), F32), pltpu.VMEM((HEAD_LANES, tq), F32)] * 2
        + [pltpu.VMEM((max(tk, m_ctx), tq), F32)] * 4,
        compiler_params=_cparams(("arbitrary", "arbitrary", "arbitrary")),
        name="diff_attention",
    )(lambda_qk, subln_w.reshape(1, HEAD_LANES), qkv, qkv, qkv)


def _log_sigmoid(z):
    return jnp.minimum(z, 0.0) - jnp.log1p(jnp.exp(-jnp.abs(z)))


def _forget_gate(z, lb):
    ls = _log_sigmoid(z)
    key = jax.nn.sigmoid(-z)
    if lb is None:
        return ls, key
    a = jnp.log(lb)
    b = jnp.log1p(-lb) + ls
    logf = jnp.maximum(a, b) + jnp.log1p(jnp.exp(-jnp.abs(a - b)))
    return logf, (1.0 - lb) * key


def _hgrn_tile(dirs):
    ch = HGRN_CHUNK
    work = []
    for q_ref, v_ref, z_ref, o_ref, s_scr, lb, reverse in dirs:
        t = q_ref.shape[1]
        logf, key = _forget_gate(z_ref[0], lb)
        r = lax.broadcasted_iota(jnp.int32, (t, t), 0)
        c = lax.broadcasted_iota(jnp.int32, (t, t), 1)
        tri = ((r // ch) == (c // ch)) & ((c >= r) if reverse else (c <= r))
        a = jnp.dot(tri.astype(F32), logf, precision=HIGHEST, preferred_element_type=F32)
        nch = t // ch

        def per_chunk(row_of):
            return jnp.concatenate([jnp.broadcast_to(row_of(a[k * ch:(k + 1) * ch]), (ch, a.shape[1]))
                                    for k in range(nch)], axis=0)
        a_mid = per_chunk(lambda ac: ac[ch // 2 - 1:ch // 2])
        a_end = per_chunk((lambda ac: ac[0:1]) if reverse else (lambda ac: ac[ch - 1:ch]))
        work.append(dict(q=q_ref[0], v32=v_ref[0], v=v_ref[0].astype(BF16), key=key, logf=logf, a=a, a_mid=a_mid,
                         a_end=a_end, tri=tri, o_ref=o_ref, s_scr=s_scr, reverse=reverse, nch=nch))

    ends = [jnp.abs(w["a"][r:r + 1] - w["a"][k * ch + ch // 2 - 1:k * ch + ch // 2])
            for w in work for k in range(w["nch"]) for r in (k * ch, (k + 1) * ch - 1)]
    safe = jnp.max(functools.reduce(jnp.maximum, ends)) <= HGRN_SAFE_EXP

    start_states = [w["s_scr"][...] for w in work]
    _hgrn_tile_chunked(work)

    @pl.when(jnp.logical_not(safe))
    def _():
        for w, st0 in zip(work, start_states):
            _hgrn_tile_stepwise(w, st0)


def _hgrn_tile_stepwise(w, st0):
    t, hd = w["q"].shape
    q_t, k_t, f_t, v_t = w["q"].T, w["key"].T, jnp.exp(w["logf"]).T, w["v32"].T
    lane = lax.broadcasted_iota(jnp.int32, (hd, t), 1)
    eye = lax.broadcasted_iota(jnp.int32, (hd, hd), 0) == lax.broadcasted_iota(jnp.int32, (hd, hd), 1)

    def column(x_t, r):
        return jnp.sum(jnp.where(lane == r, x_t, 0.0), axis=1, keepdims=True)

    def as_row(col):
        return jnp.sum(jnp.where(eye, col, 0.0), axis=0, keepdims=True)

    def body(n, carry):
        st, o_t = carry
        r = t - 1 - n if w["reverse"] else n
        st = as_row(column(f_t, r)) * st + column(v_t, r) * as_row(column(k_t, r))
        o_col = jnp.sum(st * as_row(column(q_t, r)), axis=1, keepdims=True)
        return st, jnp.where(lane == r, o_col, o_t)

    st, o_t = lax.fori_loop(0, t, body, (st0, jnp.zeros((hd, t), F32)))
    w["s_scr"][...] = st
    w["o_ref"][0] = o_t.T


def _hgrn_tile_chunked(work):
    ch = HGRN_CHUNK
    nt_dims = (((1,), (1,)), ((), ()))
    for w in work:
        a, nch, a_mid, a_end = w["a"], w["nch"], w["a_mid"], w["a_end"]
        qe = (w["q"] * jnp.exp(a - a_mid)).astype(BF16)
        ke = (w["key"] * jnp.exp(a_mid - a)).astype(BF16)
        kd = (w["key"] * jnp.exp(a_end - a)).astype(BF16)
        w["qa"] = (w["q"] * jnp.exp(a)).astype(BF16)
        w["decay"] = jnp.exp(a_end)
        w["sc"] = lax.dot_general(qe, ke, nt_dims, preferred_element_type=F32)
        w["kv"] = [lax.dot_general(w["v"][k * ch:(k + 1) * ch], kd[k * ch:(k + 1) * ch], (((0,), (0,)), ((), ())),
                                   preferred_element_type=F32) for k in range(nch)]

    for w in work:
        w["intra"] = jnp.dot(jnp.where(w["tri"], w["sc"], 0.0).astype(BF16), w["v"], preferred_element_type=F32)

    for w in work:
        nch = w["nch"]
        st = w["s_scr"][...]
        states = [None] * nch
        for k in (range(nch - 1, -1, -1) if w["reverse"] else range(nch)):
            states[k] = st.astype(BF16)
            row = k * ch if w["reverse"] else (k + 1) * ch - 1
            st = w["decay"][row:row + 1] * st + w["kv"][k]
        w["s_scr"][...] = st
        w["states"] = states

    for w in work:
        inter = [lax.dot_general(w["qa"][k * ch:(k + 1) * ch], w["states"][k], nt_dims, preferred_element_type=F32)
                 for k in range(w["nch"])]
        w["o_ref"][0] = jnp.concatenate(inter, axis=0) + w["intra"]


def _hgrn_kernel(lbl_ref, qf_ref, vf_ref, zf_ref, qb_ref, vb_ref, zb_ref, of_ref, ob_ref, sf_scr, sb_scr, *, layer):
    @pl.when(pl.program_id(2) == 0)
    def _():
        sf_scr[...] = jnp.zeros(sf_scr.shape, F32)
        sb_scr[...] = jnp.zeros(sb_scr.shape, F32)

    if layer == 0:
        lbf = lbb = None
    else:
        lg = lbl_ref[...]
        ex = jnp.exp(lg - jnp.max(lg, axis=1, keepdims=True))
        sm = ex / jnp.sum(ex, axis=1, keepdims=True)
        lb = sm[:, 1]
        for k in range(2, layer + 1):
            lb = lb + sm[:, k]
        lbf, lbb = lb[0:1], lb[1:2]
    _hgrn_tile([(qf_ref, vf_ref, zf_ref, of_ref, sf_scr, lbf, False),
                (qb_ref, vb_ref, zb_ref, ob_ref, sb_scr, lbb, True)])


def _hgrn(cg, lb_logits, *, layer, hg_heads, col_q, col_i, col_ff, col_fb):
    b, l, _ = cg.shape
    t = TILE
    nb = l // t
    depth = lb_logits.shape[1]

    def fwd(col):
        return pl.BlockSpec((1, t, HEAD_LANES), lambda bb, h, i: (bb, i, col + h))

    def bwd(col):
        return pl.BlockSpec((1, t, HEAD_LANES), lambda bb, h, i: (bb, jnp.where(i == 0, 0, nb - i), col + h))

    out_f = pl.BlockSpec((1, t, HEAD_LANES), lambda bb, h, i: (bb, i, h))
    out_b = pl.BlockSpec((1, t, HEAD_LANES), lambda bb, h, i: (bb, jnp.where(i == 0, 0, nb - i), h))
    shp = jax.ShapeDtypeStruct((b, l, hg_heads * HEAD_LANES), F32)
    return pl.pallas_call(
        functools.partial(_hgrn_kernel, layer=layer),
        grid=(b, hg_heads, nb),
        in_specs=[pl.BlockSpec((2, depth, HEAD_LANES), lambda bb, h, i: (0, 0, h)),
                  fwd(col_q), fwd(col_i), fwd(col_ff), bwd(col_q), bwd(col_i), bwd(col_fb)],
        out_specs=[out_f, out_b],
        out_shape=[shp, shp],
        scratch_shapes=[pltpu.VMEM((HEAD_LANES, HEAD_LANES), F32), pltpu.VMEM((HEAD_LANES, HEAD_LANES), F32)],
        compiler_params=_cparams(("arbitrary", "arbitrary", "arbitrary")),
        name="hgrn2",
    )(lb_logits, cg, cg, cg, cg, cg, cg)


def _layer_norm(r, w, b):
    mu = jnp.mean(r, axis=1, keepdims=True)
    var = jnp.mean(jnp.square(r - mu), axis=1, keepdims=True)
    return (r - mu) * lax.rsqrt(var + EPS) * w + b


def _outproj_kernel(x_ref, att_ref, cx_ref, cb_ref, cc_ref, cxp_ref, ccp_ref, cxn_ref, ccn_ref, of_ref, ob_ref,
                    gg_ref, wconv_ref, hnw_ref, wout_ref, mod_ref, lnw_ref, lnb_ref, wr_ref,
                    x1_ref, h2_ref, aff_ref, *, ctx_tiles, alpha, n_experts):
    i = pl.program_id(1)
    nt = pl.num_programs(1)
    tm = x_ref.shape[1]
    aw = att_ref.shape[2]
    cw = cx_ref.shape[2]

    u = cc_ref[0] * cx_ref[0]
    prev_ok = jnp.logical_and(i != 0, i != ctx_tiles)
    next_ok = jnp.logical_and(i != ctx_tiles - 1, i != nt - 1)
    u_before = jnp.where(prev_ok, (ccp_ref[0] * cxp_ref[0])[7:8], 0.0)
    u_after = jnp.where(next_ok, (ccn_ref[0] * cxn_ref[0])[0:1], 0.0)
    row = lax.broadcasted_iota(jnp.int32, (tm, 1), 0)
    u_prev = jnp.where(row == 0, u_before, pltpu.roll(u, 1, 0))
    u_next = jnp.where(row == tm - 1, u_after, pltpu.roll(u, tm - 1, 0))
    wc = wconv_ref[...]
    conv = cb_ref[0] * (u_prev * wc[0:1] + u * wc[1:2] + u_next * wc[2:3])

    o = of_ref[0] + ob_ref[0]
    gg = gg_ref[0]
    recs = []
    for h in range(o.shape[1] // HEAD_LANES):
        oh = o[:, h * HEAD_LANES:(h + 1) * HEAD_LANES]
        ms = jnp.mean(oh * oh, axis=1, keepdims=True)
        recs.append(oh * lax.rsqrt(ms + EPS) * hnw_ref[...] * _silu(gg[:, h * HEAD_LANES:(h + 1) * HEAD_LANES]))
    rec = jnp.concatenate(recs, axis=1)

    y = jnp.dot(att_ref[0], wout_ref[0:aw, :], preferred_element_type=F32)
    y = y + jnp.dot(conv.astype(BF16), wout_ref[aw:aw + cw, :], preferred_element_type=F32)
    y = y + jnp.dot(rec.astype(BF16), wout_ref[aw + cw:, :], preferred_element_type=F32)

    mod = mod_ref[0, 0]
    x1 = _layer_norm(alpha * x_ref[0] + mod[2:3] * y, lnw_ref[...], lnb_ref[...])
    x1_ref[0] = x1
    h2 = x1 * (1.0 + mod[4:5]) + mod[3:4]
    d = h2.shape[1]
    h2_ref[0, :, :d] = h2
    wr = wr_ref[...]
    wr_hi = wr.astype(BF16)
    wr_lo = (wr - wr_hi.astype(F32)).astype(BF16)
    h2_hi = h2.astype(BF16)
    h2_lo = (h2 - h2_hi.astype(F32)).astype(BF16)
    logits = (jnp.dot(h2_hi, wr_hi, preferred_element_type=F32) + jnp.dot(h2_hi, wr_lo, preferred_element_type=F32)
              + jnp.dot(h2_lo, wr_hi, preferred_element_type=F32))
    lane = lax.broadcasted_iota(jnp.int32, logits.shape, 1)
    logits = jnp.where(lane < n_experts, logits, -jnp.inf)
    ex = jnp.exp(logits - jnp.max(logits, axis=1, keepdims=True))
    aff = ex / jnp.sum(ex, axis=1, keepdims=True)
    aff_ref[0] = aff[:, :n_experts]
    h2_ref[0, :, d:] = aff


def _outproj(xa, att, cg, o_f, o_b, w_conv, hgrn_norm_w, w_out_bf, modtab, ln_w, ln_b, w_router_pad, *,
             m_ctx, alpha, n_experts, cw, col_gg):
    b, l, d = xa.shape
    tm = TILE
    aw = att.shape[2]
    hw = o_f.shape[2]
    r8 = tm // 8
    last8 = l // 8 - 1
    ctx_tiles = m_ctx // tm

    def rows(width, col):
        return pl.BlockSpec((1, tm, width), lambda bb, i: (bb, i, col))

    def halo_prev(col):
        return pl.BlockSpec((1, 8, cw), lambda bb, i: (bb, jnp.maximum(i * r8 - 1, 0), col))

    def halo_next(col):
        return pl.BlockSpec((1, 8, cw), lambda bb, i: (bb, jnp.minimum((i + 1) * r8, last8), col))

    def const(shape):
        return pl.BlockSpec(shape, lambda bb, i: (0,) * len(shape))

    kern = functools.partial(_outproj_kernel, ctx_tiles=ctx_tiles, alpha=alpha, n_experts=n_experts)
    return pl.pallas_call(
        kern,
        grid=(b, l // tm),
        in_specs=[rows(d, 0), rows(aw, 0),
                  rows(cw, 0), rows(cw, 1), rows(cw, 2),
                  halo_prev(0), halo_prev(2), halo_next(0), halo_next(2),
                  rows(hw, 0), rows(hw, 0), rows(hw, col_gg),
                  const((3, cw)), const((1, HEAD_LANES)), const((d, d)),
                  pl.BlockSpec((1, 1, 6, d), lambda bb, i: (bb, jnp.minimum(i // ctx_tiles, 1), 0, 0)),
                  const((1, d)), const((1, d)), const((d, HEAD_LANES))],
        out_specs=[rows(d, 0), rows(d + HEAD_LANES, 0), rows(n_experts, 0)],
        out_shape=[jax.ShapeDtypeStruct((b, l, d), F32), jax.ShapeDtypeStruct((b, l, d + HEAD_LANES), F32),
                   jax.ShapeDtypeStruct((b, l, n_experts), F32)],
        compiler_params=_cparams(("arbitrary", "arbitrary")),
        name="outproj_ln_router",
    )(xa, att, cg, cg, cg, cg, cg, cg, cg, o_f, o_b, cg, w_conv, hgrn_norm_w.reshape(1, HEAD_LANES), w_out_bf,
      modtab, ln_w.reshape(1, d), ln_b.reshape(1, d), w_router_pad)


def _topk_kernel(aff_ref, pos_ref, off_ref, idx_ref, cnt_scr, off_smem, dsem, *, m_ctx, cap_ctx, cap_lat):
    aff = aff_ref[0]
    ne, l = aff.shape
    ch = COMBINE_TILE
    nch = l // ch
    ctx_ch = m_ctx // ch
    bits = pltpu.bitcast(aff, jnp.int32)
    lane = lax.broadcasted_iota(jnp.int32, (ne, l), 1)
    in_ctx = lane < m_ctx

    def kth_largest(seg, k):
        def body(_, lohi):
            lo, hi = lohi
            mid = lo + lax.shift_right_logical(hi - lo, 1)
            cnt = jnp.sum(jnp.where(jnp.logical_and(seg, bits >= mid), 1.0, 0.0), axis=1, keepdims=True)
            ge = cnt >= k
            return jnp.where(ge, mid, lo), jnp.where(ge, hi, mid)
        lo0 = jnp.zeros((ne, 1), jnp.int32)
        hi0 = jnp.full((ne, 1), F32_INF_BITS, jnp.int32)
        return lax.fori_loop(0, F32_ORDER_BITS, body, (lo0, hi0))[0]

    thr = jnp.where(in_ctx, kth_largest(in_ctx, cap_ctx), kth_largest(jnp.logical_not(in_ctx), cap_lat))
    gt = bits > thr
    eq = bits == thr
    gtf = jnp.where(gt, 1.0, 0.0)
    n_gt_ctx = jnp.sum(jnp.where(in_ctx, gtf, 0.0), axis=1, keepdims=True)
    n_gt_lat = jnp.sum(jnp.where(in_ctx, 0.0, gtf), axis=1, keepdims=True)
    need = jnp.where(in_ctx, cap_ctx - n_gt_ctx, cap_lat - n_gt_lat)

    tr = lax.broadcasted_iota(jnp.int32, (ch, ch), 0)
    tc = lax.broadcasted_iota(jnp.int32, (ch, ch), 1)
    tri = jnp.where(tr <= tc, 1.0, 0.0).astype(BF16)

    def seg_prefix(flag):
        excl, bases = [], []
        base = jnp.zeros((ne, 1), F32)
        for c in range(nch):
            if c == ctx_ch:
                base = jnp.zeros((ne, 1), F32)
            fc = flag[:, c * ch:(c + 1) * ch]
            incl = jnp.dot(fc.astype(BF16), tri, preferred_element_type=F32)
            excl.append(base + incl - fc)
            bases.append(base)
            base = base + incl[:, ch - 1:ch]
        bases.append(base)
        return jnp.concatenate(excl, axis=1), bases

    eq_excl, _ = seg_prefix(jnp.where(eq, 1.0, 0.0))
    sel = jnp.logical_or(gt, jnp.logical_and(eq, eq_excl < need))
    self_ = jnp.where(sel, 1.0, 0.0)
    sel_excl, bases = seg_prefix(self_)
    seg_off = jnp.where(in_ctx, 0.0, float(cap_ctx))
    pos_ref[0] = jnp.where(sel, sel_excl + seg_off, -1.0).astype(jnp.int32)

    olane = lax.broadcasted_iota(jnp.int32, (ne, HEAD_LANES), 1)
    off = jnp.zeros((ne, HEAD_LANES), F32)
    for c in range(nch + 1):
        off = jnp.where(olane == c, bases[c] + (0.0 if c < ctx_ch else float(cap_ctx)), off)
    off_ref[0] = off.astype(jnp.int32)

    cnt = sel_excl + self_ + seg_off
    for e in range(ne):
        cnt_scr[e] = cnt[e:e + 1]
    n_slots = idx_ref.shape[1]
    sb = n_slots // IDX_SLOT_BLOCKS
    ilane = lax.broadcasted_iota(jnp.int32, (sb, HEAD_LANES), 1)
    off_copy = pltpu.make_async_copy(off_ref.at[0], off_smem, dsem.at[0])
    off_copy.start()
    off_copy.wait()
    for blk in range(IDX_SLOT_BLOCKS):
        slot = (lax.broadcasted_iota(jnp.int32, (sb, HEAD_LANES), 0) + blk * sb).astype(F32)
        p_lo, p_hi = blk * sb, blk * sb + sb - 1

        def per_expert(e, out):
            def scan(c, lohi):
                return (lohi[0] + jnp.where(off_smem[e, c + 1] <= p_lo, 1, 0),
                        lohi[1] + jnp.where(off_smem[e, c] <= p_hi, 1, 0))
            c_lo, c_hi = lax.fori_loop(0, nch, scan, (jnp.int32(0), jnp.int32(0)))

            def per_chunk(c, acc):
                cnt_row = cnt_scr[e, :, pl.ds(pl.multiple_of(c * ch, ch), ch)]
                return acc + jnp.where(cnt_row <= slot, 1.0, 0.0)
            acc = lax.fori_loop(c_lo, c_hi, per_chunk, jnp.zeros((sb, ch), F32))
            full = (c_lo * ch).astype(F32)
            return jnp.where(ilane == e, jnp.sum(acc, axis=1, keepdims=True) + full, out)
        out = lax.fori_loop(0, ne, per_expert, jnp.zeros((sb, HEAD_LANES), F32))
        idx_ref[0, blk * sb:(blk + 1) * sb, :] = out.astype(jnp.int32)


def _topk(aff_t, *, m_ctx, cap_ctx, cap_lat):
    b, ne, l = aff_t.shape
    n_slots = cap_ctx + cap_lat
    kern = functools.partial(_topk_kernel, m_ctx=m_ctx, cap_ctx=cap_ctx, cap_lat=cap_lat)
    return pl.pallas_call(
        kern,
        grid=(b,),
        in_specs=[pl.BlockSpec((1, ne, l), lambda bb: (bb, 0, 0))],
        out_specs=[pl.BlockSpec((1, ne, l), lambda bb: (bb, 0, 0)),
                   pl.BlockSpec((1, ne, HEAD_LANES), lambda bb: (bb, 0, 0)),
                   pl.BlockSpec((1, n_slots, HEAD_LANES), lambda bb: (bb, 0, 0))],
        out_shape=[jax.ShapeDtypeStruct((b, ne, l), jnp.int32),
                   jax.ShapeDtypeStruct((b, ne, HEAD_LANES), jnp.int32),
                   jax.ShapeDtypeStruct((b, n_slots, HEAD_LANES), jnp.int32)],
        scratch_shapes=[pltpu.VMEM((ne, 1, l), F32), pltpu.SMEM((ne, HEAD_LANES), jnp.int32),
                        pltpu.SemaphoreType.DMA((1,))],
        compiler_params=_cparams(("arbitrary",)),
        name="expert_choice_topk",
    )(aff_t)


def _ffn_kernel(idx_ref, h_hbm, wg_ref, wu_ref, wd_ref, y_ref, xg_scr, xb_scr, gate_scr, acc_scr, sem, *, n_experts,
                steps):
    e, b, f = pl.program_id(0), pl.program_id(1), pl.program_id(2)
    nb, nf = pl.num_programs(1), pl.num_programs(2)
    n_slots, d = xb_scr.shape

    def row_copy(bb, ee, r):
        tok = idx_ref[(bb * n_experts + ee) * n_slots + r]
        return pltpu.make_async_copy(h_hbm.at[bb, pl.ds(tok, 1)], xg_scr.at[pl.ds(r, 1)], sem.at[0])

    @pl.when(f == 0)
    def _():
        @pl.when(jnp.logical_and(e == 0, b == 0))
        def _():
            def issue(r, carry):
                row_copy(b, e, r).start()
                return carry
            lax.fori_loop(0, n_slots, issue, 0, unroll=8)
        pltpu.make_async_copy(h_hbm.at[b, pl.ds(0, n_slots)], xg_scr, sem.at[0]).wait()
        xb_scr[...] = xg_scr[:, :d].astype(BF16)
        tail = xg_scr[:, d:]
        lane = lax.broadcasted_iota(jnp.int32, tail.shape, 1)
        gate_scr[...] = jnp.sum(jnp.where(lane == e, tail, 0.0), axis=1, keepdims=True)
        acc_scr[...] = jnp.zeros(acc_scr.shape, F32)

    wrap = b + 1 == nb
    nxt_b = jnp.where(wrap, 0, b + 1)
    nxt_e = jnp.where(wrap, jnp.where(e + 1 == n_experts, 0, e + 1), e)
    share = n_slots // steps
    for r in range(share):
        row_copy(nxt_b, nxt_e, f * share + r).start()

    wg, wu, wd = wg_ref[0, 0].astype(BF16), wu_ref[0, 0].astype(BF16), wd_ref[0, 0].astype(BF16)
    half = n_slots // 2
    rows = [slice(0, half), slice(half, n_slots)]
    au = [(jnp.dot(xb_scr[r, :], wg, preferred_element_type=F32), jnp.dot(xb_scr[r, :], wu, preferred_element_type=F32))
          for r in rows]
    down = [jnp.dot((_silu(a) * u).astype(BF16), wd, preferred_element_type=F32) for a, u in au]
    for r, dn in zip(rows, down):
        acc_scr[r, :] += dn

    @pl.when(f == nf - 1)
    def _():
        y_ref[0, 0] = (acc_scr[...] * gate_scr[...]).astype(y_ref.dtype)

        @pl.when(jnp.logical_and(e == n_experts - 1, wrap))
        def _():
            pltpu.make_async_copy(h_hbm.at[b, pl.ds(0, n_slots)], xg_scr, sem.at[0]).wait()


def _expert_ffn(idx_flat, h2, w_gate, w_up, w_down, *, n_slots, layer):
    b, l, dx = h2.shape
    _, ne, d, ff = w_gate.shape
    tf = 256
    grid_spec = pltpu.PrefetchScalarGridSpec(
        num_scalar_prefetch=1,
        grid=(ne, b, ff // tf),
        in_specs=[pl.BlockSpec(memory_space=pl.ANY),
                  pl.BlockSpec((1, 1, d, tf), lambda e, bb, f, idx: (layer, e, 0, f)),
                  pl.BlockSpec((1, 1, d, tf), lambda e, bb, f, idx: (layer, e, 0, f)),
                  pl.BlockSpec((1, 1, tf, d), lambda e, bb, f, idx: (layer, e, f, 0))],
        out_specs=pl.BlockSpec((1, 1, n_slots, d), lambda e, bb, f, idx: (bb, e, 0, 0)),
        scratch_shapes=[pltpu.VMEM((n_slots, dx), F32), pltpu.VMEM((n_slots, d), BF16),
                        pltpu.VMEM((n_slots, 1), F32), pltpu.VMEM((n_slots, d), F32),
                        pltpu.SemaphoreType.DMA((1,))])
    assert ff % tf == 0 and n_slots % (ff // tf) == 0
    return pl.pallas_call(
        functools.partial(_ffn_kernel, n_experts=ne, steps=ff // tf),
        grid_spec=grid_spec,
        out_shape=jax.ShapeDtypeStruct((b, ne, n_slots, d), BF16),
        compiler_params=_cparams(("arbitrary", "arbitrary", "arbitrary")),
        name="expert_ffn",
    )(idx_flat, h2, w_gate, w_up, w_down)


def _combine_kernel(off_ref, x1_ref, pos_ref, mod_ref, lnw_ref, lnb_ref, y_hbm, o_ref, win_scr, sem, *,
                    alpha, n_experts, win):
    b, i = pl.program_id(0), pl.program_id(1)
    nb, nt = pl.num_programs(0), pl.num_programs(1)
    tm = x1_ref.shape[1]
    n_slots = y_hbm.shape[2]
    blk = win_scr.shape[1] // n_experts
    step = b * nt + i
    slot = step % 2

    def window(bb, ii, e):
        base = (bb * n_experts + e) * HEAD_LANES + ii
        start = jnp.minimum((off_ref[base] // WIN_ALIGN) * WIN_ALIGN, n_slots - win)
        return pl.multiple_of(start, WIN_ALIGN), off_ref[base + 1] - start <= WIN_SHORT

    def transfer(bb, ii, sl, go):
        for e in range(n_experts):
            start, short = window(bb, ii, e)
            for size, cond in ((WIN_SHORT, short), (win, jnp.logical_not(short))):
                @pl.when(cond)
                def _():
                    go(pltpu.make_async_copy(y_hbm.at[bb, e, pl.ds(start, size)],
                                             win_scr.at[sl, pl.ds(e * blk, size)], sem.at[sl, e]))

    @pl.when(step == 0)
    def _():
        win_scr[...] = jnp.zeros(win_scr.shape, win_scr.dtype)
        transfer(b, i, slot, lambda cp: cp.start())

    @pl.when(step + 1 < nb * nt)
    def _():
        wrap = i + 1 == nt
        transfer(jnp.where(wrap, b + 1, b), jnp.where(wrap, 0, i + 1), 1 - slot, lambda cp: cp.start())

    pos = pos_ref[0]
    scol = lax.broadcasted_iota(jnp.int32, (tm, blk), 1)
    onehot = jnp.concatenate([jnp.where((pos[:, e:e + 1] - window(b, i, e)[0]) == scol, 1.0, 0.0).astype(BF16)
                              for e in range(n_experts)], axis=1)
    transfer(b, i, slot, lambda cp: cp.wait())
    moe = jnp.dot(onehot, win_scr[slot], preferred_element_type=F32)
    mod = mod_ref[0, 0]
    o_ref[0] = _layer_norm(alpha * x1_ref[0] + mod[5:6] * moe, lnw_ref[...], lnb_ref[...])


def _combine(off_flat, x1, pos_tok, modtab, ln_w, ln_b, y, *, m_ctx, alpha, latent_only):
    b, l, d = x1.shape
    ne = pos_tok.shape[2]
    tm = COMBINE_TILE
    win = tm + WIN_ALIGN
    blk = -(-win // HEAD_LANES) * HEAD_LANES
    ctx_tiles = m_ctx // tm

    def rows(width):
        return pl.BlockSpec((1, tm, width), lambda bb, i, off: (bb, i, 0))

    out_rows = l - m_ctx if latent_only else l
    out_spec = (pl.BlockSpec((1, tm, d), lambda bb, i, off: (bb, jnp.maximum(i - ctx_tiles, 0), 0))
                if latent_only else rows(d))

    grid_spec = pltpu.PrefetchScalarGridSpec(
        num_scalar_prefetch=1,
        grid=(b, l // tm),
        in_specs=[rows(d), rows(ne),
                  pl.BlockSpec((1, 1, 6, d), lambda bb, i, off: (bb, jnp.minimum(i // ctx_tiles, 1), 0, 0)),
                  pl.BlockSpec((1, d), lambda bb, i, off: (0, 0)),
                  pl.BlockSpec((1, d), lambda bb, i, off: (0, 0)),
                  pl.BlockSpec(memory_space=pl.ANY)],
        out_specs=out_spec,
        scratch_shapes=[pltpu.VMEM((2, ne * blk, d), BF16), pltpu.SemaphoreType.DMA((2, ne))])
    return pl.pallas_call(
        functools.partial(_combine_kernel, alpha=alpha, n_experts=ne, win=win),
        grid_spec=grid_spec,
        out_shape=jax.ShapeDtypeStruct((b, out_rows, d), F32),
        compiler_params=_cparams(("arbitrary", "arbitrary")),
        name="moe_combine_ln",
    )(off_flat, x1, pos_tok, modtab, ln_w.reshape(1, d), ln_b.reshape(1, d), y)


def kernel(x, c, ctx, c_ctx, w_mod, b_mod, w_in, w_conv, lambda_qk, subln_w, hgrn_lb_logits, hgrn_norm_w, w_out,
           ln_w, ln_b, w_router, w_gate, w_up, w_down):
    bsz, n_lat, d = x.shape
    m_ctx = ctx.shape[1]
    depth = w_mod.shape[0]
    ne = w_router.shape[2]
    aw, cw, hw = d // 2, d // 4, d // 4
    assert m_ctx % TILE == 0 and n_lat % ATTN_KEY_CHUNK == 0 and n_lat % GRID_W == 0 and bsz + 1 <= 8
    assert (m_ctx + n_lat) % (8 * 16) == 0 and aw == 1024 and ne <= HEAD_LANES
    alpha = (2.0 * depth) ** 0.25
    cap_ctx = EC_CAPACITY_FACTOR * m_ctx // ne
    cap_lat = EC_CAPACITY_FACTOR * n_lat // ne
    n_slots = cap_ctx + cap_lat
    assert n_slots % WIN_ALIGN == 0 and n_slots >= COMBINE_TILE + WIN_ALIGN
    assert (m_ctx + n_lat) // COMBINE_TILE + 1 <= HEAD_LANES

    cs = jnp.zeros((8, d), F32).at[:bsz].set(c).at[bsz].set(c_ctx)
    mod = _modulation(cs, w_mod, b_mod)
    mod_lat = mod[:, :bsz].reshape(depth, bsz, 1, 6, d)
    mod_ctx = jnp.broadcast_to(mod[:, bsz].reshape(depth, 1, 1, 6, d), (depth, bsz, 1, 6, d))
    modtab = jnp.concatenate([mod_ctx, mod_lat], axis=2)

    tables = _rope_tables(m_ctx, n_lat)
    xa = jnp.concatenate([ctx, x], axis=1)
    w_router_pad = jnp.zeros((depth, d, HEAD_LANES), F32).at[:, :, :ne].set(w_router)
    hg_heads = hw // HEAD_LANES
    cb0 = 3 * cw // HEAD_LANES

    for l in range(depth):
        lambda_init = 0.8 - 0.6 * math.exp(-0.3 * l)
        w_in_bf = w_in[l].astype(BF16)
        qkv = _inproj(xa, modtab[l], w_in_bf, tables, col0=0, ncols=3 * aw, out_dtype=BF16, rope_tiles=2,
                      m_ctx=m_ctx)
        cg = _inproj(xa, modtab[l], w_in_bf, tables, col0=3 * aw, ncols=3 * cw + 5 * hw, out_dtype=F32,
                     rope_tiles=0, m_ctx=m_ctx)
        att = _attention(qkv, lambda_qk[l], subln_w[l], m_ctx=m_ctx, lambda_init=lambda_init)
        o_f, o_b = _hgrn(cg, hgrn_lb_logits, layer=l, hg_heads=hg_heads, col_q=cb0, col_i=cb0 + hg_heads,
                         col_ff=cb0 + 3 * hg_heads, col_fb=cb0 + 4 * hg_heads)
        x1, h2, aff = _outproj(xa, att, cg, o_f, o_b, w_conv[l], hgrn_norm_w[l], w_out[l].astype(BF16), modtab[l],
                               ln_w[l, 0], ln_b[l, 0], w_router_pad[l], m_ctx=m_ctx, alpha=alpha, n_experts=ne,
                               cw=cw, col_gg=(3 * cw + 2 * hw) // hw)
        pos, off, idx = _topk(jnp.swapaxes(aff, 1, 2), m_ctx=m_ctx, cap_ctx=cap_ctx, cap_lat=cap_lat)
        idx_flat = jnp.minimum(jnp.swapaxes(idx[:, :, :ne], 1, 2).reshape(-1), m_ctx + n_lat - 1)
        y = _expert_ffn(idx_flat, h2, w_gate, w_up, w_down, n_slots=n_slots, layer=l)
        xa = _combine(off.reshape(-1), x1, jnp.swapaxes(pos, 1, 2), modtab[l], ln_w[l, 1], ln_b[l, 1], y,
                      m_ctx=m_ctx, alpha=alpha, latent_only=l == depth - 1)
    return xa
```

```python
import functools
import math

import jax
import jax.numpy as jnp
from jax import lax
from jax.experimental import pallas as pl
from jax.experimental.pallas import tpu as pltpu

F32 = jnp.float32
BF16 = jnp.bfloat16
HIGHEST = lax.Precision.HIGHEST

GRID_W = 64
ROPE_THETA = 10000.0
ATT_HEAD_DIM = 64
ROPE_PAIR = ATT_HEAD_DIM // 4
F32_INF_BITS = 0x7F800000
F32_ORDER_BITS = 31
HEAD_LANES = 128
HGRN_CHUNK = 64
HGRN_HEADS_PER_STEP = 2
HGRN_SAFE_EXP = 80.0
EC_CAPACITY_FACTOR = 2
EPS = 1e-6
TILE = 256
INPROJ_SUB = 256
ATTN_KEY_CHUNK = 1024
ATTN_UNROLL = 4
COMBINE_TILE = 128
IDX_SLOT_BLOCKS = 4
WIN_SHORT = 48
WIN_ALIGN = 16
V7X_VMEM_BYTES = 64 * 1024 * 1024
VMEM_LIMIT = V7X_VMEM_BYTES - 8 * 1024 * 1024


def _cparams(sem):
    return pltpu.CompilerParams(dimension_semantics=sem, vmem_limit_bytes=VMEM_LIMIT)


def _silu(x):
    return x * jax.nn.sigmoid(x)


def _mod_kernel(cs_ref, w_ref, b_ref, o_ref):
    a = _silu(cs_ref[...])
    o_ref[0] = jnp.dot(a, w_ref[0], precision=HIGHEST, preferred_element_type=F32) + b_ref[0]


def _modulation(cs, w_mod, b_mod):
    depth, d, n6 = w_mod.shape
    tn = 1024
    return pl.pallas_call(
        _mod_kernel,
        grid=(depth, n6 // tn),
        in_specs=[pl.BlockSpec((8, d), lambda l, j: (0, 0)),
                  pl.BlockSpec((1, d, tn), lambda l, j: (l, 0, j)),
                  pl.BlockSpec((1, 1, tn), lambda l, j: (l, 0, j))],
        out_specs=pl.BlockSpec((1, 8, tn), lambda l, j: (l, 0, j)),
        out_shape=jax.ShapeDtypeStruct((depth, 8, n6), F32),
        compiler_params=_cparams(("arbitrary", "arbitrary")),
        name="modulation",
    )(cs, w_mod, b_mod.reshape(depth, 1, n6))


def _inproj_kernel(x_ref, mod_ref, w_ref, rc_ref, ra_ref, rb_ref, o_ref, h_scr, *, m_ctx, rope_tiles):
    i = pl.program_id(1)
    j = pl.program_id(2)
    tm = x_ref.shape[1]

    @pl.when(j == 0)
    def _():
        row = i * tm + lax.broadcasted_iota(jnp.int32, (tm, 1), 0)
        is_ctx = row < m_ctx
        mod = mod_ref[0]
        sh = jnp.where(is_ctx, mod[0, 0:1], mod[1, 0:1])
        sc = jnp.where(is_ctx, mod[0, 1:2], mod[1, 1:2])
        h_scr[...] = (x_ref[0] * (1.0 + sc) + sh).astype(BF16)

    h = h_scr[...]
    if rope_tiles:
        rc, ra, rb = rc_ref[0], ra_ref[0], rb_ref[0]
        qscale = jnp.where(j == 0, ATT_HEAD_DIM ** -0.5 * math.log2(math.e), 1.0).astype(F32)

    def finish(c, acc):
        for k in range(INPROJ_SUB // HEAD_LANES):
            blk = acc[:, k * HEAD_LANES:(k + 1) * HEAD_LANES]
            if rope_tiles:
                blk = (blk * rc + pltpu.roll(blk, ROPE_PAIR, 1) * ra
                       + pltpu.roll(blk, HEAD_LANES - ROPE_PAIR, 1) * rb) * qscale
            lo = c * INPROJ_SUB + k * HEAD_LANES
            o_ref[0, :, lo:lo + HEAD_LANES] = blk.astype(o_ref.dtype)

    pending = None
    for c in range(w_ref.shape[1] // INPROJ_SUB):
        acc = jnp.dot(h, w_ref[:, c * INPROJ_SUB:(c + 1) * INPROJ_SUB], preferred_element_type=F32)
        if pending is not None:
            finish(*pending)
        pending = (c, acc)
    finish(*pending)


def _inproj(xa, modtab, w_bf, tables, *, col0, ncols, out_dtype, rope_tiles, m_ctx):
    b, l, d = xa.shape
    tn = 1024
    tm = l // 8
    joff = col0 // tn
    kern = functools.partial(_inproj_kernel, m_ctx=m_ctx, rope_tiles=rope_tiles)
    tab_spec = pl.BlockSpec((1, tm, HEAD_LANES), lambda bb, i, j: (jnp.where(j < rope_tiles, 0, 1), i, 0))
    return pl.pallas_call(
        kern,
        grid=(b, l // tm, ncols // tn),
        in_specs=[pl.BlockSpec((1, tm, d), lambda bb, i, j: (bb, i, 0)),
                  pl.BlockSpec((1, 2, 6, d), lambda bb, i, j: (bb, 0, 0, 0)),
                  pl.BlockSpec((d, tn), lambda bb, i, j: (0, j + joff)),
                  tab_spec, tab_spec, tab_spec],
        out_specs=pl.BlockSpec((1, tm, tn), lambda bb, i, j: (bb, i, j)),
        out_shape=jax.ShapeDtypeStruct((b, l, ncols), out_dtype),
        scratch_shapes=[pltpu.VMEM((tm, d), BF16)],
        compiler_params=_cparams(("arbitrary", "arbitrary", "arbitrary")),
        name="inproj",
    )(xa, modtab, w_bf, *tables)


def _rope_tables(m_ctx, n_lat):
    nf = ATT_HEAD_DIM // 4
    inv = ROPE_THETA ** (-jnp.arange(nf, dtype=F32) / nf)
    rows = n_lat // GRID_W
    row = jnp.repeat(jnp.arange(rows, dtype=F32), GRID_W)
    col = jnp.tile(jnp.arange(GRID_W, dtype=F32), rows)
    ar, ac = row[:, None] * inv, col[:, None] * inv
    cr, sr, cc, sc = jnp.cos(ar), jnp.sin(ar), jnp.cos(ac), jnp.sin(ac)
    z = jnp.zeros_like(sr)
    c64 = jnp.concatenate([cr, cr, cc, cc], axis=1)
    a64 = jnp.concatenate([z, sr, z, sc], axis=1)
    b64 = jnp.concatenate([-sr, z, -sc, z], axis=1)
    reps = HEAD_LANES // ATT_HEAD_DIM

    def full(t64, fill):
        lat = jnp.tile(t64, (1, reps))
        rot = jnp.concatenate([jnp.full((m_ctx, HEAD_LANES), fill, F32), lat], axis=0)
        return jnp.stack([rot, jnp.full(rot.shape, fill, F32)])

    return full(c64, 1.0), full(a64, 0.0), full(b64, 0.0)


def _attn_kernel(lq_ref, sw_ref, q_ref, k_ref, v_ref, o_ref, vt_scr, m0_scr, l0_scr, acc0_scr, m1_scr, l1_scr,
                 acc1_scr, sa0_scr, sa1_scr, sb0_scr, sb1_scr, *, m_ctx, tk, lambda_init):
    i = pl.program_id(2)
    tq = q_ref.shape[1]
    n_lat = k_ref.shape[1] - m_ctx

    @pl.when(i == 0)
    def _():
        vt_scr[...] = v_ref[0].T

    q = q_ref[0]
    lane = lax.broadcasted_iota(jnp.int32, q.shape, 1)
    zero = jnp.zeros_like(q)
    qs = (jnp.where(lane < ATT_HEAD_DIM, q, zero), jnp.where(lane >= ATT_HEAD_DIM, q, zero))
    stats = ((m0_scr, l0_scr, acc0_scr), (m1_scr, l1_scr, acc1_scr))
    for m_scr, l_scr, acc_scr in stats:
        m_scr[...] = jnp.full(m_scr.shape, -jnp.inf, F32)
        l_scr[...] = jnp.zeros(l_scr.shape, F32)
        acc_scr[...] = jnp.zeros(acc_scr.shape, F32)

    def scores(start, size, st_refs):
        kc = k_ref[0, pl.ds(start, size), :]
        cmax = []
        for s in range(2):
            st = lax.dot_general(kc, qs[s], (((1,), (1,)), ((), ())), preferred_element_type=F32)
            st_refs[s][0:size, :] = st
            cmax.append(jnp.max(st, axis=0, keepdims=True))
        return tuple(cmax)

    def accumulate(start, size, st_refs, cmax):
        vt = vt_scr[:, pl.ds(start, size)]
        for s, (m_scr, l_scr, acc_scr) in enumerate(stats):
            m_old = m_scr[...]
            m_new = jnp.maximum(m_old, cmax[s])
            alpha = jnp.exp2(m_old - m_new)
            p = jnp.exp2(st_refs[s][0:size, :] - m_new)
            l_scr[...] = alpha * l_scr[...] + jnp.sum(p, axis=0, keepdims=True)
            acc_scr[...] = alpha * acc_scr[...] + jnp.dot(vt, p.astype(BF16), preferred_element_type=F32)
            m_scr[...] = m_new

    st_a, st_b = (sa0_scr, sa1_scr), (sb0_scr, sb1_scr)

    @pl.when(i * tq < m_ctx)
    def _():
        accumulate(0, m_ctx, st_a, scores(0, m_ctx, st_a))

    @pl.when(i * tq >= m_ctx)
    def _():
        def at(c):
            return pl.multiple_of(m_ctx + c * tk, math.gcd(m_ctx, tk))
        nck = n_lat // tk
        bufs = (st_b, st_a)

        def run(first, count, cm):
            for u in range(count):
                c = first + u
                last = isinstance(c, int) and c + 1 >= nck
                cm_next = None if last else scores(at(c + 1), tk, bufs[(u + 1) % 2])
                accumulate(at(c), tk, bufs[u % 2], cm)
                cm = cm_next
            return cm

        cm_ctx = scores(0, m_ctx, st_a)
        cm = scores(at(0), tk, st_b)
        accumulate(0, m_ctx, st_a, cm_ctx)
        trips = (nck - 1) // ATTN_UNROLL
        cm = lax.fori_loop(0, trips, lambda j, cm: run(j * ATTN_UNROLL, ATTN_UNROLL, cm), cm)
        run(trips * ATTN_UNROLL, nck - trips * ATTN_UNROLL, cm)

    lq = lq_ref[...]
    lam = (jnp.exp(jnp.sum(lq[0:1] * lq[1:2], axis=1, keepdims=True))
           - jnp.exp(jnp.sum(lq[2:3] * lq[3:4], axis=1, keepdims=True)) + lambda_init)
    ot = acc0_scr[...] / l0_scr[...] - lam * (acc1_scr[...] / l1_scr[...])
    ms = jnp.mean(ot * ot, axis=0, keepdims=True)
    o = (ot * lax.rsqrt(ms + EPS)).T * sw_ref[...] * (1.0 - lambda_init)
    o_ref[0] = o.astype(o_ref.dtype)


def _attention(qkv, lambda_qk, subln_w, *, m_ctx, lambda_init):
    b, l, w3 = qkv.shape
    heads = w3 // (3 * HEAD_LANES)
    tq = TILE
    tk = ATTN_KEY_CHUNK
    kern = functools.partial(_attn_kernel, m_ctx=m_ctx, tk=tk, lambda_init=lambda_init)
    return pl.pallas_call(
        kern,
        grid=(b, heads, l // tq),
        in_specs=[pl.BlockSpec((4, ATT_HEAD_DIM), lambda bb, h, i: (0, 0)),
                  pl.BlockSpec((1, HEAD_LANES), lambda bb, h, i: (0, 0)),
                  pl.BlockSpec((1, tq, HEAD_LANES), lambda bb, h, i: (bb, i, h)),
                  pl.BlockSpec((1, l, HEAD_LANES), lambda bb, h, i: (bb, 0, heads + h)),
                  pl.BlockSpec((1, l, HEAD_LANES), lambda bb, h, i: (bb, 0, 2 * heads + h))],
        out_specs=pl.BlockSpec((1, tq, HEAD_LANES), lambda bb, h, i: (bb, i, h)),
        out_shape=jax.ShapeDtypeStruct((b, l, heads * HEAD_LANES), BF16),
        scratch_shapes=[pltpu.VMEM((HEAD_LANES, l), BF16)]
        + [pltpu.VMEM((1, tq), F32), pltpu.VMEM((1, tq), F32), pltpu.VMEM((HEAD_LANES, tq), F32)] * 2
        + [pltpu.VMEM((max(tk, m_ctx), tq), F32)] * 4,
        compiler_params=_cparams(("arbitrary", "arbitrary", "arbitrary")),
        name="diff_attention",
    )(lambda_qk, subln_w.reshape(1, HEAD_LANES), qkv, qkv, qkv)


def _log_sigmoid(z):
    return jnp.minimum(z, 0.0) - jnp.log1p(jnp.exp(-jnp.abs(z)))


def _forget_gate(z, lb):
    ls = _log_sigmoid(z)
    key = jax.nn.sigmoid(-z)
    if lb is None:
        return ls, key
    a = jnp.log(lb)
    b = jnp.log1p(-lb) + ls
    logf = jnp.maximum(a, b) + jnp.log1p(jnp.exp(-jnp.abs(a - b)))
    return logf, (1.0 - lb) * key


def _hgrn_tile(dirs):
    ch = HGRN_CHUNK
    work = []
    for q_ref, v_ref, z_ref, o_ref, s_scr, lb, reverse in dirs:
        t = q_ref.shape[1]
        logf, key = _forget_gate(z_ref[0], lb)
        r = lax.broadcasted_iota(jnp.int32, (t, t), 0)
        c = lax.broadcasted_iota(jnp.int32, (t, t), 1)
        tri = ((r // ch) == (c // ch)) & ((c >= r) if reverse else (c <= r))
        a = jnp.dot(tri.astype(F32), logf, precision=HIGHEST, preferred_element_type=F32)
        nch = t // ch

        def per_chunk(row_of):
            return jnp.concatenate([jnp.broadcast_to(row_of(a[k * ch:(k + 1) * ch]), (ch, a.shape[1]))
                                    for k in range(nch)], axis=0)
        a_mid = per_chunk(lambda ac: ac[ch // 2 - 1:ch // 2])
        a_end = per_chunk((lambda ac: ac[0:1]) if reverse else (lambda ac: ac[ch - 1:ch]))
        work.append(dict(q=q_ref[0], v32=v_ref[0], v=v_ref[0].astype(BF16), key=key, logf=logf, a=a, a_mid=a_mid,
                         a_end=a_end, tri=tri, o_ref=o_ref, s_scr=s_scr, reverse=reverse, nch=nch))

    ends = [jnp.abs(w["a"][r:r + 1] - w["a"][k * ch + ch // 2 - 1:k * ch + ch // 2])
            for w in work for k in range(w["nch"]) for r in (k * ch, (k + 1) * ch - 1)]
    safe = jnp.max(functools.reduce(jnp.maximum, ends)) <= HGRN_SAFE_EXP

    start_states = [w["s_scr"][...] for w in work]
    _hgrn_tile_chunked(work)

    @pl.when(jnp.logical_not(safe))
    def _():
        for w, st0 in zip(work, start_states):
            _hgrn_tile_stepwise(w, st0)


def _hgrn_tile_stepwise(w, st0):
    t, hd = w["q"].shape
    q_t, k_t, f_t, v_t = w["q"].T, w["key"].T, jnp.exp(w["logf"]).T, w["v32"].T
    lane = lax.broadcasted_iota(jnp.int32, (hd, t), 1)
    eye = lax.broadcasted_iota(jnp.int32, (hd, hd), 0) == lax.broadcasted_iota(jnp.int32, (hd, hd), 1)

    def column(x_t, r):
        return jnp.sum(jnp.where(lane == r, x_t, 0.0), axis=1, keepdims=True)

    def as_row(col):
        return jnp.sum(jnp.where(eye, col, 0.0), axis=0, keepdims=True)

    def body(n, carry):
        st, o_t = carry
        r = t - 1 - n if w["reverse"] else n
        st = as_row(column(f_t, r)) * st + column(v_t, r) * as_row(column(k_t, r))
        o_col = jnp.sum(st * as_row(column(q_t, r)), axis=1, keepdims=True)
        return st, jnp.where(lane == r, o_col, o_t)

    st, o_t = lax.fori_loop(0, t, body, (st0, jnp.zeros((hd, t), F32)))
    w["s_scr"][...] = st
    w["o_ref"][0] = o_t.T


def _hgrn_tile_chunked(work):
    ch = HGRN_CHUNK
    nt_dims = (((1,), (1,)), ((), ()))
    for w in work:
        a, nch, a_mid, a_end = w["a"], w["nch"], w["a_mid"], w["a_end"]
        qe = (w["q"] * jnp.exp(a - a_mid)).astype(BF16)
        ke = (w["key"] * jnp.exp(a_mid - a)).astype(BF16)
        kd = (w["key"] * jnp.exp(a_end - a)).astype(BF16)
        w["qa"] = (w["q"] * jnp.exp(a)).astype(BF16)
        w["decay"] = jnp.exp(a_end)
        w["sc"] = lax.dot_general(qe, ke, nt_dims, preferred_element_type=F32)
        w["kv"] = [lax.dot_general(w["v"][k * ch:(k + 1) * ch], kd[k * ch:(k + 1) * ch], (((0,), (0,)), ((), ())),
                                   preferred_element_type=F32) for k in range(nch)]

    for w in work:
        w["intra"] = jnp.dot(jnp.where(w["tri"], w["sc"], 0.0).astype(BF16), w["v"], preferred_element_type=F32)

    for w in work:
        nch = w["nch"]
        st = w["s_scr"][...]
        states = [None] * nch
        for k in (range(nch - 1, -1, -1) if w["reverse"] else range(nch)):
            states[k] = st.astype(BF16)
            row = k * ch if w["reverse"] else (k + 1) * ch - 1
            st = w["decay"][row:row + 1] * st + w["kv"][k]
        w["s_scr"][...] = st
        w["states"] = states

    for w in work:
        inter = [lax.dot_general(w["qa"][k * ch:(k + 1) * ch], w["states"][k], nt_dims, preferred_element_type=F32)
                 for k in range(w["nch"])]
        w["o_ref"][0] = jnp.concatenate(inter, axis=0) + w["intra"]


def _hgrn_kernel(lbl_ref, qf_ref, vf_ref, zf_ref, qb_ref, vb_ref, zb_ref, of_ref, ob_ref, sf_scr, sb_scr, *, layer):
    @pl.when(pl.program_id(2) == 0)
    def _():
        sf_scr[...] = jnp.zeros(sf_scr.shape, F32)
        sb_scr[...] = jnp.zeros(sb_scr.shape, F32)

    if layer == 0:
        lb = None
    else:
        lg = lbl_ref[...]
        ex = jnp.exp(lg - jnp.max(lg, axis=1, keepdims=True))
        sm = ex / jnp.sum(ex, axis=1, keepdims=True)
        lb = sm[:, 1]
        for k in range(2, layer + 1):
            lb = lb + sm[:, k]

    dirs = []
    for h in range(qf_ref.shape[2] // HEAD_LANES):
        lanes = slice(h * HEAD_LANES, (h + 1) * HEAD_LANES)

        def head(ref):
            return ref.at[:, :, lanes]
        dirs.append((head(qf_ref), head(vf_ref), head(zf_ref), head(of_ref), sf_scr.at[h],
                     None if lb is None else lb[0:1, lanes], False))
        dirs.append((head(qb_ref), head(vb_ref), head(zb_ref), head(ob_ref), sb_scr.at[h],
                     None if lb is None else lb[1:2, lanes], True))
    _hgrn_tile(dirs)


def _hgrn(cg, lb_logits, *, layer, hg_heads, col_q, col_i, col_ff, col_fb):
    b, l, _ = cg.shape
    t = TILE
    nb = l // t
    depth = lb_logits.shape[1]
    hps = HGRN_HEADS_PER_STEP
    wide = hps * HEAD_LANES
    assert hg_heads % hps == 0 and all(c % hps == 0 for c in (col_q, col_i, col_ff, col_fb))

    def fwd(col):
        return pl.BlockSpec((1, t, wide), lambda bb, h, i: (bb, i, col // hps + h))

    def bwd(col):
        return pl.BlockSpec((1, t, wide), lambda bb, h, i: (bb, jnp.where(i == 0, 0, nb - i), col // hps + h))

    out_f = pl.BlockSpec((1, t, wide), lambda bb, h, i: (bb, i, h))
    out_b = pl.BlockSpec((1, t, wide), lambda bb, h, i: (bb, jnp.where(i == 0, 0, nb - i), h))
    shp = jax.ShapeDtypeStruct((b, l, hg_heads * HEAD_LANES), F32)
    state = pltpu.VMEM((hps, HEAD_LANES, HEAD_LANES), F32)
    return pl.pallas_call(
        functools.partial(_hgrn_kernel, layer=layer),
        grid=(b, hg_heads // hps, nb),
        in_specs=[pl.BlockSpec((2, depth, wide), lambda bb, h, i: (0, 0, h)),
                  fwd(col_q), fwd(col_i), fwd(col_ff), bwd(col_q), bwd(col_i), bwd(col_fb)],
        out_specs=[out_f, out_b],
        out_shape=[shp, shp],
        scratch_shapes=[state, state],
        compiler_params=_cparams(("arbitrary", "arbitrary", "arbitrary")),
        name="hgrn2",
    )(lb_logits, cg, cg, cg, cg, cg, cg)


def _layer_norm(r, w, b):
    mu = jnp.mean(r, axis=1, keepdims=True)
    var = jnp.mean(jnp.square(r - mu), axis=1, keepdims=True)
    return (r - mu) * lax.rsqrt(var + EPS) * w + b


def _outproj_kernel(x_ref, att_ref, cx_ref, cb_ref, cc_ref, cxp_ref, ccp_ref, cxn_ref, ccn_ref, of_ref, ob_ref,
                    gg_ref, wconv_ref, hnw_ref, wout_ref, mod_ref, lnw_ref, lnb_ref, wr_ref,
                    x1_ref, h2_ref, aff_ref, *, ctx_tiles, alpha, n_experts):
    i = pl.program_id(1)
    nt = pl.num_programs(1)
    tm = x_ref.shape[1]
    aw = att_ref.shape[2]
    cw = cx_ref.shape[2]

    u = cc_ref[0] * cx_ref[0]
    prev_ok = jnp.logical_and(i != 0, i != ctx_tiles)
    next_ok = jnp.logical_and(i != ctx_tiles - 1, i != nt - 1)
    u_before = jnp.where(prev_ok, (ccp_ref[0] * cxp_ref[0])[7:8], 0.0)
    u_after = jnp.where(next_ok, (ccn_ref[0] * cxn_ref[0])[0:1], 0.0)
    row = lax.broadcasted_iota(jnp.int32, (tm, 1), 0)
    u_prev = jnp.where(row == 0, u_before, pltpu.roll(u, 1, 0))
    u_next = jnp.where(row == tm - 1, u_after, pltpu.roll(u, tm - 1, 0))
    wc = wconv_ref[...]
    conv = cb_ref[0] * (u_prev * wc[0:1] + u * wc[1:2] + u_next * wc[2:3])

    o = of_ref[0] + ob_ref[0]
    gg = gg_ref[0]
    recs = []
    for h in range(o.shape[1] // HEAD_LANES):
        oh = o[:, h * HEAD_LANES:(h + 1) * HEAD_LANES]
        ms = jnp.mean(oh * oh, axis=1, keepdims=True)
        recs.append(oh * lax.rsqrt(ms + EPS) * hnw_ref[...] * _silu(gg[:, h * HEAD_LANES:(h + 1) * HEAD_LANES]))
    rec = jnp.concatenate(recs, axis=1)

    y = jnp.dot(att_ref[0], wout_ref[0:aw, :], preferred_element_type=F32)
    y = y + jnp.dot(conv.astype(BF16), wout_ref[aw:aw + cw, :], preferred_element_type=F32)
    y = y + jnp.dot(rec.astype(BF16), wout_ref[aw + cw:, :], preferred_element_type=F32)

    mod = mod_ref[0, 0]
    x1 = _layer_norm(alpha * x_ref[0] + mod[2:3] * y, lnw_ref[...], lnb_ref[...])
    x1_ref[0] = x1
    h2 = x1 * (1.0 + mod[4:5]) + mod[3:4]
    d = h2.shape[1]
    h2_ref[0, :, :d] = h2
    wr = wr_ref[...]
    wr_hi = wr.astype(BF16)
    wr_lo = (wr - wr_hi.astype(F32)).astype(BF16)
    h2_hi = h2.astype(BF16)
    h2_lo = (h2 - h2_hi.astype(F32)).astype(BF16)
    logits = (jnp.dot(h2_hi, wr_hi, preferred_element_type=F32) + jnp.dot(h2_hi, wr_lo, preferred_element_type=F32)
              + jnp.dot(h2_lo, wr_hi, preferred_element_type=F32))
    lane = lax.broadcasted_iota(jnp.int32, logits.shape, 1)
    logits = jnp.where(lane < n_experts, logits, -jnp.inf)
    ex = jnp.exp(logits - jnp.max(logits, axis=1, keepdims=True))
    aff = ex / jnp.sum(ex, axis=1, keepdims=True)
    aff_ref[0] = aff[:, :n_experts]
    h2_ref[0, :, d:] = aff


def _outproj(xa, att, cg, o_f, o_b, w_conv, hgrn_norm_w, w_out_bf, modtab, ln_w, ln_b, w_router_pad, *,
             m_ctx, alpha, n_experts, cw, col_gg):
    b, l, d = xa.shape
    tm = TILE
    aw = att.shape[2]
    hw = o_f.shape[2]
    r8 = tm // 8
    last8 = l // 8 - 1
    ctx_tiles = m_ctx // tm

    def rows(width, col):
        return pl.BlockSpec((1, tm, width), lambda bb, i: (bb, i, col))

    def halo_prev(col):
        return pl.BlockSpec((1, 8, cw), lambda bb, i: (bb, jnp.maximum(i * r8 - 1, 0), col))

    def halo_next(col):
        return pl.BlockSpec((1, 8, cw), lambda bb, i: (bb, jnp.minimum((i + 1) * r8, last8), col))

    def const(shape):
        return pl.BlockSpec(shape, lambda bb, i: (0,) * len(shape))

    kern = functools.partial(_outproj_kernel, ctx_tiles=ctx_tiles, alpha=alpha, n_experts=n_experts)
    return pl.pallas_call(
        kern,
        grid=(b, l // tm),
        in_specs=[rows(d, 0), rows(aw, 0),
                  rows(cw, 0), rows(cw, 1), rows(cw, 2),
                  halo_prev(0), halo_prev(2), halo_next(0), halo_next(2),
                  rows(hw, 0), rows(hw, 0), rows(hw, col_gg),
                  const((3, cw)), const((1, HEAD_LANES)), const((d, d)),
                  pl.BlockSpec((1, 1, 6, d), lambda bb, i: (bb, jnp.minimum(i // ctx_tiles, 1), 0, 0)),
                  const((1, d)), const((1, d)), const((d, HEAD_LANES))],
        out_specs=[rows(d, 0), rows(d + HEAD_LANES, 0), rows(n_experts, 0)],
        out_shape=[jax.ShapeDtypeStruct((b, l, d), F32), jax.ShapeDtypeStruct((b, l, d + HEAD_LANES), F32),
                   jax.ShapeDtypeStruct((b, l, n_experts), F32)],
        compiler_params=_cparams(("arbitrary", "arbitrary")),
        name="outproj_ln_router",
    )(xa, att, cg, cg, cg, cg, cg, cg, cg, o_f, o_b, cg, w_conv, hgrn_norm_w.reshape(1, HEAD_LANES), w_out_bf,
      modtab, ln_w.reshape(1, d), ln_b.reshape(1, d), w_router_pad)


def _topk_kernel(aff_ref, pos_ref, off_ref, idx_ref, cnt_scr, off_smem, dsem, *, m_ctx, cap_ctx, cap_lat):
    aff = aff_ref[0]
    ne, l = aff.shape
    ch = COMBINE_TILE
    nch = l // ch
    ctx_ch = m_ctx // ch
    bits = pltpu.bitcast(aff, jnp.int32)
    lane = lax.broadcasted_iota(jnp.int32, (ne, l), 1)
    in_ctx = lane < m_ctx

    def kth_largest(seg, k):
        def body(_, lohi):
            lo, hi = lohi
            mid = lo + lax.shift_right_logical(hi - lo, 1)
            cnt = jnp.sum(jnp.where(jnp.logical_and(seg, bits >= mid), 1.0, 0.0), axis=1, keepdims=True)
            ge = cnt >= k
            return jnp.where(ge, mid, lo), jnp.where(ge, hi, mid)
        lo0 = jnp.zeros((ne, 1), jnp.int32)
        hi0 = jnp.full((ne, 1), F32_INF_BITS, jnp.int32)
        return lax.fori_loop(0, F32_ORDER_BITS, body, (lo0, hi0))[0]

    thr = jnp.where(in_ctx, kth_largest(in_ctx, cap_ctx), kth_largest(jnp.logical_not(in_ctx), cap_lat))
    gt = bits > thr
    eq = bits == thr
    gtf = jnp.where(gt, 1.0, 0.0)
    n_gt_ctx = jnp.sum(jnp.where(in_ctx, gtf, 0.0), axis=1, keepdims=True)
    n_gt_lat = jnp.sum(jnp.where(in_ctx, 0.0, gtf), axis=1, keepdims=True)
    need = jnp.where(in_ctx, cap_ctx - n_gt_ctx, cap_lat - n_gt_lat)

    tr = lax.broadcasted_iota(jnp.int32, (ch, ch), 0)
    tc = lax.broadcasted_iota(jnp.int32, (ch, ch), 1)
    tri = jnp.where(tr <= tc, 1.0, 0.0).astype(BF16)

    def seg_prefix(flag):
        excl, bases = [], []
        base = jnp.zeros((ne, 1), F32)
        for c in range(nch):
            if c == ctx_ch:
                base = jnp.zeros((ne, 1), F32)
            fc = flag[:, c * ch:(c + 1) * ch]
            incl = jnp.dot(fc.astype(BF16), tri, preferred_element_type=F32)
            excl.append(base + incl - fc)
            bases.append(base)
            base = base + incl[:, ch - 1:ch]
        bases.append(base)
        return jnp.concatenate(excl, axis=1), bases

    eq_excl, _ = seg_prefix(jnp.where(eq, 1.0, 0.0))
    sel = jnp.logical_or(gt, jnp.logical_and(eq, eq_excl < need))
    self_ = jnp.where(sel, 1.0, 0.0)
    sel_excl, bases = seg_prefix(self_)
    seg_off = jnp.where(in_ctx, 0.0, float(cap_ctx))
    pos_ref[0] = jnp.where(sel, sel_excl + seg_off, -1.0).astype(jnp.int32)

    olane = lax.broadcasted_iota(jnp.int32, (ne, HEAD_LANES), 1)
    off = jnp.zeros((ne, HEAD_LANES), F32)
    for c in range(nch + 1):
        off = jnp.where(olane == c, bases[c] + (0.0 if c < ctx_ch else float(cap_ctx)), off)
    off_ref[0] = off.astype(jnp.int32)

    cnt = sel_excl + self_ + seg_off
    for e in range(ne):
        cnt_scr[e] = cnt[e:e + 1]
    n_slots = idx_ref.shape[1]
    sb = n_slots // IDX_SLOT_BLOCKS
    ilane = lax.broadcasted_iota(jnp.int32, (sb, HEAD_LANES), 1)
    off_copy = pltpu.make_async_copy(off_ref.at[0], off_smem, dsem.at[0])
    off_copy.start()
    off_copy.wait()
    for blk in range(IDX_SLOT_BLOCKS):
        slot = (lax.broadcasted_iota(jnp.int32, (sb, HEAD_LANES), 0) + blk * sb).astype(F32)
        p_lo, p_hi = blk * sb, blk * sb + sb - 1

        def per_expert(e, out):
            def scan(c, lohi):
                return (lohi[0] + jnp.where(off_smem[e, c + 1] <= p_lo, 1, 0),
                        lohi[1] + jnp.where(off_smem[e, c] <= p_hi, 1, 0))
            c_lo, c_hi = lax.fori_loop(0, nch, scan, (jnp.int32(0), jnp.int32(0)))

            def per_chunk(c, acc):
                cnt_row = cnt_scr[e, :, pl.ds(pl.multiple_of(c * ch, ch), ch)]
                return acc + jnp.where(cnt_row <= slot, 1.0, 0.0)
            acc = lax.fori_loop(c_lo, c_hi, per_chunk, jnp.zeros((sb, ch), F32))
            full = (c_lo * ch).astype(F32)
            return jnp.where(ilane == e, jnp.sum(acc, axis=1, keepdims=True) + full, out)
        out = lax.fori_loop(0, ne, per_expert, jnp.zeros((sb, HEAD_LANES), F32))
        idx_ref[0, blk * sb:(blk + 1) * sb, :] = out.astype(jnp.int32)


def _topk(aff_t, *, m_ctx, cap_ctx, cap_lat):
    b, ne, l = aff_t.shape
    n_slots = cap_ctx + cap_lat
    kern = functools.partial(_topk_kernel, m_ctx=m_ctx, cap_ctx=cap_ctx, cap_lat=cap_lat)
    return pl.pallas_call(
        kern,
        grid=(b,),
        in_specs=[pl.BlockSpec((1, ne, l), lambda bb: (bb, 0, 0))],
        out_specs=[pl.BlockSpec((1, ne, l), lambda bb: (bb, 0, 0)),
                   pl.BlockSpec((1, ne, HEAD_LANES), lambda bb: (bb, 0, 0)),
                   pl.BlockSpec((1, n_slots, HEAD_LANES), lambda bb: (bb, 0, 0))],
        out_shape=[jax.ShapeDtypeStruct((b, ne, l), jnp.int32),
                   jax.ShapeDtypeStruct((b, ne, HEAD_LANES), jnp.int32),
                   jax.ShapeDtypeStruct((b, n_slots, HEAD_LANES), jnp.int32)],
        scratch_shapes=[pltpu.VMEM((ne, 1, l), F32), pltpu.SMEM((ne, HEAD_LANES), jnp.int32),
                        pltpu.SemaphoreType.DMA((1,))],
        compiler_params=_cparams(("arbitrary",)),
        name="expert_choice_topk",
    )(aff_t)


def _ffn_kernel(idx_ref, h_hbm, wg_ref, wu_ref, wd_ref, y_ref, xg_scr, xb_scr, gate_scr, acc_scr, sem, *, n_experts,
                steps):
    e, b, f = pl.program_id(0), pl.program_id(1), pl.program_id(2)
    nb, nf = pl.num_programs(1), pl.num_programs(2)
    n_slots, d = xb_scr.shape

    def row_copy(bb, ee, r):
        tok = idx_ref[(bb * n_experts + ee) * n_slots + r]
        return pltpu.make_async_copy(h_hbm.at[bb, pl.ds(tok, 1)], xg_scr.at[pl.ds(r, 1)], sem.at[0])

    @pl.when(f == 0)
    def _():
        @pl.when(jnp.logical_and(e == 0, b == 0))
        def _():
            def issue(r, carry):
                row_copy(b, e, r).start()
                return carry
            lax.fori_loop(0, n_slots, issue, 0, unroll=8)
        pltpu.make_async_copy(h_hbm.at[b, pl.ds(0, n_slots)], xg_scr, sem.at[0]).wait()
        xb_scr[...] = xg_scr[:, :d].astype(BF16)
        tail = xg_scr[:, d:]
        lane = lax.broadcasted_iota(jnp.int32, tail.shape, 1)
        gate_scr[...] = jnp.sum(jnp.where(lane == e, tail, 0.0), axis=1, keepdims=True)
        acc_scr[...] = jnp.zeros(acc_scr.shape, F32)

    wrap = b + 1 == nb
    nxt_b = jnp.where(wrap, 0, b + 1)
    nxt_e = jnp.where(wrap, jnp.where(e + 1 == n_experts, 0, e + 1), e)
    share = n_slots // steps
    for r in range(share):
        row_copy(nxt_b, nxt_e, f * share + r).start()

    wg, wu, wd = wg_ref[0, 0].astype(BF16), wu_ref[0, 0].astype(BF16), wd_ref[0, 0].astype(BF16)
    half = n_slots // 2
    rows = [slice(0, half), slice(half, n_slots)]
    au = [(jnp.dot(xb_scr[r, :], wg, preferred_element_type=F32), jnp.dot(xb_scr[r, :], wu, preferred_element_type=F32))
          for r in rows]
    down = [jnp.dot((_silu(a) * u).astype(BF16), wd, preferred_element_type=F32) for a, u in au]
    for r, dn in zip(rows, down):
        acc_scr[r, :] += dn

    @pl.when(f == nf - 1)
    def _():
        y_ref[0, 0] = (acc_scr[...] * gate_scr[...]).astype(y_ref.dtype)

        @pl.when(jnp.logical_and(e == n_experts - 1, wrap))
        def _():
            pltpu.make_async_copy(h_hbm.at[b, pl.ds(0, n_slots)], xg_scr, sem.at[0]).wait()


def _expert_ffn(idx_flat, h2, w_gate, w_up, w_down, *, n_slots, layer):
    b, l, dx = h2.shape
    _, ne, d, ff = w_gate.shape
    tf = 256
    grid_spec = pltpu.PrefetchScalarGridSpec(
        num_scalar_prefetch=1,
        grid=(ne, b, ff // tf),
        in_specs=[pl.BlockSpec(memory_space=pl.ANY),
                  pl.BlockSpec((1, 1, d, tf), lambda e, bb, f, idx: (layer, e, 0, f)),
                  pl.BlockSpec((1, 1, d, tf), lambda e, bb, f, idx: (layer, e, 0, f)),
                  pl.BlockSpec((1, 1, tf, d), lambda e, bb, f, idx: (layer, e, f, 0))],
        out_specs=pl.BlockSpec((1, 1, n_slots, d), lambda e, bb, f, idx: (bb, e, 0, 0)),
        scratch_shapes=[pltpu.VMEM((n_slots, dx), F32), pltpu.VMEM((n_slots, d), BF16),
                        pltpu.VMEM((n_slots, 1), F32), pltpu.VMEM((n_slots, d), F32),
                        pltpu.SemaphoreType.DMA((1,))])
    assert ff % tf == 0 and n_slots % (ff // tf) == 0
    return pl.pallas_call(
        functools.partial(_ffn_kernel, n_experts=ne, steps=ff // tf),
        grid_spec=grid_spec,
        out_shape=jax.ShapeDtypeStruct((b, ne, n_slots, d), BF16),
        compiler_params=_cparams(("arbitrary", "arbitrary", "arbitrary")),
        name="expert_ffn",
    )(idx_flat, h2, w_gate, w_up, w_down)


def _combine_kernel(off_ref, x1_ref, pos_ref, mod_ref, lnw_ref, lnb_ref, y_hbm, o_ref, win_scr, sem, *,
                    alpha, n_experts, win):
    b, i = pl.program_id(0), pl.program_id(1)
    nb, nt = pl.num_programs(0), pl.num_programs(1)
    tm = x1_ref.shape[1]
    n_slots = y_hbm.shape[2]
    blk = win_scr.shape[1] // n_experts
    step = b * nt + i
    slot = step % 2

    def window(bb, ii, e):
        base = (bb * n_experts + e) * HEAD_LANES + ii
        start = jnp.minimum((off_ref[base] // WIN_ALIGN) * WIN_ALIGN, n_slots - win)
        return pl.multiple_of(start, WIN_ALIGN), off_ref[base + 1] - start <= WIN_SHORT

    def transfer(bb, ii, sl, go):
        for e in range(n_experts):
            start, short = window(bb, ii, e)
            for size, cond in ((WIN_SHORT, short), (win, jnp.logical_not(short))):
                @pl.when(cond)
                def _():
                    go(pltpu.make_async_copy(y_hbm.at[bb, e, pl.ds(start, size)],
                                             win_scr.at[sl, pl.ds(e * blk, size)], sem.at[sl, e]))

    @pl.when(step == 0)
    def _():
        win_scr[...] = jnp.zeros(win_scr.shape, win_scr.dtype)
        transfer(b, i, slot, lambda cp: cp.start())

    @pl.when(step + 1 < nb * nt)
    def _():
        wrap = i + 1 == nt
        transfer(jnp.where(wrap, b + 1, b), jnp.where(wrap, 0, i + 1), 1 - slot, lambda cp: cp.start())

    pos = pos_ref[0]
    scol = lax.broadcasted_iota(jnp.int32, (tm, blk), 1)
    onehot = jnp.concatenate([jnp.where((pos[:, e:e + 1] - window(b, i, e)[0]) == scol, 1.0, 0.0).astype(BF16)
                              for e in range(n_experts)], axis=1)
    transfer(b, i, slot, lambda cp: cp.wait())
    moe = jnp.dot(onehot, win_scr[slot], preferred_element_type=F32)
    mod = mod_ref[0, 0]
    o_ref[0] = _layer_norm(alpha * x1_ref[0] + mod[5:6] * moe, lnw_ref[...], lnb_ref[...])


def _combine(off_flat, x1, pos_tok, modtab, ln_w, ln_b, y, *, m_ctx, alpha, latent_only):
    b, l, d = x1.shape
    ne = pos_tok.shape[2]
    tm = COMBINE_TILE
    win = tm + WIN_ALIGN
    blk = -(-win // HEAD_LANES) * HEAD_LANES
    ctx_tiles = m_ctx // tm

    def rows(width):
        return pl.BlockSpec((1, tm, width), lambda bb, i, off: (bb, i, 0))

    out_rows = l - m_ctx if latent_only else l
    out_spec = (pl.BlockSpec((1, tm, d), lambda bb, i, off: (bb, jnp.maximum(i - ctx_tiles, 0), 0))
                if latent_only else rows(d))

    grid_spec = pltpu.PrefetchScalarGridSpec(
        num_scalar_prefetch=1,
        grid=(b, l // tm),
        in_specs=[rows(d), rows(ne),
                  pl.BlockSpec((1, 1, 6, d), lambda bb, i, off: (bb, jnp.minimum(i // ctx_tiles, 1), 0, 0)),
                  pl.BlockSpec((1, d), lambda bb, i, off: (0, 0)),
                  pl.BlockSpec((1, d), lambda bb, i, off: (0, 0)),
                  pl.BlockSpec(memory_space=pl.ANY)],
        out_specs=out_spec,
        scratch_shapes=[pltpu.VMEM((2, ne * blk, d), BF16), pltpu.SemaphoreType.DMA((2, ne))])
    return pl.pallas_call(
        functools.partial(_combine_kernel, alpha=alpha, n_experts=ne, win=win),
        grid_spec=grid_spec,
        out_shape=jax.ShapeDtypeStruct((b, out_rows, d), F32),
        compiler_params=_cparams(("arbitrary", "arbitrary")),
        name="moe_combine_ln",
    )(off_flat, x1, pos_tok, modtab, ln_w.reshape(1, d), ln_b.reshape(1, d), y)


def kernel(x, c, ctx, c_ctx, w_mod, b_mod, w_in, w_conv, lambda_qk, subln_w, hgrn_lb_logits, hgrn_norm_w, w_out,
           ln_w, ln_b, w_router, w_gate, w_up, w_down):
    bsz, n_lat, d = x.shape
    m_ctx = ctx.shape[1]
    depth = w_mod.shape[0]
    ne = w_router.shape[2]
    aw, cw, hw = d // 2, d // 4, d // 4
    assert m_ctx % TILE == 0 and n_lat % ATTN_KEY_CHUNK == 0 and n_lat % GRID_W == 0 and bsz + 1 <= 8
    assert (m_ctx + n_lat) % (8 * 16) == 0 and aw == 1024 and ne <= HEAD_LANES
    alpha = (2.0 * depth) ** 0.25
    cap_ctx = EC_CAPACITY_FACTOR * m_ctx // ne
    cap_lat = EC_CAPACITY_FACTOR * n_lat // ne
    n_slots = cap_ctx + cap_lat
    assert n_slots % WIN_ALIGN == 0 and n_slots >= COMBINE_TILE + WIN_ALIGN
    assert (m_ctx + n_lat) // COMBINE_TILE + 1 <= HEAD_LANES

    cs = jnp.zeros((8, d), F32).at[:bsz].set(c).at[bsz].set(c_ctx)
    mod = _modulation(cs, w_mod, b_mod)
    mod_lat = mod[:, :bsz].reshape(depth, bsz, 1, 6, d)
    mod_ctx = jnp.broadcast_to(mod[:, bsz].reshape(depth, 1, 1, 6, d), (depth, bsz, 1, 6, d))
    modtab = jnp.concatenate([mod_ctx, mod_lat], axis=2)

    tables = _rope_tables(m_ctx, n_lat)
    xa = jnp.concatenate([ctx, x], axis=1)
    w_router_pad = jnp.zeros((depth, d, HEAD_LANES), F32).at[:, :, :ne].set(w_router)
    hg_heads = hw // HEAD_LANES
    cb0 = 3 * cw // HEAD_LANES

    for l in range(depth):
        lambda_init = 0.8 - 0.6 * math.exp(-0.3 * l)
        w_in_bf = w_in[l].astype(BF16)
        qkv = _inproj(xa, modtab[l], w_in_bf, tables, col0=0, ncols=3 * aw, out_dtype=BF16, rope_tiles=2,
                      m_ctx=m_ctx)
        cg = _inproj(xa, modtab[l], w_in_bf, tables, col0=3 * aw, ncols=3 * cw + 5 * hw, out_dtype=F32,
                     rope_tiles=0, m_ctx=m_ctx)
        att = _attention(qkv, lambda_qk[l], subln_w[l], m_ctx=m_ctx, lambda_init=lambda_init)
        o_f, o_b = _hgrn(cg, hgrn_lb_logits, layer=l, hg_heads=hg_heads, col_q=cb0, col_i=cb0 + hg_heads,
                         col_ff=cb0 + 3 * hg_heads, col_fb=cb0 + 4 * hg_heads)
        x1, h2, aff = _outproj(xa, att, cg, o_f, o_b, w_conv[l], hgrn_norm_w[l], w_out[l].astype(BF16), modtab[l],
                               ln_w[l, 0], ln_b[l, 0], w_router_pad[l], m_ctx=m_ctx, alpha=alpha, n_experts=ne,
                               cw=cw, col_gg=(3 * cw + 2 * hw) // hw)
        pos, off, idx = _topk(jnp.swapaxes(aff, 1, 2), m_ctx=m_ctx, cap_ctx=cap_ctx, cap_lat=cap_lat)
        idx_flat = jnp.minimum(jnp.swapaxes(idx[:, :, :ne], 1, 2).reshape(-1), m_ctx + n_lat - 1)
        y = _expert_ffn(idx_flat, h2, w_gate, w_up, w_down, n_slots=n_slots, layer=l)
        xa = _combine(off.reshape(-1), x1, jnp.swapaxes(pos, 1, 2), modtab[l], ln_w[l, 1], ln_b[l, 1], y,
                      m_ctx=m_ctx, alpha=alpha, latent_only=l == depth - 1)
    return xa
```

```python
import functools
import math

import jax
import jax.numpy as jnp
from jax import lax
from jax.experimental import pallas as pl
from jax.experimental.pallas import tpu as pltpu

F32 = jnp.float32
BF16 = jnp.bfloat16
HIGHEST = lax.Precision.HIGHEST

GRID_W = 64
ROPE_THETA = 10000.0
ATT_HEAD_DIM = 64
ROPE_PAIR = ATT_HEAD_DIM // 4
F32_INF_BITS = 0x7F800000
F32_ORDER_BITS = 31
HEAD_LANES = 128
HGRN_CHUNK = 64
HGRN_HEADS_PER_STEP = 4
HGRN_SAFE_EXP = 80.0
EC_CAPACITY_FACTOR = 2
EPS = 1e-6
TILE = 256
INPROJ_SUB = 256
ATTN_KEY_CHUNK = 1024
ATTN_UNROLL = 4
COMBINE_TILE = 128
IDX_SLOT_BLOCKS = 4
WIN_SHORT = 48
WIN_ALIGN = 16
V7X_VMEM_BYTES = 64 * 1024 * 1024
VMEM_LIMIT = V7X_VMEM_BYTES - 8 * 1024 * 1024


def _cparams(sem):
    return pltpu.CompilerParams(dimension_semantics=sem, vmem_limit_bytes=VMEM_LIMIT)


def _silu(x):
    return x * jax.nn.sigmoid(x)


def _mod_kernel(cs_ref, w_ref, b_ref, o_ref):
    a = _silu(cs_ref[...])
    o_ref[0] = jnp.dot(a, w_ref[0], precision=HIGHEST, preferred_element_type=F32) + b_ref[0]


def _modulation(cs, w_mod, b_mod):
    depth, d, n6 = w_mod.shape
    tn = 1024
    return pl.pallas_call(
        _mod_kernel,
        grid=(depth, n6 // tn),
        in_specs=[pl.BlockSpec((8, d), lambda l, j: (0, 0)),
                  pl.BlockSpec((1, d, tn), lambda l, j: (l, 0, j)),
                  pl.BlockSpec((1, 1, tn), lambda l, j: (l, 0, j))],
        out_specs=pl.BlockSpec((1, 8, tn), lambda l, j: (l, 0, j)),
        out_shape=jax.ShapeDtypeStruct((depth, 8, n6), F32),
        compiler_params=_cparams(("arbitrary", "arbitrary")),
        name="modulation",
    )(cs, w_mod, b_mod.reshape(depth, 1, n6))


def _inproj_kernel(x_ref, mod_ref, w_ref, rc_ref, ra_ref, rb_ref, o_ref, h_scr, *, m_ctx, rope_tiles):
    i = pl.program_id(1)
    j = pl.program_id(2)
    tm = x_ref.shape[1]

    @pl.when(j == 0)
    def _():
        row = i * tm + lax.broadcasted_iota(jnp.int32, (tm, 1), 0)
        is_ctx = row < m_ctx
        mod = mod_ref[0]
        sh = jnp.where(is_ctx, mod[0, 0:1], mod[1, 0:1])
        sc = jnp.where(is_ctx, mod[0, 1:2], mod[1, 1:2])
        h_scr[...] = (x_ref[0] * (1.0 + sc) + sh).astype(BF16)

    h = h_scr[...]
    if rope_tiles:
        rc, ra, rb = rc_ref[0], ra_ref[0], rb_ref[0]
        qscale = jnp.where(j == 0, ATT_HEAD_DIM ** -0.5 * math.log2(math.e), 1.0).astype(F32)

    def finish(c, acc):
        for k in range(INPROJ_SUB // HEAD_LANES):
            blk = acc[:, k * HEAD_LANES:(k + 1) * HEAD_LANES]
            if rope_tiles:
                blk = (blk * rc + pltpu.roll(blk, ROPE_PAIR, 1) * ra
                       + pltpu.roll(blk, HEAD_LANES - ROPE_PAIR, 1) * rb) * qscale
            lo = c * INPROJ_SUB + k * HEAD_LANES
            o_ref[0, :, lo:lo + HEAD_LANES] = blk.astype(o_ref.dtype)

    pending = None
    for c in range(w_ref.shape[1] // INPROJ_SUB):
        acc = jnp.dot(h, w_ref[:, c * INPROJ_SUB:(c + 1) * INPROJ_SUB], preferred_element_type=F32)
        if pending is not None:
            finish(*pending)
        pending = (c, acc)
    finish(*pending)


def _inproj(xa, modtab, w_bf, tables, *, col0, ncols, out_dtype, rope_tiles, m_ctx):
    b, l, d = xa.shape
    tn = 1024
    tm = l // 8
    joff = col0 // tn
    kern = functools.partial(_inproj_kernel, m_ctx=m_ctx, rope_tiles=rope_tiles)
    tab_spec = pl.BlockSpec((1, tm, HEAD_LANES), lambda bb, i, j: (jnp.where(j < rope_tiles, 0, 1), i, 0))
    return pl.pallas_call(
        kern,
        grid=(b, l // tm, ncols // tn),
        in_specs=[pl.BlockSpec((1, tm, d), lambda bb, i, j: (bb, i, 0)),
                  pl.BlockSpec((1, 2, 6, d), lambda bb, i, j: (bb, 0, 0, 0)),
                  pl.BlockSpec((d, tn), lambda bb, i, j: (0, j + joff)),
                  tab_spec, tab_spec, tab_spec],
        out_specs=pl.BlockSpec((1, tm, tn), lambda bb, i, j: (bb, i, j)),
        out_shape=jax.ShapeDtypeStruct((b, l, ncols), out_dtype),
        scratch_shapes=[pltpu.VMEM((tm, d), BF16)],
        compiler_params=_cparams(("arbitrary", "arbitrary", "arbitrary")),
        name="inproj",
    )(xa, modtab, w_bf, *tables)


def _rope_tables(m_ctx, n_lat):
    nf = ATT_HEAD_DIM // 4
    inv = ROPE_THETA ** (-jnp.arange(nf, dtype=F32) / nf)
    rows = n_lat // GRID_W
    row = jnp.repeat(jnp.arange(rows, dtype=F32), GRID_W)
    col = jnp.tile(jnp.arange(GRID_W, dtype=F32), rows)
    ar, ac = row[:, None] * inv, col[:, None] * inv
    cr, sr, cc, sc = jnp.cos(ar), jnp.sin(ar), jnp.cos(ac), jnp.sin(ac)
    z = jnp.zeros_like(sr)
    c64 = jnp.concatenate([cr, cr, cc, cc], axis=1)
    a64 = jnp.concatenate([z, sr, z, sc], axis=1)
    b64 = jnp.concatenate([-sr, z, -sc, z], axis=1)
    reps = HEAD_LANES // ATT_HEAD_DIM

    def full(t64, fill):
        lat = jnp.tile(t64, (1, reps))
        rot = jnp.concatenate([jnp.full((m_ctx, HEAD_LANES), fill, F32), lat], axis=0)
        return jnp.stack([rot, jnp.full(rot.shape, fill, F32)])

    return full(c64, 1.0), full(a64, 0.0), full(b64, 0.0)


def _attn_kernel(lq_ref, sw_ref, q_ref, k_ref, v_ref, o_ref, vt_scr, m0_scr, l0_scr, acc0_scr, m1_scr, l1_scr,
                 acc1_scr, sa0_scr, sa1_scr, sb0_scr, sb1_scr, *, m_ctx, tk, lambda_init):
    i = pl.program_id(2)
    tq = q_ref.shape[1]
    n_lat = k_ref.shape[1] - m_ctx

    @pl.when(i == 0)
    def _():
        vt_scr[...] = v_ref[0].T

    q = q_ref[0]
    lane = lax.broadcasted_iota(jnp.int32, q.shape, 1)
    zero = jnp.zeros_like(q)
    qs = (jnp.where(lane < ATT_HEAD_DIM, q, zero), jnp.where(lane >= ATT_HEAD_DIM, q, zero))
    stats = ((m0_scr, l0_scr, acc0_scr), (m1_scr, l1_scr, acc1_scr))
    for m_scr, l_scr, acc_scr in stats:
        m_scr[...] = jnp.full(m_scr.shape, -jnp.inf, F32)
        l_scr[...] = jnp.zeros(l_scr.shape, F32)
        acc_scr[...] = jnp.zeros(acc_scr.shape, F32)

    def scores(start, size, st_refs):
        kc = k_ref[0, pl.ds(start, size), :]
        cmax = []
        for s in range(2):
            st = lax.dot_general(kc, qs[s], (((1,), (1,)), ((), ())), preferred_element_type=F32)
            st_refs[s][0:size, :] = st
            cmax.append(jnp.max(st, axis=0, keepdims=True))
        return tuple(cmax)

    def accumulate(start, size, st_refs, cmax):
        vt = vt_scr[:, pl.ds(start, size)]
        for s, (m_scr, l_scr, acc_scr) in enumerate(stats):
            m_old = m_scr[...]
            m_new = jnp.maximum(m_old, cmax[s])
            alpha = jnp.exp2(m_old - m_new)
            p = jnp.exp2(st_refs[s][0:size, :] - m_new)
            l_scr[...] = alpha * l_scr[...] + jnp.sum(p, axis=0, keepdims=True)
            acc_scr[...] = alpha * acc_scr[...] + jnp.dot(vt, p.astype(BF16), preferred_element_type=F32)
            m_scr[...] = m_new

    st_a, st_b = (sa0_scr, sa1_scr), (sb0_scr, sb1_scr)

    @pl.when(i * tq < m_ctx)
    def _():
        accumulate(0, m_ctx, st_a, scores(0, m_ctx, st_a))

    @pl.when(i * tq >= m_ctx)
    def _():
        def at(c):
            return pl.multiple_of(m_ctx + c * tk, math.gcd(m_ctx, tk))
        nck = n_lat // tk
        bufs = (st_b, st_a)

        def run(first, count, cm):
            for u in range(count):
                c = first + u
                last = isinstance(c, int) and c + 1 >= nck
                cm_next = None if last else scores(at(c + 1), tk, bufs[(u + 1) % 2])
                accumulate(at(c), tk, bufs[u % 2], cm)
                cm = cm_next
            return cm

        cm_ctx = scores(0, m_ctx, st_a)
        cm = scores(at(0), tk, st_b)
        accumulate(0, m_ctx, st_a, cm_ctx)
        trips = (nck - 1) // ATTN_UNROLL
        cm = lax.fori_loop(0, trips, lambda j, cm: run(j * ATTN_UNROLL, ATTN_UNROLL, cm), cm)
        run(trips * ATTN_UNROLL, nck - trips * ATTN_UNROLL, cm)

    lq = lq_ref[...]
    lam = (jnp.exp(jnp.sum(lq[0:1] * lq[1:2], axis=1, keepdims=True))
           - jnp.exp(jnp.sum(lq[2:3] * lq[3:4], axis=1, keepdims=True)) + lambda_init)
    ot = acc0_scr[...] / l0_scr[...] - lam * (acc1_scr[...] / l1_scr[...])
    ms = jnp.mean(ot * ot, axis=0, keepdims=True)
    o = (ot * lax.rsqrt(ms + EPS)).T * sw_ref[...] * (1.0 - lambda_init)
    o_ref[0] = o.astype(o_ref.dtype)


def _attention(qkv, lambda_qk, subln_w, *, m_ctx, lambda_init):
    b, l, w3 = qkv.shape
    heads = w3 // (3 * HEAD_LANES)
    tq = TILE
    tk = ATTN_KEY_CHUNK
    kern = functools.partial(_attn_kernel, m_ctx=m_ctx, tk=tk, lambda_init=lambda_init)
    return pl.pallas_call(
        kern,
        grid=(b, heads, l // tq),
        in_specs=[pl.BlockSpec((4, ATT_HEAD_DIM), lambda bb, h, i: (0, 0)),
                  pl.BlockSpec((1, HEAD_LANES), lambda bb, h, i: (0, 0)),
                  pl.BlockSpec((1, tq, HEAD_LANES), lambda bb, h, i: (bb, i, h)),
                  pl.BlockSpec((1, l, HEAD_LANES), lambda bb, h, i: (bb, 0, heads + h)),
                  pl.BlockSpec((1, l, HEAD_LANES), lambda bb, h, i: (bb, 0, 2 * heads + h))],
        out_specs=pl.BlockSpec((1, tq, HEAD_LANES), lambda bb, h, i: (bb, i, h)),
        out_shape=jax.ShapeDtypeStruct((b, l, heads * HEAD_LANES), BF16),
        scratch_shapes=[pltpu.VMEM((HEAD_LANES, l), BF16)]
        + [pltpu.VMEM((1, tq), F32), pltpu.VMEM((1, tq), F32), pltpu.VMEM((HEAD_LANES, tq), F32)] * 2
        + [pltpu.VMEM((max(tk, m_ctx), tq), F32)] * 4,
        compiler_params=_cparams(("arbitrary", "arbitrary", "arbitrary")),
        name="diff_attention",
    )(lambda_qk, subln_w.reshape(1, HEAD_LANES), qkv, qkv, qkv)


def _log_sigmoid(z):
    return jnp.minimum(z, 0.0) - jnp.log1p(jnp.exp(-jnp.abs(z)))


def _forget_gate(z, lb):
    ls = _log_sigmoid(z)
    key = jax.nn.sigmoid(-z)
    if lb is None:
        return ls, key
    a = jnp.log(lb)
    b = jnp.log1p(-lb) + ls
    logf = jnp.maximum(a, b) + jnp.log1p(jnp.exp(-jnp.abs(a - b)))
    return logf, (1.0 - lb) * key


def _hgrn_tile(dirs):
    ch = HGRN_CHUNK
    work = []
    for q_ref, v_ref, z_ref, o_ref, s_scr, lb, reverse in dirs:
        t = q_ref.shape[1]
        logf, key = _forget_gate(z_ref[0], lb)
        r = lax.broadcasted_iota(jnp.int32, (t, t), 0)
        c = lax.broadcasted_iota(jnp.int32, (t, t), 1)
        tri = ((r // ch) == (c // ch)) & ((c >= r) if reverse else (c <= r))
        a = jnp.dot(tri.astype(F32), logf, precision=HIGHEST, preferred_element_type=F32)
        nch = t // ch

        def per_chunk(row_of):
            return jnp.concatenate([jnp.broadcast_to(row_of(a[k * ch:(k + 1) * ch]), (ch, a.shape[1]))
                                    for k in range(nch)], axis=0)
        a_mid = per_chunk(lambda ac: ac[ch // 2 - 1:ch // 2])
        a_end = per_chunk((lambda ac: ac[0:1]) if reverse else (lambda ac: ac[ch - 1:ch]))
        work.append(dict(q=q_ref[0], v32=v_ref[0], v=v_ref[0].astype(BF16), key=key, logf=logf, a=a, a_mid=a_mid,
                         a_end=a_end, tri=tri, o_ref=o_ref, s_scr=s_scr, reverse=reverse, nch=nch))

    ends = [jnp.abs(w["a"][r:r + 1] - w["a"][k * ch + ch // 2 - 1:k * ch + ch // 2])
            for w in work for k in range(w["nch"]) for r in (k * ch, (k + 1) * ch - 1)]
    safe = jnp.max(functools.reduce(jnp.maximum, ends)) <= HGRN_SAFE_EXP

    start_states = [w["s_scr"][...] for w in work]
    _hgrn_tile_chunked(work)

    @pl.when(jnp.logical_not(safe))
    def _():
        for w, st0 in zip(work, start_states):
            _hgrn_tile_stepwise(w, st0)


def _hgrn_tile_stepwise(w, st0):
    t, hd = w["q"].shape
    q_t, k_t, f_t, v_t = w["q"].T, w["key"].T, jnp.exp(w["logf"]).T, w["v32"].T
    lane = lax.broadcasted_iota(jnp.int32, (hd, t), 1)
    eye = lax.broadcasted_iota(jnp.int32, (hd, hd), 0) == lax.broadcasted_iota(jnp.int32, (hd, hd), 1)

    def column(x_t, r):
        return jnp.sum(jnp.where(lane == r, x_t, 0.0), axis=1, keepdims=True)

    def as_row(col):
        return jnp.sum(jnp.where(eye, col, 0.0), axis=0, keepdims=True)

    def body(n, carry):
        st, o_t = carry
        r = t - 1 - n if w["reverse"] else n
        st = as_row(column(f_t, r)) * st + column(v_t, r) * as_row(column(k_t, r))
        o_col = jnp.sum(st * as_row(column(q_t, r)), axis=1, keepdims=True)
        return st, jnp.where(lane == r, o_col, o_t)

    st, o_t = lax.fori_loop(0, t, body, (st0, jnp.zeros((hd, t), F32)))
    w["s_scr"][...] = st
    w["o_ref"][0] = o_t.T


def _hgrn_tile_chunked(work):
    ch = HGRN_CHUNK
    nt_dims = (((1,), (1,)), ((), ()))
    for w in work:
        a, nch, a_mid, a_end = w["a"], w["nch"], w["a_mid"], w["a_end"]
        qe = (w["q"] * jnp.exp(a - a_mid)).astype(BF16)
        ke = (w["key"] * jnp.exp(a_mid - a)).astype(BF16)
        kd = (w["key"] * jnp.exp(a_end - a)).astype(BF16)
        w["qa"] = (w["q"] * jnp.exp(a)).astype(BF16)
        w["decay"] = jnp.exp(a_end)
        w["sc"] = lax.dot_general(qe, ke, nt_dims, preferred_element_type=F32)
        w["kv"] = [lax.dot_general(w["v"][k * ch:(k + 1) * ch], kd[k * ch:(k + 1) * ch], (((0,), (0,)), ((), ())),
                                   preferred_element_type=F32) for k in range(nch)]

    for w in work:
        w["intra"] = jnp.dot(jnp.where(w["tri"], w["sc"], 0.0).astype(BF16), w["v"], preferred_element_type=F32)

    for w in work:
        nch = w["nch"]
        st = w["s_scr"][...]
        states = [None] * nch
        for k in (range(nch - 1, -1, -1) if w["reverse"] else range(nch)):
            states[k] = st.astype(BF16)
            row = k * ch if w["reverse"] else (k + 1) * ch - 1
            st = w["decay"][row:row + 1] * st + w["kv"][k]
        w["s_scr"][...] = st
        w["states"] = states

    for w in work:
        inter = [lax.dot_general(w["qa"][k * ch:(k + 1) * ch], w["states"][k], nt_dims, preferred_element_type=F32)
                 for k in range(w["nch"])]
        w["o_ref"][0] = jnp.concatenate(inter, axis=0) + w["intra"]


def _hgrn_kernel(lbl_ref, qf_ref, vf_ref, zf_ref, qb_ref, vb_ref, zb_ref, of_ref, ob_ref, sf_scr, sb_scr, *, layer):
    @pl.when(pl.program_id(2) == 0)
    def _():
        sf_scr[...] = jnp.zeros(sf_scr.shape, F32)
        sb_scr[...] = jnp.zeros(sb_scr.shape, F32)

    if layer == 0:
        lb = None
    else:
        lg = lbl_ref[...]
        ex = jnp.exp(lg - jnp.max(lg, axis=1, keepdims=True))
        sm = ex / jnp.sum(ex, axis=1, keepdims=True)
        lb = sm[:, 1]
        for k in range(2, layer + 1):
            lb = lb + sm[:, k]

    dirs = []
    for h in range(qf_ref.shape[2] // HEAD_LANES):
        lanes = slice(h * HEAD_LANES, (h + 1) * HEAD_LANES)

        def head(ref):
            return ref.at[:, :, lanes]
        dirs.append((head(qf_ref), head(vf_ref), head(zf_ref), head(of_ref), sf_scr.at[h],
                     None if lb is None else lb[0:1, lanes], False))
        dirs.append((head(qb_ref), head(vb_ref), head(zb_ref), head(ob_ref), sb_scr.at[h],
                     None if lb is None else lb[1:2, lanes], True))
    _hgrn_tile(dirs)


def _hgrn(cg, lb_logits, *, layer, hg_heads, col_q, col_i, col_ff, col_fb):
    b, l, _ = cg.shape
    t = TILE
    nb = l // t
    depth = lb_logits.shape[1]
    hps = HGRN_HEADS_PER_STEP
    wide = hps * HEAD_LANES
    assert hg_heads % hps == 0 and all(c % hps == 0 for c in (col_q, col_i, col_ff, col_fb))

    def fwd(col):
        return pl.BlockSpec((1, t, wide), lambda bb, h, i: (bb, i, col // hps + h))

    def bwd(col):
        return pl.BlockSpec((1, t, wide), lambda bb, h, i: (bb, jnp.where(i == 0, 0, nb - i), col // hps + h))

    out_f = pl.BlockSpec((1, t, wide), lambda bb, h, i: (bb, i, h))
    out_b = pl.BlockSpec((1, t, wide), lambda bb, h, i: (bb, jnp.where(i == 0, 0, nb - i), h))
    shp = jax.ShapeDtypeStruct((b, l, hg_heads * HEAD_LANES), F32)
    state = pltpu.VMEM((hps, HEAD_LANES, HEAD_LANES), F32)
    return pl.pallas_call(
        functools.partial(_hgrn_kernel, layer=layer),
        grid=(b, hg_heads // hps, nb),
        in_specs=[pl.BlockSpec((2, depth, wide), lambda bb, h, i: (0, 0, h)),
                  fwd(col_q), fwd(col_i), fwd(col_ff), bwd(col_q), bwd(col_i), bwd(col_fb)],
        out_specs=[out_f, out_b],
        out_shape=[shp, shp],
        scratch_shapes=[state, state],
        compiler_params=_cparams(("arbitrary", "arbitrary", "arbitrary")),
        name="hgrn2",
    )(lb_logits, cg, cg, cg, cg, cg, cg)


def _layer_norm(r, w, b):
    mu = jnp.mean(r, axis=1, keepdims=True)
    var = jnp.mean(jnp.square(r - mu), axis=1, keepdims=True)
    return (r - mu) * lax.rsqrt(var + EPS) * w + b


def _outproj_kernel(x_ref, att_ref, cx_ref, cb_ref, cc_ref, cxp_ref, ccp_ref, cxn_ref, ccn_ref, of_ref, ob_ref,
                    gg_ref, wconv_ref, hnw_ref, wout_ref, mod_ref, lnw_ref, lnb_ref, wr_ref,
                    x1_ref, h2_ref, aff_ref, *, ctx_tiles, alpha, n_experts):
    i = pl.program_id(1)
    nt = pl.num_programs(1)
    tm = x_ref.shape[1]
    aw = att_ref.shape[2]
    cw = cx_ref.shape[2]

    u = cc_ref[0] * cx_ref[0]
    prev_ok = jnp.logical_and(i != 0, i != ctx_tiles)
    next_ok = jnp.logical_and(i != ctx_tiles - 1, i != nt - 1)
    u_before = jnp.where(prev_ok, (ccp_ref[0] * cxp_ref[0])[7:8], 0.0)
    u_after = jnp.where(next_ok, (ccn_ref[0] * cxn_ref[0])[0:1], 0.0)
    row = lax.broadcasted_iota(jnp.int32, (tm, 1), 0)
    u_prev = jnp.where(row == 0, u_before, pltpu.roll(u, 1, 0))
    u_next = jnp.where(row == tm - 1, u_after, pltpu.roll(u, tm - 1, 0))
    wc = wconv_ref[...]
    conv = cb_ref[0] * (u_prev * wc[0:1] + u * wc[1:2] + u_next * wc[2:3])

    o = of_ref[0] + ob_ref[0]
    gg = gg_ref[0]
    recs = []
    for h in range(o.shape[1] // HEAD_LANES):
        oh = o[:, h * HEAD_LANES:(h + 1) * HEAD_LANES]
        ms = jnp.mean(oh * oh, axis=1, keepdims=True)
        recs.append(oh * lax.rsqrt(ms + EPS) * hnw_ref[...] * _silu(gg[:, h * HEAD_LANES:(h + 1) * HEAD_LANES]))
    rec = jnp.concatenate(recs, axis=1)

    y = jnp.dot(att_ref[0], wout_ref[0:aw, :], preferred_element_type=F32)
    y = y + jnp.dot(conv.astype(BF16), wout_ref[aw:aw + cw, :], preferred_element_type=F32)
    y = y + jnp.dot(rec.astype(BF16), wout_ref[aw + cw:, :], preferred_element_type=F32)

    mod = mod_ref[0, 0]
    x1 = _layer_norm(alpha * x_ref[0] + mod[2:3] * y, lnw_ref[...], lnb_ref[...])
    x1_ref[0] = x1
    h2 = x1 * (1.0 + mod[4:5]) + mod[3:4]
    d = h2.shape[1]
    h2_ref[0, :, :d] = h2
    wr = wr_ref[...]
    wr_hi = wr.astype(BF16)
    wr_lo = (wr - wr_hi.astype(F32)).astype(BF16)
    h2_hi = h2.astype(BF16)
    h2_lo = (h2 - h2_hi.astype(F32)).astype(BF16)
    logits = (jnp.dot(h2_hi, wr_hi, preferred_element_type=F32) + jnp.dot(h2_hi, wr_lo, preferred_element_type=F32)
              + jnp.dot(h2_lo, wr_hi, preferred_element_type=F32))
    lane = lax.broadcasted_iota(jnp.int32, logits.shape, 1)
    logits = jnp.where(lane < n_experts, logits, -jnp.inf)
    ex = jnp.exp(logits - jnp.max(logits, axis=1, keepdims=True))
    aff = ex / jnp.sum(ex, axis=1, keepdims=True)
    aff_ref[0] = aff[:, :n_experts]
    h2_ref[0, :, d:] = aff


def _outproj(xa, att, cg, o_f, o_b, w_conv, hgrn_norm_w, w_out_bf, modtab, ln_w, ln_b, w_router_pad, *,
             m_ctx, alpha, n_experts, cw, col_gg):
    b, l, d = xa.shape
    tm = TILE
    aw = att.shape[2]
    hw = o_f.shape[2]
    r8 = tm // 8
    last8 = l // 8 - 1
    ctx_tiles = m_ctx // tm

    def rows(width, col):
        return pl.BlockSpec((1, tm, width), lambda bb, i: (bb, i, col))

    def halo_prev(col):
        return pl.BlockSpec((1, 8, cw), lambda bb, i: (bb, jnp.maximum(i * r8 - 1, 0), col))

    def halo_next(col):
        return pl.BlockSpec((1, 8, cw), lambda bb, i: (bb, jnp.minimum((i + 1) * r8, last8), col))

    def const(shape):
        return pl.BlockSpec(shape, lambda bb, i: (0,) * len(shape))

    kern = functools.partial(_outproj_kernel, ctx_tiles=ctx_tiles, alpha=alpha, n_experts=n_experts)
    return pl.pallas_call(
        kern,
        grid=(b, l // tm),
        in_specs=[rows(d, 0), rows(aw, 0),
                  rows(cw, 0), rows(cw, 1), rows(cw, 2),
                  halo_prev(0), halo_prev(2), halo_next(0), halo_next(2),
                  rows(hw, 0), rows(hw, 0), rows(hw, col_gg),
                  const((3, cw)), const((1, HEAD_LANES)), const((d, d)),
                  pl.BlockSpec((1, 1, 6, d), lambda bb, i: (bb, jnp.minimum(i // ctx_tiles, 1), 0, 0)),
                  const((1, d)), const((1, d)), const((d, HEAD_LANES))],
        out_specs=[rows(d, 0), rows(d + HEAD_LANES, 0), rows(n_experts, 0)],
        out_shape=[jax.ShapeDtypeStruct((b, l, d), F32), jax.ShapeDtypeStruct((b, l, d + HEAD_LANES), F32),
                   jax.ShapeDtypeStruct((b, l, n_experts), F32)],
        compiler_params=_cparams(("arbitrary", "arbitrary")),
        name="outproj_ln_router",
    )(xa, att, cg, cg, cg, cg, cg, cg, cg, o_f, o_b, cg, w_conv, hgrn_norm_w.reshape(1, HEAD_LANES), w_out_bf,
      modtab, ln_w.reshape(1, d), ln_b.reshape(1, d), w_router_pad)


def _topk_kernel(aff_ref, pos_ref, off_ref, idx_ref, cnt_scr, off_smem, dsem, *, m_ctx, cap_ctx, cap_lat):
    aff = aff_ref[0]
    ne, l = aff.shape
    ch = COMBINE_TILE
    nch = l // ch
    ctx_ch = m_ctx // ch
    bits = pltpu.bitcast(aff, jnp.int32)
    lane = lax.broadcasted_iota(jnp.int32, (ne, l), 1)
    in_ctx = lane < m_ctx

    def kth_largest(seg, k):
        def body(_, lohi):
            lo, hi = lohi
            mid = lo + lax.shift_right_logical(hi - lo, 1)
            cnt = jnp.sum(jnp.where(jnp.logical_and(seg, bits >= mid), 1.0, 0.0), axis=1, keepdims=True)
            ge = cnt >= k
            return jnp.where(ge, mid, lo), jnp.where(ge, hi, mid)
        lo0 = jnp.zeros((ne, 1), jnp.int32)
        hi0 = jnp.full((ne, 1), F32_INF_BITS, jnp.int32)
        return lax.fori_loop(0, F32_ORDER_BITS, body, (lo0, hi0))[0]

    thr = jnp.where(in_ctx, kth_largest(in_ctx, cap_ctx), kth_largest(jnp.logical_not(in_ctx), cap_lat))
    gt = bits > thr
    eq = bits == thr
    gtf = jnp.where(gt, 1.0, 0.0)
    n_gt_ctx = jnp.sum(jnp.where(in_ctx, gtf, 0.0), axis=1, keepdims=True)
    n_gt_lat = jnp.sum(jnp.where(in_ctx, 0.0, gtf), axis=1, keepdims=True)
    need = jnp.where(in_ctx, cap_ctx - n_gt_ctx, cap_lat - n_gt_lat)

    tr = lax.broadcasted_iota(jnp.int32, (ch, ch), 0)
    tc = lax.broadcasted_iota(jnp.int32, (ch, ch), 1)
    tri = jnp.where(tr <= tc, 1.0, 0.0).astype(BF16)

    def seg_prefix(flag):
        excl, bases = [], []
        base = jnp.zeros((ne, 1), F32)
        for c in range(nch):
            if c == ctx_ch:
                base = jnp.zeros((ne, 1), F32)
            fc = flag[:, c * ch:(c + 1) * ch]
            incl = jnp.dot(fc.astype(BF16), tri, preferred_element_type=F32)
            excl.append(base + incl - fc)
            bases.append(base)
            base = base + incl[:, ch - 1:ch]
        bases.append(base)
        return jnp.concatenate(excl, axis=1), bases

    eq_excl, _ = seg_prefix(jnp.where(eq, 1.0, 0.0))
    sel = jnp.logical_or(gt, jnp.logical_and(eq, eq_excl < need))
    self_ = jnp.where(sel, 1.0, 0.0)
    sel_excl, bases = seg_prefix(self_)
    seg_off = jnp.where(in_ctx, 0.0, float(cap_ctx))
    pos_ref[0] = jnp.where(sel, sel_excl + seg_off, -1.0).astype(jnp.int32)

    olane = lax.broadcasted_iota(jnp.int32, (ne, HEAD_LANES), 1)
    off = jnp.zeros((ne, HEAD_LANES), F32)
    for c in range(nch + 1):
        off = jnp.where(olane == c, bases[c] + (0.0 if c < ctx_ch else float(cap_ctx)), off)
    off_ref[0] = off.astype(jnp.int32)

    cnt = sel_excl + self_ + seg_off
    for e in range(ne):
        cnt_scr[e] = cnt[e:e + 1]
    n_slots = idx_ref.shape[1]
    sb = n_slots // IDX_SLOT_BLOCKS
    ilane = lax.broadcasted_iota(jnp.int32, (sb, HEAD_LANES), 1)
    off_copy = pltpu.make_async_copy(off_ref.at[0], off_smem, dsem.at[0])
    off_copy.start()
    off_copy.wait()
    for blk in range(IDX_SLOT_BLOCKS):
        slot = (lax.broadcasted_iota(jnp.int32, (sb, HEAD_LANES), 0) + blk * sb).astype(F32)
        p_lo, p_hi = blk * sb, blk * sb + sb - 1

        def per_expert(e, out):
            def scan(c, lohi):
                return (lohi[0] + jnp.where(off_smem[e, c + 1] <= p_lo, 1, 0),
                        lohi[1] + jnp.where(off_smem[e, c] <= p_hi, 1, 0))
            c_lo, c_hi = lax.fori_loop(0, nch, scan, (jnp.int32(0), jnp.int32(0)))

            def per_chunk(c, acc):
                cnt_row = cnt_scr[e, :, pl.ds(pl.multiple_of(c * ch, ch), ch)]
                return acc + jnp.where(cnt_row <= slot, 1.0, 0.0)
            acc = lax.fori_loop(c_lo, c_hi, per_chunk, jnp.zeros((sb, ch), F32))
            full = (c_lo * ch).astype(F32)
            return jnp.where(ilane == e, jnp.sum(acc, axis=1, keepdims=True) + full, out)
        out = lax.fori_loop(0, ne, per_expert, jnp.zeros((sb, HEAD_LANES), F32))
        idx_ref[0, blk * sb:(blk + 1) * sb, :] = out.astype(jnp.int32)


def _topk(aff_t, *, m_ctx, cap_ctx, cap_lat):
    b, ne, l = aff_t.shape
    n_slots = cap_ctx + cap_lat
    kern = functools.partial(_topk_kernel, m_ctx=m_ctx, cap_ctx=cap_ctx, cap_lat=cap_lat)
    return pl.pallas_call(
        kern,
        grid=(b,),
        in_specs=[pl.BlockSpec((1, ne, l), lambda bb: (bb, 0, 0))],
        out_specs=[pl.BlockSpec((1, ne, l), lambda bb: (bb, 0, 0)),
                   pl.BlockSpec((1, ne, HEAD_LANES), lambda bb: (bb, 0, 0)),
                   pl.BlockSpec((1, n_slots, HEAD_LANES), lambda bb: (bb, 0, 0))],
        out_shape=[jax.ShapeDtypeStruct((b, ne, l), jnp.int32),
                   jax.ShapeDtypeStruct((b, ne, HEAD_LANES), jnp.int32),
                   jax.ShapeDtypeStruct((b, n_slots, HEAD_LANES), jnp.int32)],
        scratch_shapes=[pltpu.VMEM((ne, 1, l), F32), pltpu.SMEM((ne, HEAD_LANES), jnp.int32),
                        pltpu.SemaphoreType.DMA((1,))],
        compiler_params=_cparams(("arbitrary",)),
        name="expert_choice_topk",
    )(aff_t)


def _ffn_kernel(idx_ref, h_hbm, wg_ref, wu_ref, wd_ref, y_ref, xg_scr, xb_scr, gate_scr, acc_scr, sem, *, n_experts,
                steps):
    e, b, f = pl.program_id(0), pl.program_id(1), pl.program_id(2)
    nb, nf = pl.num_programs(1), pl.num_programs(2)
    n_slots, d = xb_scr.shape

    def row_copy(bb, ee, r):
        tok = idx_ref[(bb * n_experts + ee) * n_slots + r]
        return pltpu.make_async_copy(h_hbm.at[bb, pl.ds(tok, 1)], xg_scr.at[pl.ds(r, 1)], sem.at[0])

    @pl.when(f == 0)
    def _():
        @pl.when(jnp.logical_and(e == 0, b == 0))
        def _():
            def issue(r, carry):
                row_copy(b, e, r).start()
                return carry
            lax.fori_loop(0, n_slots, issue, 0, unroll=8)
        pltpu.make_async_copy(h_hbm.at[b, pl.ds(0, n_slots)], xg_scr, sem.at[0]).wait()
        xb_scr[...] = xg_scr[:, :d].astype(BF16)
        tail = xg_scr[:, d:]
        lane = lax.broadcasted_iota(jnp.int32, tail.shape, 1)
        gate_scr[...] = jnp.sum(jnp.where(lane == e, tail, 0.0), axis=1, keepdims=True)
        acc_scr[...] = jnp.zeros(acc_scr.shape, F32)

    wrap = b + 1 == nb
    nxt_b = jnp.where(wrap, 0, b + 1)
    nxt_e = jnp.where(wrap, jnp.where(e + 1 == n_experts, 0, e + 1), e)
    share = n_slots // steps
    for r in range(share):
        row_copy(nxt_b, nxt_e, f * share + r).start()

    wg, wu, wd = wg_ref[0, 0].astype(BF16), wu_ref[0, 0].astype(BF16), wd_ref[0, 0].astype(BF16)
    half = n_slots // 2
    rows = [slice(0, half), slice(half, n_slots)]
    au = [(jnp.dot(xb_scr[r, :], wg, preferred_element_type=F32), jnp.dot(xb_scr[r, :], wu, preferred_element_type=F32))
          for r in rows]
    down = [jnp.dot((_silu(a) * u).astype(BF16), wd, preferred_element_type=F32) for a, u in au]
    for r, dn in zip(rows, down):
        acc_scr[r, :] += dn

    @pl.when(f == nf - 1)
    def _():
        y_ref[0, 0] = (acc_scr[...] * gate_scr[...]).astype(y_ref.dtype)

        @pl.when(jnp.logical_and(e == n_experts - 1, wrap))
        def _():
            pltpu.make_async_copy(h_hbm.at[b, pl.ds(0, n_slots)], xg_scr, sem.at[0]).wait()


def _expert_ffn(idx_flat, h2, w_gate, w_up, w_down, *, n_slots, layer):
    b, l, dx = h2.shape
    _, ne, d, ff = w_gate.shape
    tf = 256
    grid_spec = pltpu.PrefetchScalarGridSpec(
        num_scalar_prefetch=1,
        grid=(ne, b, ff // tf),
        in_specs=[pl.BlockSpec(memory_space=pl.ANY),
                  pl.BlockSpec((1, 1, d, tf), lambda e, bb, f, idx: (layer, e, 0, f)),
                  pl.BlockSpec((1, 1, d, tf), lambda e, bb, f, idx: (layer, e, 0, f)),
                  pl.BlockSpec((1, 1, tf, d), lambda e, bb, f, idx: (layer, e, f, 0))],
        out_specs=pl.BlockSpec((1, 1, n_slots, d), lambda e, bb, f, idx: (bb, e, 0, 0)),
        scratch_shapes=[pltpu.VMEM((n_slots, dx), F32), pltpu.VMEM((n_slots, d), BF16),
                        pltpu.VMEM((n_slots, 1), F32), pltpu.VMEM((n_slots, d), F32),
                        pltpu.SemaphoreType.DMA((1,))])
    assert ff % tf == 0 and n_slots % (ff // tf) == 0
    return pl.pallas_call(
        functools.partial(_ffn_kernel, n_experts=ne, steps=ff // tf),
        grid_spec=grid_spec,
        out_shape=jax.ShapeDtypeStruct((b, ne, n_slots, d), BF16),
        compiler_params=_cparams(("arbitrary", "arbitrary", "arbitrary")),
        name="expert_ffn",
    )(idx_flat, h2, w_gate, w_up, w_down)


def _combine_kernel(off_ref, x1_ref, pos_ref, mod_ref, lnw_ref, lnb_ref, y_hbm, o_ref, win_scr, sem, *,
                    alpha, n_experts, win):
    b, i = pl.program_id(0), pl.program_id(1)
    nb, nt = pl.num_programs(0), pl.num_programs(1)
    tm = x1_ref.shape[1]
    n_slots = y_hbm.shape[2]
    blk = win_scr.shape[1] // n_experts
    step = b * nt + i
    slot = step % 2

    def window(bb, ii, e):
        base = (bb * n_experts + e) * HEAD_LANES + ii
        start = jnp.minimum((off_ref[base] // WIN_ALIGN) * WIN_ALIGN, n_slots - win)
        return pl.multiple_of(start, WIN_ALIGN), off_ref[base + 1] - start <= WIN_SHORT

    def transfer(bb, ii, sl, go):
        for e in range(n_experts):
            start, short = window(bb, ii, e)
            for size, cond in ((WIN_SHORT, short), (win, jnp.logical_not(short))):
                @pl.when(cond)
                def _():
                    go(pltpu.make_async_copy(y_hbm.at[bb, e, pl.ds(start, size)],
                                             win_scr.at[sl, pl.ds(e * blk, size)], sem.at[sl, e]))

    @pl.when(step == 0)
    def _():
        win_scr[...] = jnp.zeros(win_scr.shape, win_scr.dtype)
        transfer(b, i, slot, lambda cp: cp.start())

    @pl.when(step + 1 < nb * nt)
    def _():
        wrap = i + 1 == nt
        transfer(jnp.where(wrap, b + 1, b), jnp.where(wrap, 0, i + 1), 1 - slot, lambda cp: cp.start())

    pos = pos_ref[0]
    scol = lax.broadcasted_iota(jnp.int32, (tm, blk), 1)
    onehot = jnp.concatenate([jnp.where((pos[:, e:e + 1] - window(b, i, e)[0]) == scol, 1.0, 0.0).astype(BF16)
                              for e in range(n_experts)], axis=1)
    transfer(b, i, slot, lambda cp: cp.wait())
    moe = jnp.dot(onehot, win_scr[slot], preferred_element_type=F32)
    mod = mod_ref[0, 0]
    o_ref[0] = _layer_norm(alpha * x1_ref[0] + mod[5:6] * moe, lnw_ref[...], lnb_ref[...])


def _combine(off_flat, x1, pos_tok, modtab, ln_w, ln_b, y, *, m_ctx, alpha, latent_only):
    b, l, d = x1.shape
    ne = pos_tok.shape[2]
    tm = COMBINE_TILE
    win = tm + WIN_ALIGN
    blk = -(-win // HEAD_LANES) * HEAD_LANES
    ctx_tiles = m_ctx // tm

    def rows(width):
        return pl.BlockSpec((1, tm, width), lambda bb, i, off: (bb, i, 0))

    out_rows = l - m_ctx if latent_only else l
    out_spec = (pl.BlockSpec((1, tm, d), lambda bb, i, off: (bb, jnp.maximum(i - ctx_tiles, 0), 0))
                if latent_only else rows(d))

    grid_spec = pltpu.PrefetchScalarGridSpec(
        num_scalar_prefetch=1,
        grid=(b, l // tm),
        in_specs=[rows(d), rows(ne),
                  pl.BlockSpec((1, 1, 6, d), lambda bb, i, off: (bb, jnp.minimum(i // ctx_tiles, 1), 0, 0)),
                  pl.BlockSpec((1, d), lambda bb, i, off: (0, 0)),
                  pl.BlockSpec((1, d), lambda bb, i, off: (0, 0)),
                  pl.BlockSpec(memory_space=pl.ANY)],
        out_specs=out_spec,
        scratch_shapes=[pltpu.VMEM((2, ne * blk, d), BF16), pltpu.SemaphoreType.DMA((2, ne))])
    return pl.pallas_call(
        functools.partial(_combine_kernel, alpha=alpha, n_experts=ne, win=win),
        grid_spec=grid_spec,
        out_shape=jax.ShapeDtypeStruct((b, out_rows, d), F32),
        compiler_params=_cparams(("arbitrary", "arbitrary")),
        name="moe_combine_ln",
    )(off_flat, x1, pos_tok, modtab, ln_w.reshape(1, d), ln_b.reshape(1, d), y)


def kernel(x, c, ctx, c_ctx, w_mod, b_mod, w_in, w_conv, lambda_qk, subln_w, hgrn_lb_logits, hgrn_norm_w, w_out,
           ln_w, ln_b, w_router, w_gate, w_up, w_down):
    bsz, n_lat, d = x.shape
    m_ctx = ctx.shape[1]
    depth = w_mod.shape[0]
    ne = w_router.shape[2]
    aw, cw, hw = d // 2, d // 4, d // 4
    assert m_ctx % TILE == 0 and n_lat % ATTN_KEY_CHUNK == 0 and n_lat % GRID_W == 0 and bsz + 1 <= 8
    assert (m_ctx + n_lat) % (8 * 16) == 0 and aw == 1024 and ne <= HEAD_LANES
    alpha = (2.0 * depth) ** 0.25
    cap_ctx = EC_CAPACITY_FACTOR * m_ctx // ne
    cap_lat = EC_CAPACITY_FACTOR * n_lat // ne
    n_slots = cap_ctx + cap_lat
    assert n_slots % WIN_ALIGN == 0 and n_slots >= COMBINE_TILE + WIN_ALIGN
    assert (m_ctx + n_lat) // COMBINE_TILE + 1 <= HEAD_LANES

    cs = jnp.zeros((8, d), F32).at[:bsz].set(c).at[bsz].set(c_ctx)
    mod = _modulation(cs, w_mod, b_mod)
    mod_lat = mod[:, :bsz].reshape(depth, bsz, 1, 6, d)
    mod_ctx = jnp.broadcast_to(mod[:, bsz].reshape(depth, 1, 1, 6, d), (depth, bsz, 1, 6, d))
    modtab = jnp.concatenate([mod_ctx, mod_lat], axis=2)

    tables = _rope_tables(m_ctx, n_lat)
    xa = jnp.concatenate([ctx, x], axis=1)
    w_router_pad = jnp.zeros((depth, d, HEAD_LANES), F32).at[:, :, :ne].set(w_router)
    hg_heads = hw // HEAD_LANES
    cb0 = 3 * cw // HEAD_LANES

    for l in range(depth):
        lambda_init = 0.8 - 0.6 * math.exp(-0.3 * l)
        w_in_bf = w_in[l].astype(BF16)
        qkv = _inproj(xa, modtab[l], w_in_bf, tables, col0=0, ncols=3 * aw, out_dtype=BF16, rope_tiles=2,
                      m_ctx=m_ctx)
        cg = _inproj(xa, modtab[l], w_in_bf, tables, col0=3 * aw, ncols=3 * cw + 5 * hw, out_dtype=F32,
                     rope_tiles=0, m_ctx=m_ctx)
        att = _attention(qkv, lambda_qk[l], subln_w[l], m_ctx=m_ctx, lambda_init=lambda_init)
        o_f, o_b = _hgrn(cg, hgrn_lb_logits, layer=l, hg_heads=hg_heads, col_q=cb0, col_i=cb0 + hg_heads,
                         col_ff=cb0 + 3 * hg_heads, col_fb=cb0 + 4 * hg_heads)
        x1, h2, aff = _outproj(xa, att, cg, o_f, o_b, w_conv[l], hgrn_norm_w[l], w_out[l].astype(BF16), modtab[l],
                               ln_w[l, 0], ln_b[l, 0], w_router_pad[l], m_ctx=m_ctx, alpha=alpha, n_experts=ne,
                               cw=cw, col_gg=(3 * cw + 2 * hw) // hw)
        pos, off, idx = _topk(jnp.swapaxes(aff, 1, 2), m_ctx=m_ctx, cap_ctx=cap_ctx, cap_lat=cap_lat)
        idx_flat = jnp.minimum(jnp.swapaxes(idx[:, :, :ne], 1, 2).reshape(-1), m_ctx + n_lat - 1)
        y = _expert_ffn(idx_flat, h2, w_gate, w_up, w_down, n_slots=n_slots, layer=l)
        xa = _combine(off.reshape(-1), x1, jnp.swapaxes(pos, 1, 2), modtab[l], ln_w[l, 1], ln_b[l, 1], y,
                      m_ctx=m_ctx, alpha=alpha, latent_only=l == depth - 1)
    return xa
```
